```python
import math
import jax, jax.numpy as jnp
from jax import lax
import numpy as np

D_MODEL = 1024
BATCH = 8
SEQ = 4096
DEPTH = 1
DEC_BATCH = 128
DEC_SEQ = 8
PAST_LEN = 8192
PAGE_SIZE = 128

POOL_WIDTH = D_MODEL // 2
POOL_WINDOWS = (2, 4, 8, 16)
POOL_GROUPS = len(POOL_WINDOWS)
POOL_GROUP_WIDTH = POOL_WIDTH // POOL_GROUPS
POOL_STATE = max(POOL_WINDOWS) - 1
HEAD_DIM = 64
N_HEADS = (D_MODEL - POOL_WIDTH) // HEAD_DIM
N_KV = 2
REP = N_HEADS // N_KV
NSA_WIDTH = N_HEADS * HEAD_DIM
KV_WIDTH = N_KV * HEAD_DIM
L_CMP = 32
STRIDE = 16
L_SEL = 64
TOP_N = 16
WINDOW = 512
N_BUCKETS = 32
MAX_DIST = 128
D_FF = ((8 * D_MODEL // 3 + 127) // 128) * 128
IN_WIDTH = POOL_WIDTH + NSA_WIDTH + 6 * KV_WIDTH + 3 * N_HEADS
ALPHA = (2.0 * DEPTH) ** 0.25
BETA = (8.0 * DEPTH) ** -0.25
Q_BLOCK = 64
LN_EPS = 1e-5
NEG = -1e30

kernel_name = "hymba_pool_nsa_macaron_decode_step"


def layer_norm(x, g, b):
    xf = x.astype(jnp.float32)
    mu = xf.mean(-1, keepdims=True)
    var = jnp.square(xf - mu).mean(-1, keepdims=True)
    return ((xf - mu) * lax.rsqrt(var + LN_EPS) * g + b).astype(x.dtype)


def swiglu(u, w_in, w_out):
    a, b = jnp.split(u @ w_in, 2, axis=-1)
    return (jax.nn.silu(a) * b) @ w_out


def rel_bucket(dist):
    n = jnp.maximum(dist, 0)
    exact = N_BUCKETS // 2
    nf = jnp.maximum(n, 1).astype(jnp.float32)
    large = exact + (jnp.log(nf / exact) / math.log(MAX_DIST / exact) * (N_BUCKETS - exact)).astype(jnp.int32)
    return jnp.where(n < exact, n, jnp.minimum(large, N_BUCKETS - 1))


def masked_softmax(s, mask):
    p = jax.nn.softmax(jnp.where(mask, s, NEG), axis=-1)
    return jnp.where(mask, p, 0.0)


def cover_matrix(n_c, n_blk):
    i = np.arange(n_c)[:, None]
    j = np.arange(n_blk)[None, :]
    start = i * STRIDE
    end = start + L_CMP - 1
    return jnp.asarray(((start <= (j + 1) * L_SEL - 1) & (end >= j * L_SEL)).astype(np.float32))


def pool_mix(u, prev, pos0, w_pool, pool_scale):
    P, T = prev.shape[1], u.shape[1]
    ext = jnp.concatenate([prev, u], axis=1)
    csum = jnp.cumsum(ext.astype(jnp.float32), axis=1)
    pos = pos0 - P + jnp.arange(P + T)
    outs = []
    for g, w in enumerate(POOL_WINDOWS):
        lo, hi = g * POOL_GROUP_WIDTH, (g + 1) * POOL_GROUP_WIDTH
        cs = csum[..., lo:hi]
        lagged = jnp.pad(cs, ((0, 0), (w, 0), (0, 0)))[:, :P + T]
        cnt = jnp.minimum(pos + 1, w).astype(jnp.float32)[None, :, None]
        mean = ((cs - lagged) / cnt)[:, P:]
        pooled = (mean - u[..., lo:hi].astype(jnp.float32)).astype(u.dtype)
        outs.append(pooled @ w_pool[g])
    return jnp.concatenate(outs, axis=-1) * pool_scale, ext[:, -POOL_STATE:]


def compress(k_full, w1, w2, cmp_pos):
    B, T, G, dk = k_full.shape
    lhs = k_full.transpose(0, 2, 1, 3).reshape(B * G, T, dk)
    w_blk = w1.reshape(L_CMP, HEAD_DIM, HEAD_DIM)
    h = lax.conv_general_dilated(lhs, w_blk, (STRIDE,), 'VALID', dimension_numbers=('NWC', 'WIO', 'NWC'))
    h = h + jnp.einsum('ld,ldh->h', cmp_pos, w_blk)
    h = jax.nn.gelu(h) @ w2
    return h.reshape(B, G, h.shape[1], dk)


def to_blocks(k_full):
    B, T, G, dk = k_full.shape
    pad = (-T) % L_SEL
    kp = jnp.pad(k_full, ((0, 0), (0, pad), (0, 0), (0, 0)))
    return kp.reshape(B, (T + pad) // L_SEL, L_SEL, G, dk).transpose(0, 3, 1, 2, 4)


def nsa_attend(q, gates, kc, vc, ks_blk, vs_blk, kw_pad, vw_pad, pos0, e0, rel_bias):
    B, T = q.shape[:2]
    qb = Q_BLOCK if T % Q_BLOCK == 0 else T
    n_c, n_blk = kc.shape[2], ks_blk.shape[2]
    top_n = min(TOP_N, n_blk)
    lw = WINDOW + qb - 1
    scale = HEAD_DIM ** -0.5
    cover = cover_matrix(n_c, n_blk)
    cmp_end = jnp.arange(n_c) * STRIDE + (L_CMP - 1)
    blk_ids = jnp.arange(n_blk)
    tab = rel_bias.reshape(N_KV, REP, N_BUCKETS)
    b_ix = jnp.arange(B)[:, None, None, None]
    g_ix = jnp.arange(N_KV)[None, :, None, None]
    g_ix5 = jnp.arange(N_KV)[None, :, None, None, None]
    r_ix5 = jnp.arange(REP)[None, None, :, None, None]

    def block(i):
        qs = i * qb
        qpos = pos0 + qs + jnp.arange(qb)
        qh = lax.dynamic_slice_in_dim(q, qs, qb, axis=1).reshape(B, qb, N_KV, REP, HEAD_DIM).transpose(0, 2, 3, 1, 4)
        gt = lax.dynamic_slice_in_dim(gates, qs, qb, axis=1)
        d_c = qpos[:, None] - cmp_end[None, :]
        s_c = jnp.einsum('bgrqd,bgcd->bgrqc', qh, kc).astype(jnp.float32) * scale + tab[:, :, rel_bucket(d_c)]
        p_c = masked_softmax(s_c, d_c >= 0)
        o_c = jnp.einsum('bgrqc,bgcd->bgrqd', p_c.astype(vc.dtype), vc)
        imp = jnp.einsum('bgrqc,cj->bgqj', p_c, cover)
        cur = (qpos // L_SEL)[:, None]
        forced = (blk_ids == 0) | (blk_ids == cur) | (blk_ids == cur - 1)
        imp = jnp.where(forced, 1e9, jnp.where(blk_ids <= cur, imp, -1e9))
        _, idx = lax.top_k(imp, top_n)
        ks = ks_blk[b_ix, g_ix, idx].reshape(B, N_KV, qb, top_n * L_SEL, HEAD_DIM)
        vs = vs_blk[b_ix, g_ix, idx].reshape(B, N_KV, qb, top_n * L_SEL, HEAD_DIM)
        kpos = (idx[..., None] * L_SEL + jnp.arange(L_SEL)).reshape(B, N_KV, qb, top_n * L_SEL)
        d_s = qpos[:, None] - kpos
        bias_s = tab[g_ix5, r_ix5, rel_bucket(d_s)[:, :, None]]
        s_s = jnp.einsum('bgrqd,bgqkd->bgrqk', qh, ks).astype(jnp.float32) * scale + bias_s
        p_s = masked_softmax(s_s, (d_s >= 0)[:, :, None])
        o_s = jnp.einsum('bgrqk,bgqkd->bgrqd', p_s.astype(vs.dtype), vs)
        start = qs + pos0 - e0 + 1
        kw = lax.dynamic_slice_in_dim(kw_pad, start, lw, axis=1)
        vw = lax.dynamic_slice_in_dim(vw_pad, start, lw, axis=1)
        kpos_w = pos0 + qs - WINDOW + 1 + jnp.arange(lw)
        d_w = qpos[:, None] - kpos_w[None, :]
        mask_w = (d_w >= 0) & (d_w < WINDOW) & (kpos_w >= 0)[None, :]
        s_w = jnp.einsum('bgrqd,bkgd->bgrqk', qh, kw).astype(jnp.float32) * scale + tab[:, :, rel_bucket(d_w)]
        p_w = masked_softmax(s_w, mask_w)
        o_w = jnp.einsum('bgrqk,bkgd->bgrqd', p_w.astype(vw.dtype), vw)
        g = gt.reshape(B, qb, 3, N_KV, REP).transpose(2, 0, 3, 4, 1)[..., None]
        o = g[0] * o_c + g[1] * o_s + g[2] * o_w
        return o.transpose(0, 3, 1, 2, 4).reshape(B, qb, NSA_WIDTH).astype(q.dtype)

    out = lax.map(block, jnp.arange(T // qb))
    return out.transpose(1, 0, 2, 3).reshape(B, T, NSA_WIDTH)


def decoder_layer(x, c, pos0, pool_prev, kc_past, vc_past, ks_past, vs_past, kw_prev, vw_prev, rel_bias, p):
    B, T, _ = x.shape
    ada = (jax.nn.silu(c) @ p['w_ada'] + p['b_ada']).reshape(B, 3, 3, 1, D_MODEL)
    shift, scl, gate = ada[:, :, 0], ada[:, :, 1], ada[:, :, 2]

    def modulate(h, s):
        return h * (1.0 + scl[:, s]) + shift[:, s]

    h = swiglu(modulate(x, 0), p['w_ffn1_in'], p['w_ffn1_out'])
    x = layer_norm(ALPHA * x + 0.5 * gate[:, 0] * h, p['ln_gain'][0], p['ln_bias'][0])

    u = modulate(x, 1)
    splits = list(np.cumsum([POOL_WIDTH, NSA_WIDTH] + [KV_WIDTH] * 6))
    u_pool, q, k_c, v_c, k_s, v_s, k_w, v_w, g_raw = jnp.split(u @ p['w_in'], splits, axis=-1)
    q = q.reshape(B, T, N_HEADS, HEAD_DIM)
    k_c, v_c, k_s, v_s, k_w, v_w = [a.reshape(B, T, N_KV, HEAD_DIM) for a in (k_c, v_c, k_s, v_s, k_w, v_w)]
    gates = jax.nn.sigmoid(g_raw).reshape(B, T, 3, N_HEADS)

    pool_out, pool_buf = pool_mix(u_pool, pool_prev, pos0, p['w_pool'], p['pool_scale'])

    kc_full = jnp.concatenate([kc_past, k_c], axis=1)
    vc_full = jnp.concatenate([vc_past, v_c], axis=1)
    kc_cmp = compress(kc_full, p['w_phi_k1'], p['w_phi_k2'], p['cmp_pos'])
    vc_cmp = compress(vc_full, p['w_phi_v1'], p['w_phi_v2'], p['cmp_pos'])
    ks_blk = to_blocks(jnp.concatenate([ks_past, k_s], axis=1))
    vs_blk = to_blocks(jnp.concatenate([vs_past, v_s], axis=1))
    kw_ext = jnp.concatenate([kw_prev, k_w], axis=1)
    vw_ext = jnp.concatenate([vw_prev, v_w], axis=1)
    e0 = pos0 - kw_prev.shape[1]
    pad_w = ((0, 0), (WINDOW, 0), (0, 0), (0, 0))
    nsa_out = nsa_attend(q, gates, kc_cmp, vc_cmp, ks_blk, vs_blk, jnp.pad(kw_ext, pad_w), jnp.pad(vw_ext, pad_w), pos0, e0, rel_bias)

    mix = jnp.concatenate([pool_out, nsa_out], axis=-1) @ p['w_out']
    x = layer_norm(ALPHA * x + gate[:, 1] * mix, p['ln_gain'][1], p['ln_bias'][1])

    h = swiglu(modulate(x, 2), p['w_ffn2_in'], p['w_ffn2_out'])
    x = layer_norm(ALPHA * x + 0.5 * gate[:, 2] * h, p['ln_gain'][2], p['ln_bias'][2])

    keep = min(WINDOW, pos0 + T)
    return x, (k_c, v_c, k_s, v_s, kw_ext[:, -keep:], vw_ext[:, -keep:], pool_buf)


def setup_inputs(seed: int = 0) -> dict:
    key = jax.random.key(seed)
    ks = iter(list(jax.random.split(key, 40)))

    def nrm(shape, scale):
        return jax.random.normal(next(ks), shape, jnp.float32) * scale

    n_pages = PAST_LEN // PAGE_SIZE
    n_used = DEC_BATCH * n_pages
    n_pool = (n_used * 5) // 4
    wbuf = min(WINDOW, PAST_LEN)
    kv_pool = (DEPTH, n_pool, PAGE_SIZE, N_KV, HEAD_DIM)
    page_table = jax.random.permutation(next(ks), n_pool)[:n_used].reshape(DEC_BATCH, n_pages).astype(jnp.int32)
    return {
        'x_prompt': nrm((BATCH, SEQ, D_MODEL), 1.0),
        'x_sample': nrm((DEC_BATCH, DEC_SEQ, D_MODEL), 1.0),
        'c_prompt': nrm((BATCH, D_MODEL), 1.0),
        'c_sample': nrm((DEC_BATCH, D_MODEL), 1.0),
        'cache_k_cmp': nrm(kv_pool, 1.0),
        'cache_v_cmp': nrm(kv_pool, 1.0),
        'cache_k_slc': nrm(kv_pool, 1.0),
        'cache_v_slc': nrm(kv_pool, 1.0),
        'page_table': page_table,
        'state_k_win': nrm((DEPTH, DEC_BATCH, wbuf, N_KV, HEAD_DIM), 1.0),
        'state_v_win': nrm((DEPTH, DEC_BATCH, wbuf, N_KV, HEAD_DIM), 1.0),
        'state_pool': nrm((DEPTH, DEC_BATCH, POOL_STATE, POOL_WIDTH), 1.0),
        'w_ada': nrm((DEPTH, D_MODEL, 9 * D_MODEL), 0.5 * D_MODEL ** -0.5),
        'b_ada': nrm((DEPTH, 9 * D_MODEL), 0.02),
        'ln_gain': 1.0 + nrm((DEPTH, 3, D_MODEL), 0.02),
        'ln_bias': nrm((DEPTH, 3, D_MODEL), 0.02),
        'w_ffn1_in': nrm((DEPTH, D_MODEL, 2 * D_FF), D_MODEL ** -0.5),
        'w_ffn1_out': nrm((DEPTH, D_FF, D_MODEL), BETA * D_FF ** -0.5),
        'w_ffn2_in': nrm((DEPTH, D_MODEL, 2 * D_FF), D_MODEL ** -0.5),
        'w_ffn2_out': nrm((DEPTH, D_FF, D_MODEL), BETA * D_FF ** -0.5),
        'w_in': nrm((DEPTH, D_MODEL, IN_WIDTH), D_MODEL ** -0.5),
        'w_phi_k1': nrm((DEPTH, L_CMP * HEAD_DIM, HEAD_DIM), (L_CMP * HEAD_DIM) ** -0.5),
        'w_phi_k2': nrm((DEPTH, HEAD_DIM, HEAD_DIM), HEAD_DIM ** -0.5),
        'w_phi_v1': nrm((DEPTH, L_CMP * HEAD_DIM, HEAD_DIM), (L_CMP * HEAD_DIM) ** -0.5),
        'w_phi_v2': nrm((DEPTH, HEAD_DIM, HEAD_DIM), HEAD_DIM ** -0.5),
        'cmp_pos': nrm((DEPTH, L_CMP, HEAD_DIM), 0.1),
        'w_pool': nrm((DEPTH, POOL_GROUPS, POOL_GROUP_WIDTH, POOL_GROUP_WIDTH), POOL_GROUP_WIDTH ** -0.5),
        'pool_scale': 1.0 + nrm((DEPTH, POOL_WIDTH), 0.02),
        'w_out': nrm((DEPTH, D_MODEL, D_MODEL), BETA * D_MODEL ** -0.5),
        'rel_bias': nrm((N_HEADS, N_BUCKETS), 0.5),
    }


def reference(x_prompt, x_sample, c_prompt, c_sample, cache_k_cmp, cache_v_cmp, cache_k_slc, cache_v_slc,
              page_table, state_k_win, state_v_win, state_pool, w_ada, b_ada, ln_gain, ln_bias,
              w_ffn1_in, w_ffn1_out, w_ffn2_in, w_ffn2_out, w_in, w_phi_k1, w_phi_k2, w_phi_v1, w_phi_v2,
              cmp_pos, w_pool, pool_scale, w_out, rel_bias):
    n_past = page_table.shape[1] * PAGE_SIZE

    def gather_pages(pool):
        return pool[page_table].reshape(page_table.shape[0], n_past, N_KV, HEAD_DIM)

    xp, xs = x_prompt, x_sample
    prompt_states, sample_states = [], []
    for l in range(DEPTH):
        p = dict(w_ada=w_ada[l], b_ada=b_ada[l], ln_gain=ln_gain[l], ln_bias=ln_bias[l],
                 w_ffn1_in=w_ffn1_in[l], w_ffn1_out=w_ffn1_out[l], w_ffn2_in=w_ffn2_in[l], w_ffn2_out=w_ffn2_out[l],
                 w_in=w_in[l], w_phi_k1=w_phi_k1[l], w_phi_k2=w_phi_k2[l], w_phi_v1=w_phi_v1[l], w_phi_v2=w_phi_v2[l],
                 cmp_pos=cmp_pos[l], w_pool=w_pool[l], pool_scale=pool_scale[l], w_out=w_out[l])
        empty_kv = jnp.zeros((xp.shape[0], 0, N_KV, HEAD_DIM), xp.dtype)
        empty_pool = jnp.zeros((xp.shape[0], 0, POOL_WIDTH), xp.dtype)
        xp, sp = decoder_layer(xp, c_prompt, 0, empty_pool, empty_kv, empty_kv, empty_kv, empty_kv,
                               empty_kv, empty_kv, rel_bias, p)
        xs, ss = decoder_layer(xs, c_sample, n_past, state_pool[l], gather_pages(cache_k_cmp[l]),
                               gather_pages(cache_v_cmp[l]), gather_pages(cache_k_slc[l]), gather_pages(cache_v_slc[l]),
                               state_k_win[l], state_v_win[l], rel_bias, p)
        prompt_states.append(sp)
        sample_states.append(ss)
    p_k_cmp, p_v_cmp, p_k_slc, p_v_slc, p_k_win, p_v_win, p_pool = [jnp.stack(a) for a in zip(*prompt_states)]
    s_k_cmp, s_v_cmp, s_k_slc, s_v_slc, s_k_win, s_v_win, s_pool = [jnp.stack(a) for a in zip(*sample_states)]
    return (xp, xs, p_k_cmp, p_v_cmp, p_k_slc, p_v_slc, p_k_win, p_v_win, p_pool,
            s_k_cmp, s_v_cmp, s_k_slc, s_v_slc, s_k_win, s_v_win, s_pool)
```

```python
import functools
import math

import numpy as np
import jax
import jax.numpy as jnp
from jax import lax
from jax.experimental import pallas as pl
from jax.experimental.pallas import tpu as pltpu

F32 = jnp.float32
BF16 = jnp.bfloat16

HEAD_DIM = 64
N_KV = 2
L_CMP = 32
STRIDE = 16
L_SEL = 64
TOP_N = 16
WINDOW = 512
N_BUCKETS = 32
MAX_DIST = 128
POOL_WINDOWS = (2, 4, 8, 16)
POOL_STATE = max(POOL_WINDOWS) - 1
LN_EPS = 1e-5
NEG = -1e30
PAGE_SIZE = 128

LANE = 128
FF_CHUNK = 256
VMEM_LIMIT = 56 * 1024 * 1024


def _cparams(n_axes):
    return pltpu.CompilerParams(dimension_semantics=("arbitrary",) * n_axes,
                                vmem_limit_bytes=VMEM_LIMIT)


def _const_spec(shape):
    nd = len(shape)
    return pl.BlockSpec(shape, lambda *_: (0,) * nd, pipeline_mode=pl.Buffered(1))


def _layer_norm(y, gain, bias):
    mu = jnp.mean(y, axis=-1, keepdims=True)
    yc = y - mu
    var = jnp.mean(yc * yc, axis=-1, keepdims=True)
    return yc * lax.rsqrt(var + LN_EPS) * gain + bias


def _swiglu(u_bf, wi_ref, wo_ref):
    d_ff = wo_ref.shape[0]
    acc = None
    for c in range(d_ff // FF_CHUNK):
        lo = c * FF_CHUNK
        a = jnp.dot(u_bf, wi_ref[:, lo:lo + FF_CHUNK], preferred_element_type=F32)
        b = jnp.dot(u_bf, wi_ref[:, d_ff + lo:d_ff + lo + FF_CHUNK], preferred_element_type=F32)
        hid = (a * jax.nn.sigmoid(a) * b).astype(BF16)
        part = jnp.dot(hid, wo_ref[lo:lo + FF_CHUNK, :], preferred_element_type=F32)
        acc = part if acc is None else acc + part
    return acc


def _ada_body(c_ref, w_ref, b_ref, o_ref):
    c = c_ref[...]
    act = (c * jax.nn.sigmoid(c)).astype(BF16)
    o_ref[...] = jnp.dot(act, w_ref[...].astype(BF16), preferred_element_type=F32) + b_ref[...]


def _ada(c_all, w_ada, b_ada):
    nb, d = c_all.shape
    n_out = w_ada.shape[1]
    cols = 9 * LANE
    return pl.pallas_call(
        _ada_body,
        grid=(n_out // cols,),
        in_specs=[pl.BlockSpec((nb, d), lambda j: (0, 0)),
                  pl.BlockSpec((d, cols), lambda j: (0, j)),
                  pl.BlockSpec((1, cols), lambda j: (0, j))],
        out_specs=pl.BlockSpec((nb, cols), lambda j: (0, j)),
        out_shape=jax.ShapeDtypeStruct((nb, n_out), F32),
        compiler_params=_cparams(1),
        name="ada",
    )(c_all, w_ada, b_ada.reshape(1, n_out))


KV_NAMES = ("k_c", "v_c", "k_s", "v_s", "k_w", "v_w")


def _ffn_in_body(x_ref, mod_ref, wi_ref, wo_ref, wp_ref, ln_ref, *out_refs, alpha, pool_w, nsa_w, kv_w, for_prompt):
    nbk, tt, d = x_ref.shape
    n = nbk * tt
    x = x_ref[...]
    mod = mod_ref[...]
    u = (x * (1.0 + mod[:, 1:2, :]) + mod[:, 0:1, :]).reshape(n, d).astype(BF16)
    h = _swiglu(u, wi_ref, wo_ref).reshape(nbk, tt, d)
    x1 = _layer_norm(alpha * x + 0.5 * mod[:, 2:3, :] * h, ln_ref[0:1, :], ln_ref[1:2, :])
    u1 = (x1 * (1.0 + mod[:, 4:5, :]) + mod[:, 3:4, :]).reshape(n, d).astype(BF16)
    proj = jnp.dot(u1, wp_ref[...], preferred_element_type=F32)

    x1_ref, upool_ref, q_ref = out_refs[:3]
    kv_refs = out_refs[3:9]
    x1_ref[...] = x1
    upool_ref[...] = proj[:, :pool_w].reshape(nbk, tt, pool_w)
    q_ref[...] = (proj[:, pool_w:pool_w + nsa_w] * (HEAD_DIM ** -0.5)).astype(BF16).reshape(nbk, tt, nsa_w)
    off = pool_w + nsa_w
    kv = []
    for i in range(6):
        blk = proj[:, off + i * kv_w:off + (i + 1) * kv_w]
        kv.append(blk)
        kv_refs[i][...] = blk.reshape(nbk, tt, kv_w)
    gates = jax.nn.sigmoid(proj[:, off + 6 * kv_w:off + 6 * kv_w + LANE])
    if for_prompt:
        ksb_ref, vst_ref, kwb_ref, vwt_ref, gt_ref = out_refs[9:]
        ksb_ref[0] = kv[2].astype(BF16)
        vst_ref[0] = kv[3].T.astype(BF16)
        kwb_ref[0] = kv[4].astype(BF16)
        vwt_ref[0] = kv[5].T.astype(BF16)
        gt_ref[0] = gates.T[:gt_ref.shape[1], :]
    else:
        out_refs[9][...] = gates.reshape(nbk, tt, LANE)


def _ffn_in(x, mod, wi, wo, wp, ln, *, alpha, nbk, tt, for_prompt):
    nb, t, d = x.shape
    pool_w = d // 2
    nsa_w = d - pool_w
    kv_w = N_KV * HEAD_DIM
    grid = (nb // nbk, t // tt)
    tok = lambda w: pl.BlockSpec((nbk, tt, w), lambda i, j: (i, j, 0))
    out_specs = [tok(d), tok(pool_w), tok(nsa_w)] + [tok(kv_w)] * 6
    out_shape = ([jax.ShapeDtypeStruct((nb, t, d), F32), jax.ShapeDtypeStruct((nb, t, pool_w), F32),
                  jax.ShapeDtypeStruct((nb, t, nsa_w), BF16)]
                 + [jax.ShapeDtypeStruct((nb, t, kv_w), F32)] * 6)
    if for_prompt:
        assert nbk == 1
        tr = lambda rows: pl.BlockSpec((1, rows, tt), lambda i, j: (i, 0, j))
        out_specs += [tok(kv_w), tr(kv_w), tok(kv_w), tr(kv_w), tr(2 * GATE_ROWS)]
        out_shape += [jax.ShapeDtypeStruct((nb, t, kv_w), BF16), jax.ShapeDtypeStruct((nb, kv_w, t), BF16),
                      jax.ShapeDtypeStruct((nb, t, kv_w), BF16), jax.ShapeDtypeStruct((nb, kv_w, t), BF16),
                      jax.ShapeDtypeStruct((nb, 2 * GATE_ROWS, t), F32)]
    else:
        out_specs += [tok(LANE)]
        out_shape += [jax.ShapeDtypeStruct((nb, t, LANE), F32)]
    body = functools.partial(_ffn_in_body, alpha=alpha, pool_w=pool_w, nsa_w=nsa_w, kv_w=kv_w,
                             for_prompt=for_prompt)
    return pl.pallas_call(
        body,
        grid=grid,
        in_specs=[tok(d),
                  pl.BlockSpec((nbk, mod.shape[1], d), lambda i, j: (i, 0, 0)),
                  _const_spec(wi.shape), _const_spec(wo.shape), _const_spec(wp.shape), _const_spec(ln.shape)],
        out_specs=out_specs,
        out_shape=out_shape,
        compiler_params=_cparams(2),
        name="ffn1_inproj_prompt" if for_prompt else "ffn1_inproj_sample",
    )(x, mod, wi, wo, wp, ln)


GATE_ROWS = 16


def _gate_column_source(n_heads):
    rep = n_heads // N_KV
    src = np.full((LANE,), -1, np.int32)
    for g in range(N_KV):
        for br in range(3):
            for r in range(rep):
                src[g * GATE_ROWS + br * rep + r] = br * n_heads + g * rep + r
    return src


def _ffn_out_body(pool_ref, nsa_ref, x1_ref, mod_ref, wout_ref, wi_ref, wo_ref, ln_ref, o_ref, *, alpha):
    nbk, tt, d = x1_ref.shape
    n = nbk * tt
    pw = pool_ref.shape[-1]
    mod = mod_ref[...]
    x1 = x1_ref[...]
    mix = (jnp.dot(pool_ref[...].reshape(n, pw), wout_ref[:pw, :], preferred_element_type=F32)
           + jnp.dot(nsa_ref[...].reshape(n, d - pw), wout_ref[pw:, :], preferred_element_type=F32))
    x2 = _layer_norm(alpha * x1 + mod[:, 5:6, :] * mix.reshape(nbk, tt, d), ln_ref[0:1, :], ln_ref[1:2, :])
    u = (x2 * (1.0 + mod[:, 7:8, :]) + mod[:, 6:7, :]).reshape(n, d).astype(BF16)
    h = _swiglu(u, wi_ref, wo_ref).reshape(nbk, tt, d)
    o_ref[...] = _layer_norm(alpha * x2 + 0.5 * mod[:, 8:9, :] * h, ln_ref[2:3, :], ln_ref[3:4, :])


def _ffn_out(pool_out, nsa_out, x1, mod, wout, wi, wo, ln, *, alpha, nbk, tt, name):
    nb, t, d = x1.shape
    tok = lambda w: pl.BlockSpec((nbk, tt, w), lambda i, j: (i, j, 0))
    return pl.pallas_call(
        functools.partial(_ffn_out_body, alpha=alpha),
        grid=(nb // nbk, t // tt),
        in_specs=[tok(pool_out.shape[-1]), tok(nsa_out.shape[-1]), tok(d),
                  pl.BlockSpec((nbk, mod.shape[1], d), lambda i, j: (i, 0, 0)),
                  _const_spec(wout.shape), _const_spec(wi.shape), _const_spec(wo.shape), _const_spec(ln.shape)],
        out_specs=tok(d),
        out_shape=jax.ShapeDtypeStruct((nb, t, d), F32),
        compiler_params=_cparams(2),
        name=name,
    )(pool_out, nsa_out, x1, mod, wout, wi, wo, ln)


PREV_ROWS = 16


def _pool_body(u_ref, prev_ref, w_ref, scale_ref, o_ref, ext_ref, *, pos0, zero_first):
    nbk, tt, width = u_ref.shape
    gw = width // len(POOL_WINDOWS)
    j = pl.program_id(1)
    u = u_ref[...]
    prev = prev_ref[...]
    if zero_first:
        prev = jnp.where(j == 0, 0.0, prev)
    ext_ref[:, 0:PREV_ROWS, :] = prev
    ext_ref[:, PREV_ROWS:, :] = u
    pos = pos0 + j * tt + lax.broadcasted_iota(jnp.int32, (1, tt, 1), 1)
    for g, w in enumerate(POOL_WINDOWS):
        lo = g * gw
        tot = None
        for k in range(w):
            part = ext_ref[:, PREV_ROWS - k:PREV_ROWS - k + tt, lo:lo + gw]
            tot = part if tot is None else tot + part
        cnt = jnp.minimum(pos + 1, w).astype(F32)
        pooled = (tot / cnt - u[:, :, lo:lo + gw]).astype(BF16).reshape(nbk * tt, gw)
        mixed = jnp.dot(pooled, w_ref[g], preferred_element_type=F32) * scale_ref[:, lo:lo + gw]
        o_ref[:, :, lo:lo + gw] = mixed.astype(BF16).reshape(nbk, tt, gw)


def _pool_mix(u_pool, prev, w_pool_bf, pool_scale, *, pos0, nbk, tt, name):
    nb, t, width = u_pool.shape
    tok = pl.BlockSpec((nbk, tt, width), lambda i, j: (i, j, 0))
    if prev is None:
        assert nbk == 1 and tt % PREV_ROWS == 0 and pos0 == 0
        step = tt // PREV_ROWS
        prev_arr = u_pool
        prev_spec = pl.BlockSpec((1, PREV_ROWS, width), lambda i, j: (i, jnp.maximum(j * step - 1, 0), 0))
    else:
        assert t == tt
        prev_arr = prev
        prev_spec = pl.BlockSpec((nbk, PREV_ROWS, width), lambda i, j: (i, 0, 0))
    return pl.pallas_call(
        functools.partial(_pool_body, pos0=pos0, zero_first=prev is None),
        grid=(nb // nbk, t // tt),
        in_specs=[tok, prev_spec, _const_spec(w_pool_bf.shape), _const_spec((1, width))],
        out_specs=tok,
        out_shape=jax.ShapeDtypeStruct((nb, t, width), BF16),
        scratch_shapes=[pltpu.VMEM((nbk, PREV_ROWS + tt, width), F32)],
        compiler_params=_cparams(2),
        name=name,
    )(u_pool, prev_arr, w_pool_bf, pool_scale.reshape(1, width))


ROW_TOKENS = STRIDE


def _compress_body(r_ref, wexp_ref, w2_ref, pos_ref, o_ref, sh_ref, *, transposed):
    nr = r_ref.shape[1]
    half = wexp_ref.shape[1] // 2
    p = jnp.dot(r_ref[0].astype(BF16), wexp_ref[...], preferred_element_type=F32)
    pp = jnp.dot(pos_ref[...], wexp_ref[...], preferred_element_type=F32)
    posb = pp[0:1, :half] + pp[1:2, half:]
    sh_ref[0:nr, :] = p[:, half:]
    sh_ref[nr:nr + 8, :] = jnp.zeros((8, half), F32)
    h = p[:, :half] + sh_ref[1:nr + 1, :] + posb
    c = jnp.dot(jax.nn.gelu(h).astype(BF16), w2_ref[...], preferred_element_type=F32)
    o_ref[0] = (c.T if transposed else c).astype(BF16)


def _compress(rows, wexp, w2bd, posrows, *, transposed, name):
    nb, nr, k = rows.shape
    half = wexp.shape[1] // 2
    oshape = (nb, half, nr) if transposed else (nb, nr, half)
    return pl.pallas_call(
        functools.partial(_compress_body, transposed=transposed),
        grid=(nb,),
        in_specs=[pl.BlockSpec((1, nr, k), lambda b: (b, 0, 0)),
                  _const_spec(wexp.shape), _const_spec(w2bd.shape), _const_spec(posrows.shape)],
        out_specs=pl.BlockSpec((1,) + oshape[1:], lambda b: (b, 0, 0)),
        out_shape=jax.ShapeDtypeStruct(oshape, BF16),
        scratch_shapes=[pltpu.VMEM((nr + 8, half), F32)],
        compiler_params=_cparams(1),
        name=name,
    )(rows, wexp, w2bd, posrows)


def _compress_weights(w1, w2, cmp_pos):
    eye = jnp.eye(N_KV, dtype=F32)
    w_blk = w1.reshape(L_CMP, HEAD_DIM, HEAD_DIM)
    halves = []
    for part in range(L_CMP // ROW_TOKENS):
        wpart = w_blk[part * ROW_TOKENS:(part + 1) * ROW_TOKENS]
        halves.append(jnp.einsum("gh,lio->lgiho", eye, wpart).reshape(ROW_TOKENS * N_KV * HEAD_DIM, N_KV * HEAD_DIM))
    wexp = jnp.concatenate(halves, axis=1).astype(BF16)
    w2bd = jnp.einsum("gh,io->giho", eye, w2).reshape(N_KV * HEAD_DIM, N_KV * HEAD_DIM).astype(BF16)
    pos = jnp.broadcast_to(cmp_pos.reshape(L_CMP // ROW_TOKENS, ROW_TOKENS, 1, HEAD_DIM),
                           (L_CMP // ROW_TOKENS, ROW_TOKENS, N_KV, HEAD_DIM)).reshape(L_CMP // ROW_TOKENS, -1)
    posrows = jnp.zeros((8, pos.shape[1]), F32).at[:pos.shape[0]].set(pos).astype(BF16)
    return wexp, w2bd, posrows


def _rel_bucket(dist):
    n = jnp.maximum(dist, 0)
    exact = N_BUCKETS // 2
    nf = jnp.maximum(n, 1).astype(F32)
    large = exact + (jnp.log(nf / exact) / math.log(MAX_DIST / exact) * (N_BUCKETS - exact)).astype(jnp.int32)
    return jnp.where(n < exact, n, jnp.minimum(large, N_BUCKETS - 1))


def _bias_minus_far(rel_bias, dist):
    far = rel_bias[:, _rel_bucket(jnp.asarray(1 << 30, jnp.int32))]
    return rel_bias[:, _rel_bucket(dist)] - far.reshape((-1,) + (1,) * dist.ndim)


TQ = 128


def _prompt_tables(rel_bias, t):
    h = rel_bias.shape[0]
    qo = jnp.arange(TQ, dtype=jnp.int32)
    n_near = WINDOW // L_SEL + TQ // L_SEL
    dd = (WINDOW // L_SEL) - jnp.arange(n_near, dtype=jnp.int32)
    ko = jnp.arange(L_SEL, dtype=jnp.int32)
    d = (L_SEL * dd[:, None, None] + qo[None, None, :] - ko[None, :, None]).reshape(n_near * L_SEL, TQ)
    ok = (d >= 0) & (d < WINDOW)
    near = jnp.where(ok[None], _bias_minus_far(rel_bias, d), NEG)
    near = near.transpose(1, 0, 2).reshape(n_near * L_SEL, h * TQ)
    n_tiles = t // TQ
    per_tile = TQ // STRIDE
    e0 = per_tile * (n_tiles - 1)
    rows = e0 + t // STRIDE
    e = e0 - jnp.arange(rows, dtype=jnp.int32)
    dc = STRIDE * e[:, None] - (L_CMP - 1) + qo[None, :]
    cmp = jnp.where((dc >= 0)[None], _bias_minus_far(rel_bias, dc), NEG)
    cmp = cmp.transpose(1, 0, 2).reshape(rows, h * TQ)
    return near.astype(F32), cmp.astype(F32), e0


def _cover_t(n_rows, n_c, n_blk):
    i = np.arange(n_rows)[None, :]
    j = np.arange(n_blk)[:, None]
    start = i * STRIDE
    end = start + L_CMP - 1
    cov = (start <= (j + 1) * L_SEL - 1) & (end >= j * L_SEL) & (i < n_c)
    return jnp.asarray(cov.astype(np.float32), BF16)


def _descending_rank(v, n_valid):
    rows, cols = v.shape
    sub = lax.broadcasted_iota(jnp.int32, (8, cols), 0)
    rank = jnp.zeros((rows, cols), F32)
    for jp in range(n_valid):
        other = v[jp:jp + 1, :]
        lo = (jp // 8) * 8
        mid = v[lo:lo + 8]
        parts = [jnp.where(other > mid, 1.0, jnp.where((other == mid) & (sub > jp - lo), 1.0, 0.0))]
        if lo > 0:
            parts.insert(0, jnp.where(other > v[:lo], 1.0, 0.0))
        if lo + 8 < rows:
            parts.append(jnp.where(other >= v[lo + 8:], 1.0, 0.0))
        rank = rank + jnp.concatenate(parts, axis=0)
    return rank


def _nsa_prompt_body(q_ref, gt_ref, kc_ref, vct_ref, ks_ref, vst_ref, kw_ref, vwt_ref, tcmp_ref, near_ref, cov_ref,
                     o_ref, s_ref, m8_ref, l8_ref, acc_ref, *, e0, top_n):
    g = pl.program_id(1)
    it = pl.program_id(2)
    t = ks_ref.shape[1]
    n_tiles = t // TQ
    hd = HEAD_DIM
    rep = q_ref.shape[2] // hd
    cols = rep * TQ
    n_blk = cov_ref.shape[0]
    g_off = pl.multiple_of(g * hd, hd)

    qt = q_ref[0].astype(F32).T
    row = lax.broadcasted_iota(jnp.int32, (N_KV * hd, TQ), 0)
    mine = (row >= hd) == (g == 1)
    tiles = []
    for r in range(rep):
        blk = qt[r * hd:(r + 1) * hd, :]
        tiles.append(jnp.where(mine, jnp.concatenate([blk] * N_KV, axis=0), 0.0))
    qbd = jnp.concatenate(tiles, axis=1).astype(BF16)

    gate = [jnp.concatenate([gt_ref[0, br * rep + r:br * rep + r + 1, :] for r in range(rep)], axis=1)
            for br in range(3)]

    n_ck = kc_ref.shape[1]
    sc = jnp.dot(kc_ref[0], qbd, preferred_element_type=F32)
    t0 = pl.multiple_of(e0 - (TQ // STRIDE) * it, 8)
    sc = sc + tcmp_ref[pl.ds(t0, n_ck), :]
    mc = jnp.max(sc, axis=0, keepdims=True)
    pc = jnp.where(sc > 0.1 * NEG, jnp.exp(sc - mc), 0.0)
    lc = jnp.sum(pc, axis=0, keepdims=True)
    pn = (pc * jnp.where(lc > 0.0, 1.0 / lc, 0.0)).astype(BF16)
    oc = jnp.dot(vct_ref[0], pn, preferred_element_type=F32)
    oc = jnp.where(g == 0, oc[:hd], oc[hd:])

    imp = jnp.dot(cov_ref[...], pn, preferred_element_type=F32)
    v = imp[:, 0:TQ]
    for r in range(1, rep):
        v = v + imp[:, r * TQ:(r + 1) * TQ]
    j = lax.broadcasted_iota(jnp.int32, (n_blk, TQ), 0)
    lane = lax.broadcasted_iota(jnp.int32, (n_blk, TQ), 1)
    cur = (TQ // L_SEL) * it + lane // L_SEL
    forced = (j == 0) | (j == cur) | (j == cur - 1)
    v = jnp.where(forced, 1e9, jnp.where(j <= cur, v, -1e9))
    rank = _descending_rank(v, n_blk)
    selb = jnp.where((rank < top_n) & (j <= cur), 0.0, NEG)
    selb = jnp.concatenate([selb] * rep, axis=1)

    per = TQ // L_SEL
    for c in range(n_tiles):
        @pl.when(c <= it)
        def _(c=c):
            s = jnp.dot(ks_ref[0, c * TQ:(c + 1) * TQ, :], qbd, preferred_element_type=F32)
            bias = jnp.concatenate(
                [jnp.broadcast_to(selb[per * c + k:per * c + k + 1, :], (L_SEL, cols)) for k in range(per)], axis=0)
            s_ref[c * TQ:(c + 1) * TQ, :] = s + bias
    n_near = near_ref.shape[0]
    off = pl.multiple_of(it * TQ, TQ)
    s_ref[pl.ds(off, TQ), :] = s_ref[pl.ds(off, TQ), :] + near_ref[n_near - TQ:, :]

    @pl.when(it >= 1)
    def _():
        offp = pl.multiple_of((it - 1) * TQ, TQ)
        s_ref[pl.ds(offp, TQ), :] = s_ref[pl.ds(offp, TQ), :] + near_ref[n_near - 2 * TQ:n_near - TQ, :]

    m8_ref[...] = jnp.full(m8_ref.shape, NEG, F32)
    for c in range(n_tiles):
        @pl.when(c <= it)
        def _(c=c):
            s = s_ref[c * TQ:(c + 1) * TQ, :]
            m8_ref[...] = jnp.maximum(m8_ref[...], jnp.max(s.reshape(TQ // 8, 8, cols), axis=0))
    ms = jnp.max(m8_ref[...], axis=0, keepdims=True)

    l8_ref[...] = jnp.zeros(l8_ref.shape, F32)
    acc_ref[...] = jnp.zeros(acc_ref.shape, F32)
    for c in range(n_tiles):
        @pl.when(c <= it)
        def _(c=c):
            p = jnp.exp(s_ref[c * TQ:(c + 1) * TQ, :] - ms)
            l8_ref[...] = l8_ref[...] + jnp.sum(p.reshape(TQ // 8, 8, cols), axis=0)
            acc_ref[...] = acc_ref[...] + jnp.dot(vst_ref[0, pl.ds(g_off, hd), c * TQ:(c + 1) * TQ], p.astype(BF16),
                                                  preferred_element_type=F32)
    ls = jnp.sum(l8_ref[...], axis=0, keepdims=True)

    n_wc = n_near // TQ
    sw = []
    offs = []
    for k in range(n_wc):
        kidx = it - (n_wc - 1) + k
        offk = pl.multiple_of(jnp.maximum(kidx, 0) * TQ, TQ)
        offs.append(offk)
        s = jnp.dot(kw_ref[0, pl.ds(offk, TQ), :], qbd, preferred_element_type=F32)
        sw.append(s + near_ref[k * TQ:(k + 1) * TQ, :] + jnp.where(kidx >= 0, 0.0, NEG))
    sw = jnp.concatenate(sw, axis=0)
    mw = jnp.max(sw, axis=0, keepdims=True)
    pw = jnp.exp(sw - mw)
    lw = jnp.sum(pw, axis=0, keepdims=True)
    pw = pw.astype(BF16)
    ow = None
    for k in range(n_wc):
        part = jnp.dot(vwt_ref[0, pl.ds(g_off, hd), pl.ds(offs[k], TQ)], pw[k * TQ:(k + 1) * TQ, :],
                       preferred_element_type=F32)
        ow = part if ow is None else ow + part

    out_t = gate[0] * oc + (gate[1] / ls) * acc_ref[...] + (gate[2] / lw) * ow
    pieces = []
    for r in range(0, rep, 2):
        pair = jnp.concatenate([out_t[:, r * TQ:(r + 1) * TQ], out_t[:, (r + 1) * TQ:(r + 2) * TQ]], axis=0)
        pieces.append(pair.T)
    o_ref[0] = jnp.concatenate(pieces, axis=1).astype(BF16)


def _nsa_prompt(q, gt, kc, vct, ks, vst, kw, vwt, tcmp, near, cov_t, *, e0, top_n):
    nb, t, nsa_w = q.shape
    gw = nsa_w // N_KV
    kv_w = ks.shape[2]
    n_tiles = t // TQ
    cols = (gw // HEAD_DIM) * TQ
    per_b = lambda shape: pl.BlockSpec((1,) + shape, lambda b, g, i: (b, 0, 0))
    return pl.pallas_call(
        functools.partial(_nsa_prompt_body, e0=e0, top_n=top_n),
        grid=(nb, N_KV, n_tiles),
        in_specs=[pl.BlockSpec((1, TQ, gw), lambda b, g, i: (b, i, g)),
                  pl.BlockSpec((1, GATE_ROWS, TQ), lambda b, g, i: (b, g, i)),
                  per_b(kc.shape[1:]), per_b(vct.shape[1:]),
                  per_b((t, kv_w)), per_b((kv_w, t)), per_b((t, kv_w)), per_b((kv_w, t)),
                  pl.BlockSpec((tcmp.shape[0], cols), lambda b, g, i: (0, g)),
                  pl.BlockSpec((near.shape[0], cols), lambda b, g, i: (0, g)),
                  pl.BlockSpec(cov_t.shape, lambda b, g, i: (0, 0))],
        out_specs=pl.BlockSpec((1, TQ, gw), lambda b, g, i: (b, i, g)),
        out_shape=jax.ShapeDtypeStruct((nb, t, nsa_w), BF16),
        scratch_shapes=[pltpu.VMEM((t, cols), F32), pltpu.VMEM((8, cols), F32), pltpu.VMEM((8, cols), F32),
                        pltpu.VMEM((HEAD_DIM, cols), F32)],
        compiler_params=_cparams(3),
        name="nsa_prompt",
    )(q, gt, kc, vct, ks, vst, kw, vwt, tcmp, near, cov_t)


def _dot_nt(a, b):
    return lax.dot_general(a, b, (((1,), (1,)), ((), ())), preferred_element_type=F32)


def _page_copies(pt_ref, pools, bufs, sems, batch, slot, rows):
    n_pages = pt_ref.shape[1]
    out = []
    for i, (pool, buf) in enumerate(zip(pools, bufs)):
        for p in range(n_pages):
            out.append(pltpu.make_async_copy(pool.at[pt_ref[batch, p]], buf.at[slot, pl.ds(p * rows, rows), :],
                                             sems.at[i, slot]))
    return out


def _gather_step(pt_ref, pools, bufs, sems, rows):
    b = pl.program_id(0)
    slot = b % 2

    @pl.when(b == 0)
    def _():
        for cp in _page_copies(pt_ref, pools, bufs, sems, 0, 0, rows):
            cp.start()

    @pl.when(b + 1 < pl.num_programs(0))
    def _():
        for cp in _page_copies(pt_ref, pools, bufs, sems, b + 1, 1 - slot, rows):
            cp.start()

    for cp in _page_copies(pt_ref, pools, bufs, sems, b, slot, rows):
        cp.wait()
    return slot


def _compress_rows(rows_f32, wexp_ref, w2_ref, pos_ref, sh_ref):
    nr = rows_f32.shape[0]
    half = wexp_ref.shape[1] // 2
    p = jnp.dot(rows_f32.astype(BF16), wexp_ref[...], preferred_element_type=F32)
    pp = jnp.dot(pos_ref[...], wexp_ref[...], preferred_element_type=F32)
    posb = pp[0:1, :half] + pp[1:2, half:]
    sh_ref[0:nr, :] = p[:, half:]
    sh_ref[nr:nr + 8, :] = jnp.zeros((8, half), F32)
    h = p[:, :half] + sh_ref[1:nr + 1, :] + posb
    return jnp.dot(jax.nn.gelu(h).astype(BF16), w2_ref[...], preferred_element_type=F32).astype(BF16)


def _sample_cmp_body(pt_ref, kpool_ref, vpool_ref, qbd_ref, tab_ref, wk_ref, w2k_ref, pk_ref, wv_ref, w2v_ref, pv_ref,
                     cov_ref, oc_ref, imp_ref, kbuf, vbuf, sems, sh_ref, *, rep, t_new):
    slot = _gather_step(pt_ref, (kpool_ref, vpool_ref), (kbuf, vbuf), sems, PAGE_SIZE // ROW_TOKENS)
    kc = _compress_rows(kbuf[slot], wk_ref, w2k_ref, pk_ref, sh_ref)
    vc = _compress_rows(vbuf[slot], wv_ref, w2v_ref, pv_ref, sh_ref)
    s = _dot_nt(qbd_ref[0], kc) + tab_ref[...]
    m = jnp.max(s, axis=1, keepdims=True)
    p = jnp.where(s > 0.1 * NEG, jnp.exp(s - m), 0.0)
    l = jnp.sum(p, axis=1, keepdims=True)
    pn = (p * jnp.where(l > 0.0, 1.0 / l, 0.0)).astype(BF16)
    oc_ref[0] = jnp.dot(pn, vc, preferred_element_type=F32)
    imp = jnp.dot(pn, cov_ref[...], preferred_element_type=F32)
    per_g = rep * t_new
    parts = []
    for g in range(N_KV):
        acc = imp[g * per_g:g * per_g + t_new]
        for r in range(1, rep):
            acc = acc + imp[g * per_g + r * t_new:g * per_g + (r + 1) * t_new]
        parts.append(acc)
    imp_ref[0] = jnp.concatenate(parts, axis=0)


def _sample_cmp(page_table, kpool_rows, vpool_rows, qbd, tab, cmp_k, cmp_v, cov, *, rep, t_new):
    nb, n_pages = page_table.shape
    rows_per_page = PAGE_SIZE // ROW_TOKENS
    n_rows = n_pages * rows_per_page
    row_w = kpool_rows.shape[-1]
    n_q = qbd.shape[1]
    kvw = N_KV * HEAD_DIM
    const = lambda a: pl.BlockSpec(a.shape, lambda b, pt: (0,) * a.ndim, pipeline_mode=pl.Buffered(1))
    grid_spec = pltpu.PrefetchScalarGridSpec(
        num_scalar_prefetch=1,
        grid=(nb,),
        in_specs=[pl.BlockSpec(memory_space=pl.ANY), pl.BlockSpec(memory_space=pl.ANY),
                  pl.BlockSpec((1, n_q, kvw), lambda b, pt: (b, 0, 0)), const(tab),
                  const(cmp_k[0]), const(cmp_k[1]), const(cmp_k[2]), const(cmp_v[0]), const(cmp_v[1]), const(cmp_v[2]),
                  const(cov)],
        out_specs=[pl.BlockSpec((1, n_q, kvw), lambda b, pt: (b, 0, 0)),
                   pl.BlockSpec((1, N_KV * t_new, cov.shape[1]), lambda b, pt: (b, 0, 0))],
        scratch_shapes=[pltpu.VMEM((2, n_rows, row_w), F32), pltpu.VMEM((2, n_rows, row_w), F32),
                        pltpu.SemaphoreType.DMA((2, 2)), pltpu.VMEM((n_rows + 8, kvw), F32)],
    )
    return pl.pallas_call(
        functools.partial(_sample_cmp_body, rep=rep, t_new=t_new),
        grid_spec=grid_spec,
        out_shape=[jax.ShapeDtypeStruct((nb, n_q, kvw), F32),
                   jax.ShapeDtypeStruct((nb, N_KV * t_new, cov.shape[1]), F32)],
        compiler_params=_cparams(1),
        name="sample_cmp",
    )(page_table, kpool_rows, vpool_rows, qbd, tab, *cmp_k, *cmp_v, cov)


def _select_body(imp_ref, o_ref, *, top_n, n_blk, pos0, t_new):
    rows, cols = imp_ref.shape
    j = lax.broadcasted_iota(jnp.int32, (rows, cols), 0)
    col = pl.program_id(0) * cols + lax.broadcasted_iota(jnp.int32, (rows, cols), 1)
    cur = (pos0 + col % t_new) // L_SEL
    forced = (j == 0) | (j == cur) | (j == cur - 1)
    v = jnp.where(forced, 1e9, jnp.where(j <= cur, imp_ref[...], -1e9))
    v = jnp.where(j < n_blk, v, -3e38)
    rank = _descending_rank(v, n_blk)
    o_ref[...] = jnp.where((rank < top_n) & (j <= cur) & (j < n_blk), 0.0, NEG)


def _select_blocks(imp_t, *, top_n, n_blk, pos0, t_new):
    rows, cols = imp_t.shape
    tile = 4 * LANE
    return pl.pallas_call(
        functools.partial(_select_body, top_n=top_n, n_blk=n_blk, pos0=pos0, t_new=t_new),
        grid=(cols // tile,),
        in_specs=[pl.BlockSpec((rows, tile), lambda i: (0, i))],
        out_specs=pl.BlockSpec((rows, tile), lambda i: (0, i)),
        out_shape=jax.ShapeDtypeStruct((rows, cols), F32),
        compiler_params=_cparams(1),
        name="sample_select",
    )(imp_t)


def _softmax_pv(parts, hd2):
    m = None
    for s, _ in parts:
        mk = jnp.max(s, axis=1, keepdims=True)
        m = mk if m is None else jnp.maximum(m, mk)
    l = None
    o = None
    for s, v in parts:
        p = jnp.exp(s - m)
        lk = jnp.sum(p, axis=1, keepdims=True)
        ok = jnp.dot(p.astype(BF16), v, preferred_element_type=F32)
        l = lk if l is None else l + lk
        o = ok if o is None else o + ok
    return o / l


def _sample_slc_body(pt_ref, kpool_ref, vpool_ref, qbd_ref, selb_ref, gate_ref, oc_ref, ksn_ref, vsn_ref, kwn_ref,
                     vwn_ref, kwin_ref, vwin_ref, near_ref, newtab_ref, wtab_ref, expand_ref,
                     o_ref, kbuf, vbuf, sems, pad_ref, *, rep, t_new):
    b = pl.program_id(0)
    slot = _gather_step(pt_ref, (kpool_ref, vpool_ref), (kbuf, vbuf), sems, PAGE_SIZE)
    hd = HEAD_DIM
    kvw = N_KV * hd

    @pl.when(b == 0)
    def _():
        pad_ref[...] = jnp.zeros(pad_ref.shape, F32)

    news = []
    for i, ref in enumerate((ksn_ref, vsn_ref, kwn_ref, vwn_ref)):
        pad_ref[i, 0:t_new, :] = ref[0]
        news.append(pad_ref[i].astype(BF16))
    ks_new, vs_new, kw_new, vw_new = news
    qbd = qbd_ref[0]

    per_g = rep * t_new
    sel = selb_ref[0]
    sel_rows = jnp.concatenate([sel[g * t_new:(g + 1) * t_new] for g in range(N_KV) for _ in range(rep)], axis=0)
    s_past = (_dot_nt(qbd, kbuf[slot].astype(BF16))
              + jnp.dot(sel_rows.astype(BF16), expand_ref[...], preferred_element_type=F32))
    n_past = s_past.shape[1]
    n_near = near_ref.shape[1]
    s_past = jnp.concatenate([s_past[:, :n_past - n_near], s_past[:, n_past - n_near:] + near_ref[...]], axis=1)
    s_new = _dot_nt(qbd, ks_new) + newtab_ref[0]
    o_s = _softmax_pv([(s_past, vbuf[slot].astype(BF16)), (s_new, vs_new)], kvw)

    s_win = _dot_nt(qbd, kwin_ref[0].astype(BF16)) + wtab_ref[...]
    s_wnew = _dot_nt(qbd, kw_new) + newtab_ref[1]
    o_w = _softmax_pv([(s_win, vwin_ref[0].astype(BF16)), (s_wnew, vw_new)], kvw)

    gate = gate_ref[0]
    comb = gate[:, 0:1] * oc_ref[0] + gate[:, 1:2] * o_s + gate[:, 2:3] * o_w
    row = lax.broadcasted_iota(jnp.int32, comb.shape, 0)
    o_ref[0] = jnp.where(row < per_g, comb, pltpu.roll(comb, hd, axis=1))


def _sample_slc(page_table, kpool, vpool, qbd, selb, gate_rows, oc, ks_new, vs_new, kw_new, vw_new, kwin, vwin,
                near, newtab, wtab, expand, *, rep, t_new):
    nb, n_pages = page_table.shape
    n_keys = n_pages * PAGE_SIZE
    n_q = qbd.shape[1]
    kvw = N_KV * HEAD_DIM
    const = lambda a: pl.BlockSpec(a.shape, lambda b, pt: (0,) * a.ndim, pipeline_mode=pl.Buffered(1))
    per_b = lambda a: pl.BlockSpec((1,) + a.shape[1:], lambda b, pt: (b,) + (0,) * (a.ndim - 1))
    grid_spec = pltpu.PrefetchScalarGridSpec(
        num_scalar_prefetch=1,
        grid=(nb,),
        in_specs=[pl.BlockSpec(memory_space=pl.ANY), pl.BlockSpec(memory_space=pl.ANY),
                  per_b(qbd), per_b(selb), per_b(gate_rows), per_b(oc),
                  per_b(ks_new), per_b(vs_new), per_b(kw_new), per_b(vw_new), per_b(kwin), per_b(vwin),
                  const(near), const(newtab), const(wtab), const(expand)],
        out_specs=pl.BlockSpec((1, n_q, kvw), lambda b, pt: (b, 0, 0)),
        scratch_shapes=[pltpu.VMEM((2, n_keys, kvw), F32), pltpu.VMEM((2, n_keys, kvw), F32),
                        pltpu.SemaphoreType.DMA((2, 2)), pltpu.VMEM((4, LANE, kvw), F32)],
    )
    return pl.pallas_call(
        functools.partial(_sample_slc_body, rep=rep, t_new=t_new),
        grid_spec=grid_spec,
        out_shape=jax.ShapeDtypeStruct((nb, n_q, kvw), F32),
        compiler_params=_cparams(1),
        name="sample_slc_win",
    )(page_table, kpool, vpool, qbd, selb, gate_rows, oc, ks_new, vs_new, kw_new, vw_new, kwin, vwin,
      near, newtab, wtab, expand)


def _sample_tables(rel_bias, pos0, t_new, n_rows, n_c, wbuf):
    h = rel_bias.shape[0]
    qpos = pos0 + jnp.arange(t_new, dtype=jnp.int32)

    def table(kpos, ok_extra=None, upper=None):
        d = qpos[:, None] - kpos[None, :]
        ok = d >= 0
        if upper is not None:
            ok = ok & (d < upper)
        if ok_extra is not None:
            ok = ok & ok_extra[None, :]
        tab = jnp.where(ok[None], _bias_minus_far(rel_bias, d), NEG)
        return tab.reshape(h * t_new, kpos.shape[0]).astype(F32)

    c = jnp.arange(n_rows, dtype=jnp.int32)
    cmp = table(c * STRIDE + (L_CMP - 1), ok_extra=c < n_c)
    near = table(pos0 - PAGE_SIZE + jnp.arange(PAGE_SIZE, dtype=jnp.int32))
    lane = jnp.arange(LANE, dtype=jnp.int32)
    new_slc = table(pos0 + lane, ok_extra=lane < t_new)
    new_win = table(pos0 + lane, ok_extra=lane < t_new, upper=WINDOW)
    kpos_w = pos0 - wbuf + jnp.arange(wbuf, dtype=jnp.int32)
    win = table(kpos_w, ok_extra=kpos_w >= 0, upper=WINDOW)
    return cmp, near, jnp.stack([new_slc, new_win]), win


def _layer_weights(p, n_heads):
    d = p["w_in"].shape[0]
    kv_end = d + 6 * N_KV * HEAD_DIM
    src = _gate_column_source(n_heads)
    gate_cols = jnp.where(jnp.asarray(src >= 0)[None, :], p["w_in"][:, kv_end + np.maximum(src, 0)], 0.0)
    ln = jnp.stack([p["ln_gain"][0], p["ln_bias"][0], p["ln_gain"][1], p["ln_bias"][1],
                    p["ln_gain"][2], p["ln_bias"][2]])
    return dict(
        wi1=p["w_ffn1_in"].astype(BF16), wo1=p["w_ffn1_out"].astype(BF16),
        wi2=p["w_ffn2_in"].astype(BF16), wo2=p["w_ffn2_out"].astype(BF16),
        wp=jnp.concatenate([p["w_in"][:, :kv_end], gate_cols], axis=1).astype(BF16),
        wout=p["w_out"].astype(BF16),
        wpool=p["w_pool"].astype(BF16), pool_scale=p["pool_scale"],
        ln1=ln[0:2], ln23=ln[2:6],
        cmp_k=_compress_weights(p["w_phi_k1"], p["w_phi_k2"], p["cmp_pos"]),
        cmp_v=_compress_weights(p["w_phi_v1"], p["w_phi_v2"], p["cmp_pos"]),
    )


PROMPT_TOKENS = 512


def _prompt_layer(x, mod, w, rel_bias, alpha):
    nb, t, d = x.shape
    assert t % TQ == 0 and t % PROMPT_TOKENS == 0
    (x1, u_pool, q, k_c, v_c, k_s, v_s, k_w, v_w, ks_bf, vs_t, kw_bf, vw_t, gates_t) = _ffn_in(
        x, mod, w["wi1"], w["wo1"], w["wp"], w["ln1"], alpha=alpha, nbk=1, tt=PROMPT_TOKENS, for_prompt=True)
    pool_out = _pool_mix(u_pool, None, w["wpool"], w["pool_scale"], pos0=0, nbk=1, tt=PROMPT_TOKENS,
                         name="pool_prompt")
    row_w = ROW_TOKENS * N_KV * HEAD_DIM
    kc_cmp = _compress(k_c.reshape(nb, t // ROW_TOKENS, row_w), *w["cmp_k"], transposed=False, name="compress_k_prompt")
    vc_cmp_t = _compress(v_c.reshape(nb, t // ROW_TOKENS, row_w), *w["cmp_v"], transposed=True, name="compress_v_prompt")
    n_c = (t - L_CMP) // STRIDE + 1
    n_blk = t // L_SEL
    near, tcmp, e0 = _prompt_tables(rel_bias, t)
    nsa = _nsa_prompt(q, gates_t, kc_cmp, vc_cmp_t, ks_bf, vs_t, kw_bf, vw_t, tcmp, near,
                      _cover_t(t // ROW_TOKENS, n_c, n_blk), e0=e0, top_n=min(TOP_N, n_blk))
    y = _ffn_out(pool_out, nsa, x1, mod, w["wout"], w["wi2"], w["wo2"], w["ln23"], alpha=alpha,
                 nbk=1, tt=PROMPT_TOKENS, name="outproj_ffn2_prompt")
    keep = min(WINDOW, t)
    heads = lambda a: a.reshape(nb, -1, N_KV, HEAD_DIM)
    states = (heads(k_c), heads(v_c), heads(k_s), heads(v_s), heads(k_w[:, t - keep:]), heads(v_w[:, t - keep:]),
              u_pool[:, t - POOL_STATE:])
    return y, states


SAMPLE_BATCHES = 32


def _sample_layer(x, mod, w, rel_bias, alpha, page_table, pools, kwin, vwin, pool_state):
    nb, t, d = x.shape
    hd = HEAD_DIM
    n_heads = rel_bias.shape[0]
    rep = n_heads // N_KV
    kvw = N_KV * hd
    pos0 = page_table.shape[1] * PAGE_SIZE
    n_c = (pos0 + t - L_CMP) // STRIDE + 1
    assert (n_c - 1) * STRIDE + L_CMP <= pos0 and pos0 % PAGE_SIZE == 0 and nb % SAMPLE_BATCHES == 0
    n_rows = pos0 // ROW_TOKENS
    n_blk = -(-(pos0 + t) // L_SEL)
    assert n_blk - 1 == pos0 // L_SEL
    blk_pad = -(-n_blk // LANE) * LANE
    wbuf = kwin.shape[1]

    x1, u_pool, q, k_c, v_c, k_s, v_s, k_w, v_w, gates = _ffn_in(
        x, mod, w["wi1"], w["wo1"], w["wp"], w["ln1"], alpha=alpha, nbk=SAMPLE_BATCHES, tt=t, for_prompt=False)
    pw = u_pool.shape[-1]
    prev = jnp.concatenate([jnp.zeros((nb, PREV_ROWS - POOL_STATE, pw), F32), pool_state], axis=1)
    pool_out = _pool_mix(u_pool, prev, w["wpool"], w["pool_scale"], pos0=pos0, nbk=SAMPLE_BATCHES, tt=t,
                         name="pool_sample")

    qh = q.reshape(nb, t, N_KV, rep, hd).transpose(0, 2, 3, 1, 4)
    own = jnp.eye(N_KV, dtype=q.dtype)[None, :, None, None, :, None]
    qbd = (qh[:, :, :, :, None, :] * own).reshape(nb, N_KV * rep * t, kvw)
    gate_rows = gates[:, :, :N_KV * GATE_ROWS].reshape(nb, t, N_KV, GATE_ROWS)[..., :3 * rep]
    gate_rows = gate_rows.reshape(nb, t, N_KV, 3, rep).transpose(0, 2, 4, 1, 3).reshape(nb, N_KV * rep * t, 3)
    gate_rows = jnp.pad(gate_rows, ((0, 0), (0, 0), (0, LANE - 3)))

    tab_c, near, newtab, wtab = _sample_tables(rel_bias, pos0, t, n_rows, n_c, wbuf)
    cov = np.zeros((n_rows, blk_pad), np.float32)
    ci = np.arange(n_rows)[:, None]
    bj = np.arange(blk_pad)[None, :]
    cov[:] = ((ci * STRIDE <= (bj + 1) * L_SEL - 1) & (ci * STRIDE + L_CMP - 1 >= bj * L_SEL)
              & (ci < n_c) & (bj < n_blk))
    row_w = ROW_TOKENS * kvw
    as_rows = lambda pool: pool.reshape(pool.shape[0], PAGE_SIZE // ROW_TOKENS, row_w)
    oc, imp = _sample_cmp(page_table, as_rows(pools[0]), as_rows(pools[1]), qbd, tab_c, w["cmp_k"], w["cmp_v"],
                          jnp.asarray(cov, BF16), rep=rep, t_new=t)

    sel_rows = -(-n_blk // 8) * 8
    imp_t = imp.transpose(2, 0, 1).reshape(blk_pad, nb * N_KV * t)[:sel_rows]
    selb_t = _select_blocks(imp_t, top_n=min(TOP_N, n_blk), n_blk=n_blk, pos0=pos0, t_new=t)
    selb_t = jnp.pad(selb_t, ((0, blk_pad - sel_rows), (0, 0)), constant_values=NEG)
    selb = selb_t.reshape(blk_pad, nb, N_KV * t).transpose(1, 2, 0)

    key_blk = np.arange(pos0)[None, :] // L_SEL
    expand = jnp.asarray((key_blk == np.arange(blk_pad)[:, None]).astype(np.float32), BF16)
    kvf = lambda a: a.reshape(a.shape[0], a.shape[1], kvw)
    o = _sample_slc(page_table, kvf(pools[2]), kvf(pools[3]), qbd, selb, gate_rows, oc, k_s, v_s, k_w, v_w,
                    kvf(kwin), kvf(vwin), near, newtab, wtab, expand, rep=rep, t_new=t)
    nsa = o[:, :, :hd].reshape(nb, N_KV, rep, t, hd).transpose(0, 3, 1, 2, 4).reshape(nb, t, n_heads * hd).astype(BF16)

    y = _ffn_out(pool_out, nsa, x1, mod, w["wout"], w["wi2"], w["wo2"], w["ln23"], alpha=alpha,
                 nbk=SAMPLE_BATCHES, tt=t, name="outproj_ffn2_sample")
    keep = min(WINDOW, pos0 + t)
    heads = lambda a: a.reshape(nb, -1, N_KV, HEAD_DIM)
    kw_ext = jnp.concatenate([kvf(kwin), k_w], axis=1)
    vw_ext = jnp.concatenate([kvf(vwin), v_w], axis=1)
    pool_ext = jnp.concatenate([pool_state, u_pool], axis=1)
    states = (heads(k_c), heads(v_c), heads(k_s), heads(v_s), heads(kw_ext[:, kw_ext.shape[1] - keep:]),
              heads(vw_ext[:, vw_ext.shape[1] - keep:]), pool_ext[:, pool_ext.shape[1] - POOL_STATE:])
    return y, states


def kernel(x_prompt, x_sample, c_prompt, c_sample, cache_k_cmp, cache_v_cmp, cache_k_slc, cache_v_slc, page_table,
           state_k_win, state_v_win, state_pool, w_ada, b_ada, ln_gain, ln_bias, w_ffn1_in, w_ffn1_out, w_ffn2_in,
           w_ffn2_out, w_in, w_phi_k1, w_phi_k2, w_phi_v1, w_phi_v2, cmp_pos, w_pool, pool_scale, w_out, rel_bias):
    depth = w_ada.shape[0]
    alpha = (2.0 * depth) ** 0.25
    n_heads = rel_bias.shape[0]
    nb_p, nb_s = x_prompt.shape[0], x_sample.shape[0]
    d = x_prompt.shape[-1]
    xp, xs = x_prompt, x_sample
    c_all = jnp.concatenate([c_prompt, c_sample], axis=0)
    p_states, s_states = [], []
    for l in range(depth):
        p = dict(w_ada=w_ada[l], b_ada=b_ada[l], ln_gain=ln_gain[l], ln_bias=ln_bias[l], w_ffn1_in=w_ffn1_in[l],
                 w_ffn1_out=w_ffn1_out[l], w_ffn2_in=w_ffn2_in[l], w_ffn2_out=w_ffn2_out[l], w_in=w_in[l],
                 w_phi_k1=w_phi_k1[l], w_phi_k2=w_phi_k2[l], w_phi_v1=w_phi_v1[l], w_phi_v2=w_phi_v2[l],
                 cmp_pos=cmp_pos[l], w_pool=w_pool[l], pool_scale=pool_scale[l], w_out=w_out[l])
        w = _layer_weights(p, n_heads)
        mod = _ada(c_all, p["w_ada"], p["b_ada"]).reshape(nb_p + nb_s, 9, d)
        xp, sp = _prompt_layer(xp, mod[:nb_p], w, rel_bias, alpha)
        xs, ss = _sample_layer(xs, mod[nb_p:], w, rel_bias, alpha, page_table,
                               (cache_k_cmp[l], cache_v_cmp[l], cache_k_slc[l], cache_v_slc[l]),
                               state_k_win[l], state_v_win[l], state_pool[l])
        p_states.append(sp)
        s_states.append(ss)
    stack = lambda states: tuple(jnp.stack(a) for a in zip(*states))
    return (xp, xs) + stack(p_states) + stack(s_states)
```

```python
import functools
import math

import numpy as np
import jax
import jax.numpy as jnp
from jax import lax
from jax.experimental import pallas as pl
from jax.experimental.pallas import tpu as pltpu

F32 = jnp.float32
BF16 = jnp.bfloat16

HEAD_DIM = 64
N_KV = 2
L_CMP = 32
STRIDE = 16
L_SEL = 64
TOP_N = 16
WINDOW = 512
N_BUCKETS = 32
MAX_DIST = 128
POOL_WINDOWS = (2, 4, 8, 16)
POOL_STATE = max(POOL_WINDOWS) - 1
LN_EPS = 1e-5
NEG = -1e30
PAGE_SIZE = 128

LANE = 128
FF_CHUNK = 256
VMEM_LIMIT = 56 * 1024 * 1024


def _cparams(n_axes):
    return pltpu.CompilerParams(dimension_semantics=("arbitrary",) * n_axes,
                                vmem_limit_bytes=VMEM_LIMIT)


def _const_spec(shape):
    nd = len(shape)
    return pl.BlockSpec(shape, lambda *_: (0,) * nd, pipeline_mode=pl.Buffered(1))


def _layer_norm(y, gain, bias):
    mu = jnp.mean(y, axis=-1, keepdims=True)
    yc = y - mu
    var = jnp.mean(yc * yc, axis=-1, keepdims=True)
    return yc * lax.rsqrt(var + LN_EPS) * gain + bias


def _swiglu(u_bf, wi_ref, wo_ref):
    d_ff = wo_ref.shape[0]
    acc = None
    for c in range(d_ff // FF_CHUNK):
        lo = c * FF_CHUNK
        a = jnp.dot(u_bf, wi_ref[:, lo:lo + FF_CHUNK], preferred_element_type=F32)
        b = jnp.dot(u_bf, wi_ref[:, d_ff + lo:d_ff + lo + FF_CHUNK], preferred_element_type=F32)
        hid = (a * jax.nn.sigmoid(a) * b).astype(BF16)
        part = jnp.dot(hid, wo_ref[lo:lo + FF_CHUNK, :], preferred_element_type=F32)
        acc = part if acc is None else acc + part
    return acc


def _ada_body(c_ref, w_ref, b_ref, o_ref):
    c = c_ref[...]
    act = (c * jax.nn.sigmoid(c)).astype(BF16)
    o_ref[...] = jnp.dot(act, w_ref[...].astype(BF16), preferred_element_type=F32) + b_ref[...]


def _ada(c_all, w_ada, b_ada):
    nb, d = c_all.shape
    n_out = w_ada.shape[1]
    cols = 9 * LANE
    return pl.pallas_call(
        _ada_body,
        grid=(n_out // cols,),
        in_specs=[pl.BlockSpec((nb, d), lambda j: (0, 0)),
                  pl.BlockSpec((d, cols), lambda j: (0, j)),
                  pl.BlockSpec((1, cols), lambda j: (0, j))],
        out_specs=pl.BlockSpec((nb, cols), lambda j: (0, j)),
        out_shape=jax.ShapeDtypeStruct((nb, n_out), F32),
        compiler_params=_cparams(1),
        name="ada",
    )(c_all, w_ada, b_ada.reshape(1, n_out))


KV_NAMES = ("k_c", "v_c", "k_s", "v_s", "k_w", "v_w")


def _ffn_in_body(x_ref, mod_ref, wi_ref, wo_ref, wp_ref, ln_ref, *out_refs, alpha, pool_w, nsa_w, kv_w, for_prompt):
    nbk, tt, d = x_ref.shape
    n = nbk * tt
    x = x_ref[...]
    mod = mod_ref[...]
    u = (x * (1.0 + mod[:, 1:2, :]) + mod[:, 0:1, :]).reshape(n, d).astype(BF16)
    h = _swiglu(u, wi_ref, wo_ref).reshape(nbk, tt, d)
    x1 = _layer_norm(alpha * x + 0.5 * mod[:, 2:3, :] * h, ln_ref[0:1, :], ln_ref[1:2, :])
    u1 = (x1 * (1.0 + mod[:, 4:5, :]) + mod[:, 3:4, :]).reshape(n, d).astype(BF16)
    proj = jnp.dot(u1, wp_ref[...], preferred_element_type=F32)

    x1_ref, upool_ref, q_ref = out_refs[:3]
    kv_refs = out_refs[3:9]
    x1_ref[...] = x1
    upool_ref[...] = proj[:, :pool_w].reshape(nbk, tt, pool_w)
    q_ref[...] = (proj[:, pool_w:pool_w + nsa_w] * (HEAD_DIM ** -0.5)).astype(BF16).reshape(nbk, tt, nsa_w)
    off = pool_w + nsa_w
    kv = []
    for i in range(6):
        blk = proj[:, off + i * kv_w:off + (i + 1) * kv_w]
        kv.append(blk)
        kv_refs[i][...] = blk.reshape(nbk, tt, kv_w)
    gates = jax.nn.sigmoid(proj[:, off + 6 * kv_w:off + 6 * kv_w + LANE])
    if for_prompt:
        ksb_ref, vst_ref, kwb_ref, vwt_ref, gt_ref = out_refs[9:]
        ksb_ref[0] = kv[2].astype(BF16)
        vst_ref[0] = kv[3].T.astype(BF16)
        kwb_ref[0] = kv[4].astype(BF16)
        vwt_ref[0] = kv[5].T.astype(BF16)
        gt_ref[0] = gates.T[:gt_ref.shape[1], :]
    else:
        out_refs[9][...] = gates.reshape(nbk, tt, LANE)


def _ffn_in(x, mod, wi, wo, wp, ln, *, alpha, nbk, tt, for_prompt):
    nb, t, d = x.shape
    pool_w = d // 2
    nsa_w = d - pool_w
    kv_w = N_KV * HEAD_DIM
    grid = (nb // nbk, t // tt)
    tok = lambda w: pl.BlockSpec((nbk, tt, w), lambda i, j: (i, j, 0))
    out_specs = [tok(d), tok(pool_w), tok(nsa_w)] + [tok(kv_w)] * 6
    out_shape = ([jax.ShapeDtypeStruct((nb, t, d), F32), jax.ShapeDtypeStruct((nb, t, pool_w), F32),
                  jax.ShapeDtypeStruct((nb, t, nsa_w), BF16)]
                 + [jax.ShapeDtypeStruct((nb, t, kv_w), F32)] * 6)
    if for_prompt:
        assert nbk == 1
        tr = lambda rows: pl.BlockSpec((1, rows, tt), lambda i, j: (i, 0, j))
        out_specs += [tok(kv_w), tr(kv_w), tok(kv_w), tr(kv_w), tr(2 * GATE_ROWS)]
        out_shape += [jax.ShapeDtypeStruct((nb, t, kv_w), BF16), jax.ShapeDtypeStruct((nb, kv_w, t), BF16),
                      jax.ShapeDtypeStruct((nb, t, kv_w), BF16), jax.ShapeDtypeStruct((nb, kv_w, t), BF16),
                      jax.ShapeDtypeStruct((nb, 2 * GATE_ROWS, t), F32)]
    else:
        out_specs += [tok(LANE)]
        out_shape += [jax.ShapeDtypeStruct((nb, t, LANE), F32)]
    body = functools.partial(_ffn_in_body, alpha=alpha, pool_w=pool_w, nsa_w=nsa_w, kv_w=kv_w,
                             for_prompt=for_prompt)
    return pl.pallas_call(
        body,
        grid=grid,
        in_specs=[tok(d),
                  pl.BlockSpec((nbk, mod.shape[1], d), lambda i, j: (i, 0, 0)),
                  _const_spec(wi.shape), _const_spec(wo.shape), _const_spec(wp.shape), _const_spec(ln.shape)],
        out_specs=out_specs,
        out_shape=out_shape,
        compiler_params=_cparams(2),
        name="ffn1_inproj_prompt" if for_prompt else "ffn1_inproj_sample",
    )(x, mod, wi, wo, wp, ln)


GATE_ROWS = 16


def _gate_column_source(n_heads):
    rep = n_heads // N_KV
    src = np.full((LANE,), -1, np.int32)
    for g in range(N_KV):
        for br in range(3):
            for r in range(rep):
                src[g * GATE_ROWS + br * rep + r] = br * n_heads + g * rep + r
    return src


def _ffn_out_body(pool_ref, nsa_ref, x1_ref, mod_ref, wout_ref, wi_ref, wo_ref, ln_ref, o_ref, *, alpha):
    nbk, tt, d = x1_ref.shape
    n = nbk * tt
    pw = pool_ref.shape[-1]
    mod = mod_ref[...]
    x1 = x1_ref[...]
    mix = (jnp.dot(pool_ref[...].reshape(n, pw), wout_ref[:pw, :], preferred_element_type=F32)
           + jnp.dot(nsa_ref[...].reshape(n, d - pw), wout_ref[pw:, :], preferred_element_type=F32))
    x2 = _layer_norm(alpha * x1 + mod[:, 5:6, :] * mix.reshape(nbk, tt, d), ln_ref[0:1, :], ln_ref[1:2, :])
    u = (x2 * (1.0 + mod[:, 7:8, :]) + mod[:, 6:7, :]).reshape(n, d).astype(BF16)
    h = _swiglu(u, wi_ref, wo_ref).reshape(nbk, tt, d)
    o_ref[...] = _layer_norm(alpha * x2 + 0.5 * mod[:, 8:9, :] * h, ln_ref[2:3, :], ln_ref[3:4, :])


def _ffn_out(pool_out, nsa_out, x1, mod, wout, wi, wo, ln, *, alpha, nbk, tt, name):
    nb, t, d = x1.shape
    tok = lambda w: pl.BlockSpec((nbk, tt, w), lambda i, j: (i, j, 0))
    return pl.pallas_call(
        functools.partial(_ffn_out_body, alpha=alpha),
        grid=(nb // nbk, t // tt),
        in_specs=[tok(pool_out.shape[-1]), tok(nsa_out.shape[-1]), tok(d),
                  pl.BlockSpec((nbk, mod.shape[1], d), lambda i, j: (i, 0, 0)),
                  _const_spec(wout.shape), _const_spec(wi.shape), _const_spec(wo.shape), _const_spec(ln.shape)],
        out_specs=tok(d),
        out_shape=jax.ShapeDtypeStruct((nb, t, d), F32),
        compiler_params=_cparams(2),
        name=name,
    )(pool_out, nsa_out, x1, mod, wout, wi, wo, ln)


PREV_ROWS = 16


def _pool_body(u_ref, prev_ref, w_ref, scale_ref, o_ref, ext_ref, *, pos0, zero_first):
    nbk, tt, width = u_ref.shape
    gw = width // len(POOL_WINDOWS)
    j = pl.program_id(1)
    u = u_ref[...]
    prev = prev_ref[...]
    if zero_first:
        prev = jnp.where(j == 0, 0.0, prev)
    ext_ref[:, 0:PREV_ROWS, :] = prev
    ext_ref[:, PREV_ROWS:, :] = u
    pos = pos0 + j * tt + lax.broadcasted_iota(jnp.int32, (1, tt, 1), 1)
    for g, w in enumerate(POOL_WINDOWS):
        lo = g * gw
        tot = None
        for k in range(w):
            part = ext_ref[:, PREV_ROWS - k:PREV_ROWS - k + tt, lo:lo + gw]
            tot = part if tot is None else tot + part
        cnt = jnp.minimum(pos + 1, w).astype(F32)
        pooled = (tot / cnt - u[:, :, lo:lo + gw]).astype(BF16).reshape(nbk * tt, gw)
        mixed = jnp.dot(pooled, w_ref[g], preferred_element_type=F32) * scale_ref[:, lo:lo + gw]
        o_ref[:, :, lo:lo + gw] = mixed.astype(BF16).reshape(nbk, tt, gw)


def _pool_mix(u_pool, prev, w_pool_bf, pool_scale, *, pos0, nbk, tt, name):
    nb, t, width = u_pool.shape
    tok = pl.BlockSpec((nbk, tt, width), lambda i, j: (i, j, 0))
    if prev is None:
        assert nbk == 1 and tt % PREV_ROWS == 0 and pos0 == 0
        step = tt // PREV_ROWS
        prev_arr = u_pool
        prev_spec = pl.BlockSpec((1, PREV_ROWS, width), lambda i, j: (i, jnp.maximum(j * step - 1, 0), 0))
    else:
        assert t == tt
        prev_arr = prev
        prev_spec = pl.BlockSpec((nbk, PREV_ROWS, width), lambda i, j: (i, 0, 0))
    return pl.pallas_call(
        functools.partial(_pool_body, pos0=pos0, zero_first=prev is None),
        grid=(nb // nbk, t // tt),
        in_specs=[tok, prev_spec, _const_spec(w_pool_bf.shape), _const_spec((1, width))],
        out_specs=tok,
        out_shape=jax.ShapeDtypeStruct((nb, t, width), BF16),
        scratch_shapes=[pltpu.VMEM((nbk, PREV_ROWS + tt, width), F32)],
        compiler_params=_cparams(2),
        name=name,
    )(u_pool, prev_arr, w_pool_bf, pool_scale.reshape(1, width))


ROW_TOKENS = STRIDE


def _compress_body(r_ref, wexp_ref, w2_ref, pos_ref, o_ref, sh_ref, *, transposed):
    nr = r_ref.shape[1]
    half = wexp_ref.shape[1] // 2
    p = jnp.dot(r_ref[0].astype(BF16), wexp_ref[...], preferred_element_type=F32)
    pp = jnp.dot(pos_ref[...], wexp_ref[...], preferred_element_type=F32)
    posb = pp[0:1, :half] + pp[1:2, half:]
    sh_ref[0:nr, :] = p[:, half:]
    sh_ref[nr:nr + 8, :] = jnp.zeros((8, half), F32)
    h = p[:, :half] + sh_ref[1:nr + 1, :] + posb
    c = jnp.dot(jax.nn.gelu(h).astype(BF16), w2_ref[...], preferred_element_type=F32)
    o_ref[0] = (c.T if transposed else c).astype(BF16)


def _compress(rows, wexp, w2bd, posrows, *, transposed, name):
    nb, nr, k = rows.shape
    half = wexp.shape[1] // 2
    oshape = (nb, half, nr) if transposed else (nb, nr, half)
    return pl.pallas_call(
        functools.partial(_compress_body, transposed=transposed),
        grid=(nb,),
        in_specs=[pl.BlockSpec((1, nr, k), lambda b: (b, 0, 0)),
                  _const_spec(wexp.shape), _const_spec(w2bd.shape), _const_spec(posrows.shape)],
        out_specs=pl.BlockSpec((1,) + oshape[1:], lambda b: (b, 0, 0)),
        out_shape=jax.ShapeDtypeStruct(oshape, BF16),
        scratch_shapes=[pltpu.VMEM((nr + 8, half), F32)],
        compiler_params=_cparams(1),
        name=name,
    )(rows, wexp, w2bd, posrows)


def _compress_weights(w1, w2, cmp_pos):
    eye = jnp.eye(N_KV, dtype=F32)
    w_blk = w1.reshape(L_CMP, HEAD_DIM, HEAD_DIM)
    halves = []
    for part in range(L_CMP // ROW_TOKENS):
        wpart = w_blk[part * ROW_TOKENS:(part + 1) * ROW_TOKENS]
        halves.append(jnp.einsum("gh,lio->lgiho", eye, wpart).reshape(ROW_TOKENS * N_KV * HEAD_DIM, N_KV * HEAD_DIM))
    wexp = jnp.concatenate(halves, axis=1).astype(BF16)
    w2bd = jnp.einsum("gh,io->giho", eye, w2).reshape(N_KV * HEAD_DIM, N_KV * HEAD_DIM).astype(BF16)
    pos = jnp.broadcast_to(cmp_pos.reshape(L_CMP // ROW_TOKENS, ROW_TOKENS, 1, HEAD_DIM),
                           (L_CMP // ROW_TOKENS, ROW_TOKENS, N_KV, HEAD_DIM)).reshape(L_CMP // ROW_TOKENS, -1)
    posrows = jnp.zeros((8, pos.shape[1]), F32).at[:pos.shape[0]].set(pos).astype(BF16)
    return wexp, w2bd, posrows


def _bucket_starts():
    d = np.arange(0, 4 * MAX_DIST)
    exact = N_BUCKETS // 2
    nf = np.maximum(d, 1).astype(np.float32)
    large = exact + (np.log(nf / np.float32(exact)) / np.float32(math.log(MAX_DIST / exact))
                     * np.float32(N_BUCKETS - exact)).astype(np.int32)
    bucket = np.where(d < exact, d, np.minimum(large, N_BUCKETS - 1))
    assert bucket.max() == N_BUCKETS - 1 and np.all(np.diff(bucket) >= 0)
    return [int(np.argmax(bucket >= b)) for b in range(1, N_BUCKETS)]


def _bias_minus_far(rel_bias, dist, head_axis):
    tab = rel_bias - rel_bias[:, N_BUCKETS - 1:]
    hshape = (1,) * head_axis + (-1,) + (1,) * (dist.ndim - head_axis)
    dist = jnp.expand_dims(dist, head_axis)
    out = jnp.broadcast_to(tab[:, 0].reshape(hshape),
                           dist.shape[:head_axis] + (tab.shape[0],) + dist.shape[head_axis + 1:])
    for b, start in enumerate(_bucket_starts(), start=1):
        out = jnp.where(dist >= start, tab[:, b].reshape(hshape), out)
    return out


TQ = 128
FAR_CHUNKS = 4


def _prompt_tables(rel_bias, t):
    h = rel_bias.shape[0]
    qo = jnp.arange(TQ, dtype=jnp.int32)
    n_near = WINDOW // L_SEL + 2 * (TQ // L_SEL)
    dd = (WINDOW // L_SEL) - jnp.arange(n_near, dtype=jnp.int32)
    ko = jnp.arange(L_SEL, dtype=jnp.int32)
    d = (L_SEL * dd[:, None, None] + qo[None, None, :] - ko[None, :, None]).reshape(n_near * L_SEL, TQ)
    ok = (d >= 0) & (d < WINDOW)
    near = jnp.where(ok[:, None, :], _bias_minus_far(rel_bias, d, 1), NEG)
    near = near.reshape(n_near * L_SEL, h * TQ)
    n_tiles = t // TQ
    per_tile = TQ // STRIDE
    e0 = per_tile * (n_tiles - 1)
    rows = e0 + t // STRIDE
    e = e0 - jnp.arange(rows, dtype=jnp.int32)
    dc = STRIDE * e[:, None] - (L_CMP - 1) + qo[None, :]
    cmp = jnp.where((dc >= 0)[:, None, :], _bias_minus_far(rel_bias, dc, 1), NEG)
    cmp = cmp.reshape(rows, h * TQ)
    return near.astype(F32), cmp.astype(F32), e0


def _cover_t(n_rows, n_c, n_blk):
    i = np.arange(n_rows)[None, :]
    j = np.arange(n_blk)[:, None]
    start = i * STRIDE
    end = start + L_CMP - 1
    cov = (start <= (j + 1) * L_SEL - 1) & (end >= j * L_SEL) & (i < n_c)
    return jnp.asarray(cov.astype(np.float32), BF16)


def _descending_rank(v, n_valid):
    rows, cols = v.shape
    sub = lax.broadcasted_iota(jnp.int32, (8, cols), 0)
    rank = jnp.zeros((rows, cols), F32)
    for jp in range(n_valid):
        other = v[jp:jp + 1, :]
        lo = (jp // 8) * 8
        mid = v[lo:lo + 8]
        parts = [jnp.where(other > mid, 1.0, jnp.where((other == mid) & (sub > jp - lo), 1.0, 0.0))]
        if lo > 0:
            parts.insert(0, jnp.where(other > v[:lo], 1.0, 0.0))
        if lo + 8 < rows:
            parts.append(jnp.where(other >= v[lo + 8:], 1.0, 0.0))
        rank = rank + jnp.concatenate(parts, axis=0)
    return rank


def _nsa_prompt_body(q_ref, gt_ref, kc_ref, vct_ref, ks_ref, vst_ref, kw_ref, vwt_ref, tcmp_ref, near_ref, cov_ref,
                     o_ref, selb_ref, *, e0, top_n):
    g = pl.program_id(1)
    it = pl.program_id(2)
    t = ks_ref.shape[1]
    n_tiles = t // TQ
    hd = HEAD_DIM
    rep = q_ref.shape[2] // hd
    cols = rep * TQ
    n_blk = cov_ref.shape[0]
    g_off = pl.multiple_of(g * hd, hd)

    qt = q_ref[0].astype(F32).T
    row = lax.broadcasted_iota(jnp.int32, (N_KV * hd, TQ), 0)
    mine = (row >= hd) == (g == 1)
    tiles = []
    for r in range(rep):
        blk = qt[r * hd:(r + 1) * hd, :]
        tiles.append(jnp.where(mine, jnp.concatenate([blk] * N_KV, axis=0), 0.0))
    qbd = jnp.concatenate(tiles, axis=1).astype(BF16)

    gate = [jnp.concatenate([gt_ref[0, br * rep + r:br * rep + r + 1, :] for r in range(rep)], axis=1)
            for br in range(3)]

    n_ck = kc_ref.shape[1]
    sc = jnp.dot(kc_ref[0], qbd, preferred_element_type=F32)
    t0 = pl.multiple_of(e0 - (TQ // STRIDE) * it, 8)
    sc = sc + tcmp_ref[pl.ds(t0, n_ck), :]
    mc = jnp.max(sc, axis=0, keepdims=True)
    pc = jnp.where(sc > 0.1 * NEG, jnp.exp(sc - mc), 0.0)
    lc = jnp.sum(pc, axis=0, keepdims=True)
    pn = (pc * jnp.where(lc > 0.0, 1.0 / lc, 0.0)).astype(BF16)
    oc = jnp.dot(vct_ref[0], pn, preferred_element_type=F32)
    oc = jnp.where(g == 0, oc[:hd], oc[hd:])

    imp = jnp.dot(cov_ref[...], pn, preferred_element_type=F32)
    v = imp[:, 0:TQ]
    for r in range(1, rep):
        v = v + imp[:, r * TQ:(r + 1) * TQ]
    j = lax.broadcasted_iota(jnp.int32, (n_blk, TQ), 0)
    lane = lax.broadcasted_iota(jnp.int32, (n_blk, TQ), 1)
    cur = (TQ // L_SEL) * it + lane // L_SEL
    forced = (j == 0) | (j == cur) | (j == cur - 1)
    v = jnp.where(forced, 1e9, jnp.where(j <= cur, v, -1e9))
    rank = _descending_rank(v, n_blk)
    selb = jnp.where((rank < top_n) & (j <= cur), 0.0, NEG)

    per = TQ // L_SEL
    diag0 = jnp.maximum(it - 1, 0)
    selb_ref[...] = jnp.concatenate([jnp.where(j < per * diag0, selb, NEG)] * rep, axis=1)
    gk = FAR_CHUNKS * TQ
    bpg = gk // L_SEL

    def online_step(carry, s, v_t):
        m, l, acc = carry
        m_new = jnp.maximum(m, jnp.max(s, axis=0, keepdims=True))
        alpha = jnp.exp(m - m_new)
        p = jnp.exp(s - m_new)
        l = alpha * l + jnp.sum(p, axis=0, keepdims=True)
        acc = alpha * acc + jnp.dot(v_t, p.astype(BF16), preferred_element_type=F32)
        return m_new, l, acc

    def far_group(cb, carry):
        off = pl.multiple_of(cb * gk, gk)
        s = jnp.dot(ks_ref[0, pl.ds(off, gk), :], qbd, preferred_element_type=F32)
        sel = selb_ref[pl.ds(pl.multiple_of(cb * bpg, bpg), bpg), :]
        s = s + jnp.concatenate([jnp.broadcast_to(sel[k:k + 1, :], (L_SEL, cols)) for k in range(bpg)], axis=0)
        return online_step(carry, s, vst_ref[0, pl.ds(g_off, hd), pl.ds(off, gk)])

    carry = (jnp.full((1, cols), NEG, F32), jnp.zeros((1, cols), F32), jnp.zeros((hd, cols), F32))
    carry = lax.fori_loop(0, (diag0 + FAR_CHUNKS - 1) // FAR_CHUNKS, far_group, carry)

    dk = 2 * TQ
    offd = pl.multiple_of(diag0 * TQ, TQ)
    near_off = pl.multiple_of((WINDOW // L_SEL - per * (it - diag0)) * L_SEL, TQ)
    sd = jnp.dot(ks_ref[0, pl.ds(offd, dk), :], qbd, preferred_element_type=F32) + near_ref[pl.ds(near_off, dk), :]
    rows = []
    for k in range(dk // L_SEL):
        rk = jnp.max(jnp.where(j == per * diag0 + k, selb, NEG), axis=0, keepdims=True)
        rows.append(jnp.broadcast_to(jnp.concatenate([rk] * rep, axis=1), (L_SEL, cols)))
    sd = sd + jnp.concatenate(rows, axis=0)
    _, ls, acc_s = online_step(carry, sd, vst_ref[0, pl.ds(g_off, hd), pl.ds(offd, dk)])

    n_wc = WINDOW // TQ + 1
    sw = []
    offs = []
    for k in range(n_wc):
        kidx = it - (n_wc - 1) + k
        offk = pl.multiple_of(jnp.maximum(kidx, 0) * TQ, TQ)
        offs.append(offk)
        s = jnp.dot(kw_ref[0, pl.ds(offk, TQ), :], qbd, preferred_element_type=F32)
        sw.append(s + near_ref[k * TQ:(k + 1) * TQ, :] + jnp.where(kidx >= 0, 0.0, NEG))
    sw = jnp.concatenate(sw, axis=0)
    mw = jnp.max(sw, axis=0, keepdims=True)
    pw = jnp.exp(sw - mw)
    lw = jnp.sum(pw, axis=0, keepdims=True)
    pw = pw.astype(BF16)
    ow = None
    for k in range(n_wc):
        part = jnp.dot(vwt_ref[0, pl.ds(g_off, hd), pl.ds(offs[k], TQ)], pw[k * TQ:(k + 1) * TQ, :],
                       preferred_element_type=F32)
        ow = part if ow is None else ow + part

    out_t = gate[0] * oc + (gate[1] / ls) * acc_s + (gate[2] / lw) * ow
    pieces = []
    for r in range(0, rep, 2):
        pair = jnp.concatenate([out_t[:, r * TQ:(r + 1) * TQ], out_t[:, (r + 1) * TQ:(r + 2) * TQ]], axis=0)
        pieces.append(pair.T)
    o_ref[0] = jnp.concatenate(pieces, axis=1).astype(BF16)


def _nsa_prompt(q, gt, kc, vct, ks, vst, kw, vwt, tcmp, near, cov_t, *, e0, top_n):
    nb, t, nsa_w = q.shape
    gw = nsa_w // N_KV
    kv_w = ks.shape[2]
    n_tiles = t // TQ
    cols = (gw // HEAD_DIM) * TQ
    per_b = lambda shape: pl.BlockSpec((1,) + shape, lambda b, g, i: (b, 0, 0))
    return pl.pallas_call(
        functools.partial(_nsa_prompt_body, e0=e0, top_n=top_n),
        grid=(nb, N_KV, n_tiles),
        in_specs=[pl.BlockSpec((1, TQ, gw), lambda b, g, i: (b, i, g)),
                  pl.BlockSpec((1, GATE_ROWS, TQ), lambda b, g, i: (b, g, i)),
                  per_b(kc.shape[1:]), per_b(vct.shape[1:]),
                  per_b((t, kv_w)), per_b((kv_w, t)), per_b((t, kv_w)), per_b((kv_w, t)),
                  pl.BlockSpec((tcmp.shape[0], cols), lambda b, g, i: (0, g)),
                  pl.BlockSpec((near.shape[0], cols), lambda b, g, i: (0, g)),
                  pl.BlockSpec(cov_t.shape, lambda b, g, i: (0, 0))],
        out_specs=pl.BlockSpec((1, TQ, gw), lambda b, g, i: (b, i, g)),
        out_shape=jax.ShapeDtypeStruct((nb, t, nsa_w), BF16),
        scratch_shapes=[pltpu.VMEM((cov_t.shape[0], cols), F32)],
        compiler_params=_cparams(3),
        name="nsa_prompt",
    )(q, gt, kc, vct, ks, vst, kw, vwt, tcmp, near, cov_t)


def _dot_nt(a, b):
    return lax.dot_general(a, b, (((1,), (1,)), ((), ())), preferred_element_type=F32)


def _page_copies(pt_ref, pools, bufs, sems, batch, slot, dst):
    n_pages = pt_ref.shape[1]
    out = []
    for i, (pool, buf) in enumerate(zip(pools, bufs)):
        for p in range(n_pages):
            out.append(pltpu.make_async_copy(pool.at[pt_ref[batch, p]], dst(buf, slot, p), sems.at[i, slot]))
    return out


def _gather_step(pt_ref, pools, bufs, sems, dst):
    b = pl.program_id(0)
    slot = b % 2

    @pl.when(b == 0)
    def _():
        for cp in _page_copies(pt_ref, pools, bufs, sems, 0, 0, dst):
            cp.start()

    @pl.when(b + 1 < pl.num_programs(0))
    def _():
        for cp in _page_copies(pt_ref, pools, bufs, sems, b + 1, 1 - slot, dst):
            cp.start()

    for cp in _page_copies(pt_ref, pools, bufs, sems, b, slot, dst):
        cp.wait()
    return slot


def _compress_pages(buf, slot, perm_ref, x_ref, wexp_ref, w2_ref, pos_ref, sh_ref):
    n_pages = buf.shape[1]
    kvw = buf.shape[2]
    rpp = PAGE_SIZE // ROW_TOKENS
    nr = n_pages * rpp
    half = wexp_ref.shape[1] // 2

    for i in range(n_pages // 2):
        pair = buf[slot, 2 * i:2 * i + 2].reshape(2 * kvw, PAGE_SIZE).astype(BF16)
        xt = _dot_nt(perm_ref[...], pair)
        for pp in range(2):
            for l in range(ROW_TOKENS):
                row0 = (l // 2) * nr + (2 * i + pp) * rpp
                x_ref[row0:row0 + rpp, (l % 2) * kvw:(l % 2 + 1) * kvw] = (
                    xt[l * rpp:(l + 1) * rpp, pp * kvw:(pp + 1) * kvw])
    p = None
    for i in range(ROW_TOKENS // 2):
        part = jnp.dot(x_ref[i * nr:(i + 1) * nr, :].astype(BF16), wexp_ref[2 * i * kvw:2 * (i + 1) * kvw, :],
                       preferred_element_type=F32)
        p = part if p is None else p + part
    pp = jnp.dot(pos_ref[...], wexp_ref[...], preferred_element_type=F32)
    posb = pp[0:1, :half] + pp[1:2, half:]
    sh_ref[0:nr, :] = p[:, half:]
    sh_ref[nr:nr + 8, :] = jnp.zeros((8, half), F32)
    h = p[:, :half] + sh_ref[1:nr + 1, :] + posb
    return jnp.dot(jax.nn.gelu(h).astype(BF16), w2_ref[...], preferred_element_type=F32).astype(BF16)


def _sample_cmp_body(pt_ref, kpool_ref, vpool_ref, qbd_ref, tab_ref, perm_ref, wk_ref, w2k_ref, pk_ref, wv_ref, w2v_ref,
                     pv_ref, cov_ref, oc_ref, imp_ref, kbuf, vbuf, sems, x_ref, sh_ref, *, rep, t_new):
    slot = _gather_step(pt_ref, (kpool_ref, vpool_ref), (kbuf, vbuf), sems, lambda buf, sl, p: buf.at[sl, p])
    kc = _compress_pages(kbuf, slot, perm_ref, x_ref, wk_ref, w2k_ref, pk_ref, sh_ref)
    vc = _compress_pages(vbuf, slot, perm_ref, x_ref, wv_ref, w2v_ref, pv_ref, sh_ref)
    s = _dot_nt(qbd_ref[0], kc) + tab_ref[...]
    m = jnp.max(s, axis=1, keepdims=True)
    p = jnp.where(s > 0.1 * NEG, jnp.exp(s - m), 0.0)
    l = jnp.sum(p, axis=1, keepdims=True)
    pn = (p * jnp.where(l > 0.0, 1.0 / l, 0.0)).astype(BF16)
    oc_ref[0] = jnp.dot(pn, vc, preferred_element_type=F32)
    imp = jnp.dot(pn, cov_ref[...], preferred_element_type=F32)
    per_g = rep * t_new
    parts = []
    for g in range(N_KV):
        acc = imp[g * per_g:g * per_g + t_new]
        for r in range(1, rep):
            acc = acc + imp[g * per_g + r * t_new:g * per_g + (r + 1) * t_new]
        parts.append(acc)
    imp_ref[0] = jnp.concatenate(parts, axis=0)


def _sample_cmp(page_table, kpool_t, vpool_t, qbd, tab, cmp_k, cmp_v, cov, *, rep, t_new):
    nb, n_pages = page_table.shape
    rpp = PAGE_SIZE // ROW_TOKENS
    n_rows = n_pages * rpp
    n_q = qbd.shape[1]
    kvw = N_KV * HEAD_DIM
    tok = np.arange(PAGE_SIZE)
    perm = jnp.asarray((tok[None, :] == (tok[:, None] % rpp) * ROW_TOKENS + tok[:, None] // rpp).astype(np.float32), BF16)
    const = lambda a: pl.BlockSpec(a.shape, lambda b, pt: (0,) * a.ndim, pipeline_mode=pl.Buffered(1))
    grid_spec = pltpu.PrefetchScalarGridSpec(
        num_scalar_prefetch=1,
        grid=(nb,),
        in_specs=[pl.BlockSpec(memory_space=pl.ANY), pl.BlockSpec(memory_space=pl.ANY),
                  pl.BlockSpec((1, n_q, kvw), lambda b, pt: (b, 0, 0)), const(tab), const(perm),
                  const(cmp_k[0]), const(cmp_k[1]), const(cmp_k[2]), const(cmp_v[0]), const(cmp_v[1]), const(cmp_v[2]),
                  const(cov)],
        out_specs=[pl.BlockSpec((1, n_q, kvw), lambda b, pt: (b, 0, 0)),
                   pl.BlockSpec((1, N_KV * t_new, cov.shape[1]), lambda b, pt: (b, 0, 0))],
        scratch_shapes=[pltpu.VMEM((2, n_pages, kvw, PAGE_SIZE), F32), pltpu.VMEM((2, n_pages, kvw, PAGE_SIZE), F32),
                        pltpu.SemaphoreType.DMA((2, 2)), pltpu.VMEM((n_pages * PAGE_SIZE // 2, 2 * kvw), F32),
                        pltpu.VMEM((n_rows + 8, kvw), F32)],
    )
    return pl.pallas_call(
        functools.partial(_sample_cmp_body, rep=rep, t_new=t_new),
        grid_spec=grid_spec,
        out_shape=[jax.ShapeDtypeStruct((nb, n_q, kvw), F32),
                   jax.ShapeDtypeStruct((nb, N_KV * t_new, cov.shape[1]), F32)],
        compiler_params=_cparams(1),
        name="sample_cmp",
    )(page_table, kpool_t, vpool_t, qbd, tab, perm, *cmp_k, *cmp_v, cov)


def _select_body(imp_ref, o_ref, *, top_n, n_blk, pos0, t_new):
    rows, cols = imp_ref.shape
    j = lax.broadcasted_iota(jnp.int32, (rows, cols), 0)
    col = pl.program_id(0) * cols + lax.broadcasted_iota(jnp.int32, (rows, cols), 1)
    cur = (pos0 + col % t_new) // L_SEL
    forced = (j == 0) | (j == cur) | (j == cur - 1)
    v = jnp.where(forced, 1e9, jnp.where(j <= cur, imp_ref[...], -1e9))
    v = jnp.where(j < n_blk, v, -3e38)
    rank = _descending_rank(v, n_blk)
    o_ref[...] = jnp.where((rank < top_n) & (j <= cur) & (j < n_blk), 0.0, NEG)


def _select_blocks(imp_t, *, top_n, n_blk, pos0, t_new):
    rows, cols = imp_t.shape
    tile = 4 * LANE
    return pl.pallas_call(
        functools.partial(_select_body, top_n=top_n, n_blk=n_blk, pos0=pos0, t_new=t_new),
        grid=(cols // tile,),
        in_specs=[pl.BlockSpec((rows, tile), lambda i: (0, i))],
        out_specs=pl.BlockSpec((rows, tile), lambda i: (0, i)),
        out_shape=jax.ShapeDtypeStruct((rows, cols), F32),
        compiler_params=_cparams(1),
        name="sample_select",
    )(imp_t)


def _softmax_pv(parts):
    m = None
    for s, _, _ in parts:
        mk = jnp.max(s, axis=1, keepdims=True)
        m = mk if m is None else jnp.maximum(m, mk)
    l = None
    o = None
    for s, v, channel_major in parts:
        p = jnp.exp(s - m)
        lk = jnp.sum(p, axis=1, keepdims=True)
        pb = p.astype(BF16)
        ok = _dot_nt(pb, v) if channel_major else jnp.dot(pb, v, preferred_element_type=F32)
        l = lk if l is None else l + lk
        o = ok if o is None else o + ok
    return o / l


def _sample_slc_body(pt_ref, kpool_ref, vpool_ref, qbd_ref, selb_ref, gate_ref, oc_ref, ksn_ref, vsn_ref, kwn_ref,
                     vwn_ref, kwin_ref, vwin_ref, near_ref, newtab_ref, wtab_ref, expand_ref,
                     o_ref, kbuf, vbuf, sems, pad_ref, *, rep, t_new):
    b = pl.program_id(0)
    slot = _gather_step(pt_ref, (kpool_ref, vpool_ref), (kbuf, vbuf), sems,
                        lambda buf, sl, p: buf.at[sl, :, pl.ds(p * PAGE_SIZE, PAGE_SIZE)])
    hd = HEAD_DIM
    kvw = N_KV * hd

    @pl.when(b == 0)
    def _():
        pad_ref[...] = jnp.zeros(pad_ref.shape, F32)

    news = []
    for i, ref in enumerate((ksn_ref, vsn_ref, kwn_ref, vwn_ref)):
        pad_ref[i, 0:t_new, :] = ref[0]
        news.append(pad_ref[i].astype(BF16))
    ks_new, vs_new, kw_new, vw_new = news
    qbd = qbd_ref[0]

    per_g = rep * t_new
    sel = selb_ref[0]
    sel_rows = jnp.concatenate([sel[g * t_new:(g + 1) * t_new] for g in range(N_KV) for _ in range(rep)], axis=0)
    s_past = (jnp.dot(qbd, kbuf[slot].astype(BF16), preferred_element_type=F32)
              + jnp.dot(sel_rows.astype(BF16), expand_ref[...], preferred_element_type=F32))
    n_past = s_past.shape[1]
    n_near = near_ref.shape[1]
    s_past = jnp.concatenate([s_past[:, :n_past - n_near], s_past[:, n_past - n_near:] + near_ref[...]], axis=1)
    s_new = _dot_nt(qbd, ks_new) + newtab_ref[0]
    o_s = _softmax_pv([(s_past, vbuf[slot].astype(BF16), True), (s_new, vs_new, False)])

    s_win = jnp.dot(qbd, kwin_ref[0].astype(BF16), preferred_element_type=F32) + wtab_ref[...]
    s_wnew = _dot_nt(qbd, kw_new) + newtab_ref[1]
    o_w = _softmax_pv([(s_win, vwin_ref[0].astype(BF16), True), (s_wnew, vw_new, False)])

    gate = gate_ref[0]
    comb = gate[:, 0:1] * oc_ref[0] + gate[:, 1:2] * o_s + gate[:, 2:3] * o_w
    row = lax.broadcasted_iota(jnp.int32, comb.shape, 0)
    o_ref[0] = jnp.where(row < per_g, comb, pltpu.roll(comb, hd, axis=1))


def _sample_slc(page_table, kpool, vpool, qbd, selb, gate_rows, oc, ks_new, vs_new, kw_new, vw_new, kwin, vwin,
                near, newtab, wtab, expand, *, rep, t_new):
    nb, n_pages = page_table.shape
    n_keys = n_pages * PAGE_SIZE
    n_q = qbd.shape[1]
    kvw = N_KV * HEAD_DIM
    const = lambda a: pl.BlockSpec(a.shape, lambda b, pt: (0,) * a.ndim, pipeline_mode=pl.Buffered(1))
    per_b = lambda a: pl.BlockSpec((1,) + a.shape[1:], lambda b, pt: (b,) + (0,) * (a.ndim - 1))
    grid_spec = pltpu.PrefetchScalarGridSpec(
        num_scalar_prefetch=1,
        grid=(nb,),
        in_specs=[pl.BlockSpec(memory_space=pl.ANY), pl.BlockSpec(memory_space=pl.ANY),
                  per_b(qbd), per_b(selb), per_b(gate_rows), per_b(oc),
                  per_b(ks_new), per_b(vs_new), per_b(kw_new), per_b(vw_new), per_b(kwin), per_b(vwin),
                  const(near), const(newtab), const(wtab), const(expand)],
        out_specs=pl.BlockSpec((1, n_q, kvw), lambda b, pt: (b, 0, 0)),
        scratch_shapes=[pltpu.VMEM((2, kvw, n_keys), F32), pltpu.VMEM((2, kvw, n_keys), F32),
                        pltpu.SemaphoreType.DMA((2, 2)), pltpu.VMEM((4, LANE, kvw), F32)],
    )
    return pl.pallas_call(
        functools.partial(_sample_slc_body, rep=rep, t_new=t_new),
        grid_spec=grid_spec,
        out_shape=jax.ShapeDtypeStruct((nb, n_q, kvw), F32),
        compiler_params=_cparams(1),
        name="sample_slc_win",
    )(page_table, kpool, vpool, qbd, selb, gate_rows, oc, ks_new, vs_new, kw_new, vw_new, kwin, vwin,
      near, newtab, wtab, expand)


def _sample_tables(rel_bias, pos0, t_new, n_rows, n_c, wbuf):
    h = rel_bias.shape[0]
    qpos = pos0 + jnp.arange(t_new, dtype=jnp.int32)

    def table(kpos, ok_extra=None, upper=None):
        d = qpos[:, None] - kpos[None, :]
        ok = d >= 0
        if upper is not None:
            ok = ok & (d < upper)
        if ok_extra is not None:
            ok = ok & ok_extra[None, :]
        tab = jnp.where(ok[None], _bias_minus_far(rel_bias, d, 0), NEG)
        return tab.reshape(h * t_new, kpos.shape[0]).astype(F32)

    c = jnp.arange(n_rows, dtype=jnp.int32)
    cmp = table(c * STRIDE + (L_CMP - 1), ok_extra=c < n_c)
    near = table(pos0 - PAGE_SIZE + jnp.arange(PAGE_SIZE, dtype=jnp.int32))
    lane = jnp.arange(LANE, dtype=jnp.int32)
    new_slc = table(pos0 + lane, ok_extra=lane < t_new)
    new_win = table(pos0 + lane, ok_extra=lane < t_new, upper=WINDOW)
    kpos_w = pos0 - wbuf + jnp.arange(wbuf, dtype=jnp.int32)
    win = table(kpos_w, ok_extra=kpos_w >= 0, upper=WINDOW)
    return cmp, near, jnp.stack([new_slc, new_win]), win


def _layer_weights(p, n_heads):
    d = p["w_in"].shape[0]
    kv_end = d + 6 * N_KV * HEAD_DIM
    src = _gate_column_source(n_heads)
    gate_cols = jnp.where(jnp.asarray(src >= 0)[None, :], p["w_in"][:, kv_end + np.maximum(src, 0)], 0.0)
    ln = jnp.stack([p["ln_gain"][0], p["ln_bias"][0], p["ln_gain"][1], p["ln_bias"][1],
                    p["ln_gain"][2], p["ln_bias"][2]])
    return dict(
        wi1=p["w_ffn1_in"].astype(BF16), wo1=p["w_ffn1_out"].astype(BF16),
        wi2=p["w_ffn2_in"].astype(BF16), wo2=p["w_ffn2_out"].astype(BF16),
        wp=jnp.concatenate([p["w_in"][:, :kv_end], gate_cols], axis=1).astype(BF16),
        wout=p["w_out"].astype(BF16),
        wpool=p["w_pool"].astype(BF16), pool_scale=p["pool_scale"],
        ln1=ln[0:2], ln23=ln[2:6],
        cmp_k=_compress_weights(p["w_phi_k1"], p["w_phi_k2"], p["cmp_pos"]),
        cmp_v=_compress_weights(p["w_phi_v1"], p["w_phi_v2"], p["cmp_pos"]),
    )


PROMPT_TOKENS = 512


def _prompt_layer(x, mod, w, rel_bias, alpha):
    nb, t, d = x.shape
    assert t % TQ == 0 and t % PROMPT_TOKENS == 0
    (x1, u_pool, q, k_c, v_c, k_s, v_s, k_w, v_w, ks_bf, vs_t, kw_bf, vw_t, gates_t) = _ffn_in(
        x, mod, w["wi1"], w["wo1"], w["wp"], w["ln1"], alpha=alpha, nbk=1, tt=PROMPT_TOKENS, for_prompt=True)
    pool_out = _pool_mix(u_pool, None, w["wpool"], w["pool_scale"], pos0=0, nbk=1, tt=PROMPT_TOKENS,
                         name="pool_prompt")
    row_w = ROW_TOKENS * N_KV * HEAD_DIM
    kc_cmp = _compress(k_c.reshape(nb, t // ROW_TOKENS, row_w), *w["cmp_k"], transposed=False, name="compress_k_prompt")
    vc_cmp_t = _compress(v_c.reshape(nb, t // ROW_TOKENS, row_w), *w["cmp_v"], transposed=True, name="compress_v_prompt")
    n_c = (t - L_CMP) // STRIDE + 1
    n_blk = t // L_SEL
    near, tcmp, e0 = _prompt_tables(rel_bias, t)
    nsa = _nsa_prompt(q, gates_t, kc_cmp, vc_cmp_t, ks_bf, vs_t, kw_bf, vw_t, tcmp, near,
                      _cover_t(t // ROW_TOKENS, n_c, n_blk), e0=e0, top_n=min(TOP_N, n_blk))
    y = _ffn_out(pool_out, nsa, x1, mod, w["wout"], w["wi2"], w["wo2"], w["ln23"], alpha=alpha,
                 nbk=1, tt=PROMPT_TOKENS, name="outproj_ffn2_prompt")
    keep = min(WINDOW, t)
    heads = lambda a: a.reshape(nb, -1, N_KV, HEAD_DIM)
    states = (heads(k_c), heads(v_c), heads(k_s), heads(v_s), heads(k_w[:, t - keep:]), heads(v_w[:, t - keep:]),
              u_pool[:, t - POOL_STATE:])
    return y, states


SAMPLE_BATCHES = 32


def _sample_layer(x, mod, w, rel_bias, alpha, page_table, pools, kwin, vwin, pool_state):
    nb, t, d = x.shape
    hd = HEAD_DIM
    n_heads = rel_bias.shape[0]
    rep = n_heads // N_KV
    kvw = N_KV * hd
    pos0 = page_table.shape[1] * PAGE_SIZE
    n_c = (pos0 + t - L_CMP) // STRIDE + 1
    assert (n_c - 1) * STRIDE + L_CMP <= pos0 and pos0 % PAGE_SIZE == 0 and nb % SAMPLE_BATCHES == 0
    n_rows = pos0 // ROW_TOKENS
    n_blk = -(-(pos0 + t) // L_SEL)
    assert n_blk - 1 == pos0 // L_SEL
    blk_pad = -(-n_blk // LANE) * LANE
    wbuf = kwin.shape[1]

    x1, u_pool, q, k_c, v_c, k_s, v_s, k_w, v_w, gates = _ffn_in(
        x, mod, w["wi1"], w["wo1"], w["wp"], w["ln1"], alpha=alpha, nbk=SAMPLE_BATCHES, tt=t, for_prompt=False)
    pw = u_pool.shape[-1]
    prev = jnp.concatenate([jnp.zeros((nb, PREV_ROWS - POOL_STATE, pw), F32), pool_state], axis=1)
    pool_out = _pool_mix(u_pool, prev, w["wpool"], w["pool_scale"], pos0=pos0, nbk=SAMPLE_BATCHES, tt=t,
                         name="pool_sample")

    qh = q.reshape(nb, t, N_KV, rep, hd).transpose(0, 2, 3, 1, 4)
    own = jnp.eye(N_KV, dtype=q.dtype)[None, :, None, None, :, None]
    qbd = (qh[:, :, :, :, None, :] * own).reshape(nb, N_KV * rep * t, kvw)
    gate_rows = gates[:, :, :N_KV * GATE_ROWS].reshape(nb, t, N_KV, GATE_ROWS)[..., :3 * rep]
    gate_rows = gate_rows.reshape(nb, t, N_KV, 3, rep).transpose(0, 2, 4, 1, 3).reshape(nb, N_KV * rep * t, 3)
    gate_rows = jnp.pad(gate_rows, ((0, 0), (0, 0), (0, LANE - 3)))

    tab_c, near, newtab, wtab = _sample_tables(rel_bias, pos0, t, n_rows, n_c, wbuf)
    cov = np.zeros((n_rows, blk_pad), np.float32)
    ci = np.arange(n_rows)[:, None]
    bj = np.arange(blk_pad)[None, :]
    cov[:] = ((ci * STRIDE <= (bj + 1) * L_SEL - 1) & (ci * STRIDE + L_CMP - 1 >= bj * L_SEL)
              & (ci < n_c) & (bj < n_blk))
    channel_major = lambda a: a.transpose(0, 2, 3, 1).reshape(a.shape[0], kvw, a.shape[1])
    oc, imp = _sample_cmp(page_table, channel_major(pools[0]), channel_major(pools[1]), qbd, tab_c, w["cmp_k"],
                          w["cmp_v"], jnp.asarray(cov, BF16), rep=rep, t_new=t)

    sel_rows = -(-n_blk // 8) * 8
    imp_t = imp.transpose(2, 0, 1).reshape(blk_pad, nb * N_KV * t)[:sel_rows]
    selb_t = _select_blocks(imp_t, top_n=min(TOP_N, n_blk), n_blk=n_blk, pos0=pos0, t_new=t)
    selb_t = jnp.pad(selb_t, ((0, blk_pad - sel_rows), (0, 0)), constant_values=NEG)
    selb = selb_t.reshape(blk_pad, nb, N_KV * t).transpose(1, 2, 0)

    key_blk = lax.broadcasted_iota(jnp.int32, (blk_pad, pos0), 1) // L_SEL
    expand = (key_blk == lax.broadcasted_iota(jnp.int32, (blk_pad, pos0), 0)).astype(BF16)
    kvf = lambda a: a.reshape(a.shape[0], a.shape[1], kvw)
    o = _sample_slc(page_table, channel_major(pools[2]), channel_major(pools[3]), qbd, selb, gate_rows, oc,
                    k_s, v_s, k_w, v_w, channel_major(kwin), channel_major(vwin), near, newtab, wtab, expand,
                    rep=rep, t_new=t)
    nsa = o[:, :, :hd].reshape(nb, N_KV, rep, t, hd).transpose(0, 3, 1, 2, 4).reshape(nb, t, n_heads * hd).astype(BF16)

    y = _ffn_out(pool_out, nsa, x1, mod, w["wout"], w["wi2"], w["wo2"], w["ln23"], alpha=alpha,
                 nbk=SAMPLE_BATCHES, tt=t, name="outproj_ffn2_sample")
    keep = min(WINDOW, pos0 + t)
    heads = lambda a: a.reshape(nb, -1, N_KV, HEAD_DIM)
    kw_ext = jnp.concatenate([kvf(kwin), k_w], axis=1)
    vw_ext = jnp.concatenate([kvf(vwin), v_w], axis=1)
    pool_ext = jnp.concatenate([pool_state, u_pool], axis=1)
    states = (heads(k_c), heads(v_c), heads(k_s), heads(v_s), heads(kw_ext[:, kw_ext.shape[1] - keep:]),
              heads(vw_ext[:, vw_ext.shape[1] - keep:]), pool_ext[:, pool_ext.shape[1] - POOL_STATE:])
    return y, states


def kernel(x_prompt, x_sample, c_prompt, c_sample, cache_k_cmp, cache_v_cmp, cache_k_slc, cache_v_slc, page_table,
           state_k_win, state_v_win, state_pool, w_ada, b_ada, ln_gain, ln_bias, w_ffn1_in, w_ffn1_out, w_ffn2_in,
           w_ffn2_out, w_in, w_phi_k1, w_phi_k2, w_phi_v1, w_phi_v2, cmp_pos, w_pool, pool_scale, w_out, rel_bias):
    depth = w_ada.shape[0]
    alpha = (2.0 * depth) ** 0.25
    n_heads = rel_bias.shape[0]
    nb_p, nb_s = x_prompt.shape[0], x_sample.shape[0]
    d = x_prompt.shape[-1]
    xp, xs = x_prompt, x_sample
    c_all = jnp.concatenate([c_prompt, c_sample], axis=0)
    p_states, s_states = [], []
    for l in range(depth):
        p = dict(w_ada=w_ada[l], b_ada=b_ada[l], ln_gain=ln_gain[l], ln_bias=ln_bias[l], w_ffn1_in=w_ffn1_in[l],
                 w_ffn1_out=w_ffn1_out[l], w_ffn2_in=w_ffn2_in[l], w_ffn2_out=w_ffn2_out[l], w_in=w_in[l],
                 w_phi_k1=w_phi_k1[l], w_phi_k2=w_phi_k2[l], w_phi_v1=w_phi_v1[l], w_phi_v2=w_phi_v2[l],
                 cmp_pos=cmp_pos[l], w_pool=w_pool[l], pool_scale=pool_scale[l], w_out=w_out[l])
        w = _layer_weights(p, n_heads)
        mod = _ada(c_all, p["w_ada"], p["b_ada"]).reshape(nb_p + nb_s, 9, d)
        xp, sp = _prompt_layer(xp, mod[:nb_p], w, rel_bias, alpha)
        xs, ss = _sample_layer(xs, mod[nb_p:], w, rel_bias, alpha, page_table,
                               (cache_k_cmp[l], cache_v_cmp[l], cache_k_slc[l], cache_v_slc[l]),
                               state_k_win[l], state_v_win[l], state_pool[l])
        p_states.append(sp)
        s_states.append(ss)
    stack = lambda states: tuple(jnp.stack(a) for a in zip(*states))
    return (xp, xs) + stack(p_states) + stack(s_states)
```

```python
import functools
import math

import numpy as np
import jax
import jax.numpy as jnp
from jax import lax
from jax.experimental import pallas as pl
from jax.experimental.pallas import tpu as pltpu

F32 = jnp.float32
BF16 = jnp.bfloat16

HEAD_DIM = 64
N_KV = 2
L_CMP = 32
STRIDE = 16
L_SEL = 64
TOP_N = 16
WINDOW = 512
N_BUCKETS = 32
MAX_DIST = 128
POOL_WINDOWS = (2, 4, 8, 16)
POOL_STATE = max(POOL_WINDOWS) - 1
LN_EPS = 1e-5
NEG = -1e30
PAGE_SIZE = 128

Q_SCALE = HEAD_DIM ** -0.5 * math.log2(math.e)
TQ = 128
FAR_CHUNKS = 4
FAR_BLOCKS = FAR_CHUNKS * TQ // L_SEL
VT_ROWS = HEAD_DIM + 16

LANE = 128
MXU_COLS = 256
FF_CHUNK = MXU_COLS
VMEM_LIMIT = 56 * 1024 * 1024


def _cparams(n_axes):
    return pltpu.CompilerParams(dimension_semantics=("arbitrary",) * n_axes,
                                vmem_limit_bytes=VMEM_LIMIT)


def _const_spec(shape):
    nd = len(shape)
    return pl.BlockSpec(shape, lambda *_: (0,) * nd, pipeline_mode=pl.Buffered(1))


def _layer_norm(y, gain, bias):
    mu = jnp.mean(y, axis=-1, keepdims=True)
    yc = y - mu
    var = jnp.mean(yc * yc, axis=-1, keepdims=True)
    return yc * lax.rsqrt(var + LN_EPS) * gain + bias


def _swiglu(u_bf, wi_ref, wo_ref):
    d_ff = wo_ref.shape[0]
    acc = None
    for c in range(d_ff // FF_CHUNK):
        lo = c * FF_CHUNK
        a = jnp.dot(u_bf, wi_ref[:, lo:lo + FF_CHUNK], preferred_element_type=F32)
        b = jnp.dot(u_bf, wi_ref[:, d_ff + lo:d_ff + lo + FF_CHUNK], preferred_element_type=F32)
        hid = (a * jax.nn.sigmoid(a) * b).astype(BF16)
        part = jnp.dot(hid, wo_ref[lo:lo + FF_CHUNK, :], preferred_element_type=F32)
        acc = part if acc is None else acc + part
    return acc


def _ada_body(c_ref, w_ref, b_ref, o_ref):
    c = c_ref[...]
    act = (c * jax.nn.sigmoid(c)).astype(BF16)
    o_ref[...] = jnp.dot(act, w_ref[...].astype(BF16), preferred_element_type=F32) + b_ref[...]


def _ada(c_all, w_ada, b_ada):
    nb, d = c_all.shape
    n_out = w_ada.shape[1]
    cols = 9 * LANE
    return pl.pallas_call(
        _ada_body,
        grid=(n_out // cols,),
        in_specs=[pl.BlockSpec((nb, d), lambda j: (0, 0)),
                  pl.BlockSpec((d, cols), lambda j: (0, j)),
                  pl.BlockSpec((1, cols), lambda j: (0, j))],
        out_specs=pl.BlockSpec((nb, cols), lambda j: (0, j)),
        out_shape=jax.ShapeDtypeStruct((nb, n_out), F32),
        compiler_params=_cparams(1),
        name="ada",
    )(c_all, w_ada, b_ada.reshape(1, n_out))


KV_NAMES = ("k_c", "v_c", "k_s", "v_s", "k_w", "v_w")


def _ffn_in_body(x_ref, mod_ref, wi_ref, wo_ref, wp_ref, ln_ref, *out_refs, alpha, pool_w, nsa_w, kv_w, for_prompt):
    nbk, tt, d = x_ref.shape
    n = nbk * tt
    x = x_ref[...]
    mod = mod_ref[...]
    u = (x * (1.0 + mod[:, 1:2, :]) + mod[:, 0:1, :]).reshape(n, d).astype(BF16)
    h = _swiglu(u, wi_ref, wo_ref).reshape(nbk, tt, d)
    x1 = _layer_norm(alpha * x + 0.5 * mod[:, 2:3, :] * h, ln_ref[0:1, :], ln_ref[1:2, :])
    u1 = (x1 * (1.0 + mod[:, 4:5, :]) + mod[:, 3:4, :]).reshape(n, d).astype(BF16)
    proj = jnp.dot(u1, wp_ref[...], preferred_element_type=F32)

    x1_ref, upool_ref, q_ref = out_refs[:3]
    kv_refs = out_refs[3:9]
    x1_ref[...] = x1
    upool_ref[...] = proj[:, :pool_w].reshape(nbk, tt, pool_w)
    q_ref[...] = (proj[:, pool_w:pool_w + nsa_w] * Q_SCALE).astype(BF16).reshape(nbk, tt, nsa_w)
    off = pool_w + nsa_w
    kv = []
    for i in range(6):
        blk = proj[:, off + i * kv_w:off + (i + 1) * kv_w]
        kv.append(blk)
        kv_refs[i][...] = blk.reshape(nbk, tt, kv_w)
    gates = jax.nn.sigmoid(proj[:, off + 6 * kv_w:off + 6 * kv_w + LANE])
    if for_prompt:
        ksb_ref, vst_ref, kwb_ref, vwt_ref, gt_ref = out_refs[9:]
        row = lax.broadcasted_iota(jnp.int32, (n, kv_w), 0)
        lane = lax.broadcasted_iota(jnp.int32, (n, kv_w), 1)
        onehot = jnp.where((row // L_SEL) % FAR_BLOCKS == lane, 1.0, 0.0)
        ones_row = jnp.where(lax.broadcasted_iota(jnp.int32, (VT_ROWS - HEAD_DIM, n), 0) == 0, 1.0, 0.0)

        def values_t(x):
            xt = x.T
            return jnp.concatenate([piece for g in range(N_KV)
                                    for piece in (xt[g * HEAD_DIM:(g + 1) * HEAD_DIM], ones_row)], axis=0).astype(BF16)

        ksb_ref[0] = jnp.concatenate([kv[2], onehot], axis=1).astype(BF16)
        vst_ref[0] = values_t(kv[3])
        kwb_ref[0] = kv[4].astype(BF16)
        vwt_ref[0] = values_t(kv[5])
        gt_ref[0] = gates.T[:gt_ref.shape[1], :]
    else:
        out_refs[9][...] = gates.reshape(nbk, tt, LANE)


def _ffn_in(x, mod, wi, wo, wp, ln, *, alpha, nbk, tt, for_prompt):
    nb, t, d = x.shape
    pool_w = d // 2
    nsa_w = d - pool_w
    kv_w = N_KV * HEAD_DIM
    grid = (nb // nbk, t // tt)
    tok = lambda w: pl.BlockSpec((nbk, tt, w), lambda i, j: (i, j, 0))
    out_specs = [tok(d), tok(pool_w), tok(nsa_w)] + [tok(kv_w)] * 6
    out_shape = ([jax.ShapeDtypeStruct((nb, t, d), F32), jax.ShapeDtypeStruct((nb, t, pool_w), F32),
                  jax.ShapeDtypeStruct((nb, t, nsa_w), BF16)]
                 + [jax.ShapeDtypeStruct((nb, t, kv_w), F32)] * 6)
    if for_prompt:
        assert nbk == 1 and tt % (FAR_BLOCKS * L_SEL) == 0
        tr = lambda rows: pl.BlockSpec((1, rows, tt), lambda i, j: (i, 0, j))
        vt_rows = N_KV * VT_ROWS
        out_specs += [tok(2 * kv_w), tr(vt_rows), tok(kv_w), tr(vt_rows), tr(N_KV * GATE_ROWS)]
        out_shape += [jax.ShapeDtypeStruct((nb, t, 2 * kv_w), BF16), jax.ShapeDtypeStruct((nb, vt_rows, t), BF16),
                      jax.ShapeDtypeStruct((nb, t, kv_w), BF16), jax.ShapeDtypeStruct((nb, vt_rows, t), BF16),
                      jax.ShapeDtypeStruct((nb, N_KV * GATE_ROWS, t), F32)]
    else:
        out_specs += [tok(LANE)]
        out_shape += [jax.ShapeDtypeStruct((nb, t, LANE), F32)]
    body = functools.partial(_ffn_in_body, alpha=alpha, pool_w=pool_w, nsa_w=nsa_w, kv_w=kv_w,
                             for_prompt=for_prompt)
    return pl.pallas_call(
        body,
        grid=grid,
        in_specs=[tok(d),
                  pl.BlockSpec((nbk, mod.shape[1], d), lambda i, j: (i, 0, 0)),
                  _const_spec(wi.shape), _const_spec(wo.shape), _const_spec(wp.shape), _const_spec(ln.shape)],
        out_specs=out_specs,
        out_shape=out_shape,
        compiler_params=_cparams(2),
        name="ffn1_inproj_prompt" if for_prompt else "ffn1_inproj_sample",
    )(x, mod, wi, wo, wp, ln)


GATE_ROWS = 16


def _gate_column_source(n_heads):
    rep = n_heads // N_KV
    src = np.full((LANE,), -1, np.int32)
    for g in range(N_KV):
        for br in range(3):
            for r in range(rep):
                src[g * GATE_ROWS + br * rep + r] = br * n_heads + g * rep + r
    return src


def _ffn_out_body(pool_ref, nsa_ref, x1_ref, mod_ref, wout_ref, wi_ref, wo_ref, ln_ref, o_ref, *, alpha):
    nbk, tt, d = x1_ref.shape
    n = nbk * tt
    pw = pool_ref.shape[-1]
    mod = mod_ref[...]
    x1 = x1_ref[...]
    mix = (jnp.dot(pool_ref[...].reshape(n, pw), wout_ref[:pw, :], preferred_element_type=F32)
           + jnp.dot(nsa_ref[...].reshape(n, d - pw), wout_ref[pw:, :], preferred_element_type=F32))
    x2 = _layer_norm(alpha * x1 + mod[:, 5:6, :] * mix.reshape(nbk, tt, d), ln_ref[0:1, :], ln_ref[1:2, :])
    u = (x2 * (1.0 + mod[:, 7:8, :]) + mod[:, 6:7, :]).reshape(n, d).astype(BF16)
    h = _swiglu(u, wi_ref, wo_ref).reshape(nbk, tt, d)
    o_ref[...] = _layer_norm(alpha * x2 + 0.5 * mod[:, 8:9, :] * h, ln_ref[2:3, :], ln_ref[3:4, :])


def _ffn_out(pool_out, nsa_out, x1, mod, wout, wi, wo, ln, *, alpha, nbk, tt, name):
    nb, t, d = x1.shape
    tok = lambda w: pl.BlockSpec((nbk, tt, w), lambda i, j: (i, j, 0))
    return pl.pallas_call(
        functools.partial(_ffn_out_body, alpha=alpha),
        grid=(nb // nbk, t // tt),
        in_specs=[tok(pool_out.shape[-1]), tok(nsa_out.shape[-1]), tok(d),
                  pl.BlockSpec((nbk, mod.shape[1], d), lambda i, j: (i, 0, 0)),
                  _const_spec(wout.shape), _const_spec(wi.shape), _const_spec(wo.shape), _const_spec(ln.shape)],
        out_specs=tok(d),
        out_shape=jax.ShapeDtypeStruct((nb, t, d), F32),
        compiler_params=_cparams(2),
        name=name,
    )(pool_out, nsa_out, x1, mod, wout, wi, wo, ln)


PREV_ROWS = 16


def _pool_body(u_ref, prev_ref, w_ref, scale_ref, o_ref, ext_ref, *, pos0, zero_first):
    nbk, tt, width = u_ref.shape
    gw = width // len(POOL_WINDOWS)
    j = pl.program_id(1)
    u = u_ref[...]
    prev = prev_ref[...]
    if zero_first:
        prev = jnp.where(j == 0, 0.0, prev)
    ext_ref[:, 0:PREV_ROWS, :] = prev
    ext_ref[:, PREV_ROWS:, :] = u
    pos = pos0 + j * tt + lax.broadcasted_iota(jnp.int32, (1, tt, 1), 1)
    for g, w in enumerate(POOL_WINDOWS):
        lo = g * gw
        tot = None
        for k in range(w):
            part = ext_ref[:, PREV_ROWS - k:PREV_ROWS - k + tt, lo:lo + gw]
            tot = part if tot is None else tot + part
        cnt = jnp.minimum(pos + 1, w).astype(F32)
        pooled = (tot / cnt - u[:, :, lo:lo + gw]).astype(BF16).reshape(nbk * tt, gw)
        mixed = jnp.dot(pooled, w_ref[g], preferred_element_type=F32) * scale_ref[:, lo:lo + gw]
        o_ref[:, :, lo:lo + gw] = mixed.astype(BF16).reshape(nbk, tt, gw)


def _pool_mix(u_pool, prev, w_pool_bf, pool_scale, *, pos0, nbk, tt, name):
    nb, t, width = u_pool.shape
    tok = pl.BlockSpec((nbk, tt, width), lambda i, j: (i, j, 0))
    if prev is None:
        assert nbk == 1 and tt % PREV_ROWS == 0 and pos0 == 0
        step = tt // PREV_ROWS
        prev_arr = u_pool
        prev_spec = pl.BlockSpec((1, PREV_ROWS, width), lambda i, j: (i, jnp.maximum(j * step - 1, 0), 0))
    else:
        assert t == tt
        prev_arr = prev
        prev_spec = pl.BlockSpec((nbk, PREV_ROWS, width), lambda i, j: (i, 0, 0))
    return pl.pallas_call(
        functools.partial(_pool_body, pos0=pos0, zero_first=prev is None),
        grid=(nb // nbk, t // tt),
        in_specs=[tok, prev_spec, _const_spec(w_pool_bf.shape), _const_spec((1, width))],
        out_specs=tok,
        out_shape=jax.ShapeDtypeStruct((nb, t, width), BF16),
        scratch_shapes=[pltpu.VMEM((nbk, PREV_ROWS + tt, width), F32)],
        compiler_params=_cparams(2),
        name=name,
    )(u_pool, prev_arr, w_pool_bf, pool_scale.reshape(1, width))


ROW_TOKENS = STRIDE


def _compress_body(r_ref, wexp_ref, w2_ref, pos_ref, o_ref, sh_ref, *, transposed):
    nr = r_ref.shape[1]
    half = wexp_ref.shape[1] // 2
    p = jnp.dot(r_ref[0].astype(BF16), wexp_ref[...], preferred_element_type=F32)
    pp = jnp.dot(pos_ref[...], wexp_ref[...], preferred_element_type=F32)
    posb = pp[0:1, :half] + pp[1:2, half:]
    sh_ref[0:nr, :] = p[:, half:]
    sh_ref[nr:nr + 8, :] = jnp.zeros((8, half), F32)
    h = p[:, :half] + sh_ref[1:nr + 1, :] + posb
    c = jnp.dot(jax.nn.gelu(h).astype(BF16), w2_ref[...], preferred_element_type=F32)
    o_ref[0] = (c.T if transposed else c).astype(BF16)


def _compress(rows, wexp, w2bd, posrows, *, transposed, name):
    nb, nr, k = rows.shape
    half = wexp.shape[1] // 2
    oshape = (nb, half, nr) if transposed else (nb, nr, half)
    return pl.pallas_call(
        functools.partial(_compress_body, transposed=transposed),
        grid=(nb,),
        in_specs=[pl.BlockSpec((1, nr, k), lambda b: (b, 0, 0)),
                  _const_spec(wexp.shape), _const_spec(w2bd.shape), _const_spec(posrows.shape)],
        out_specs=pl.BlockSpec((1,) + oshape[1:], lambda b: (b, 0, 0)),
        out_shape=jax.ShapeDtypeStruct(oshape, BF16),
        scratch_shapes=[pltpu.VMEM((nr + 8, half), F32)],
        compiler_params=_cparams(1),
        name=name,
    )(rows, wexp, w2bd, posrows)


def _compress_weights(w1, w2, cmp_pos):
    eye = jnp.eye(N_KV, dtype=F32)
    w_blk = w1.reshape(L_CMP, HEAD_DIM, HEAD_DIM)
    halves = []
    for part in range(L_CMP // ROW_TOKENS):
        wpart = w_blk[part * ROW_TOKENS:(part + 1) * ROW_TOKENS]
        halves.append(jnp.einsum("gh,lio->lgiho", eye, wpart).reshape(ROW_TOKENS * N_KV * HEAD_DIM, N_KV * HEAD_DIM))
    wexp = jnp.concatenate(halves, axis=1).astype(BF16)
    w2bd = jnp.einsum("gh,io->giho", eye, w2).reshape(N_KV * HEAD_DIM, N_KV * HEAD_DIM).astype(BF16)
    pos = jnp.broadcast_to(cmp_pos.reshape(L_CMP // ROW_TOKENS, ROW_TOKENS, 1, HEAD_DIM),
                           (L_CMP // ROW_TOKENS, ROW_TOKENS, N_KV, HEAD_DIM)).reshape(L_CMP // ROW_TOKENS, -1)
    posrows = jnp.zeros((8, pos.shape[1]), F32).at[:pos.shape[0]].set(pos).astype(BF16)
    return wexp, w2bd, posrows


def _bucket_starts():
    d = np.arange(0, 4 * MAX_DIST)
    exact = N_BUCKETS // 2
    nf = np.maximum(d, 1).astype(np.float32)
    large = exact + (np.log(nf / np.float32(exact)) / np.float32(math.log(MAX_DIST / exact))
                     * np.float32(N_BUCKETS - exact)).astype(np.int32)
    bucket = np.where(d < exact, d, np.minimum(large, N_BUCKETS - 1))
    assert bucket.max() == N_BUCKETS - 1 and np.all(np.diff(bucket) >= 0)
    return [int(np.argmax(bucket >= b)) for b in range(1, N_BUCKETS)]


def _bias_minus_far(rel_bias, dist, head_axis):
    tab = (rel_bias - rel_bias[:, N_BUCKETS - 1:]) * math.log2(math.e)
    hshape = (1,) * head_axis + (-1,) + (1,) * (dist.ndim - head_axis)
    dist = jnp.expand_dims(dist, head_axis)
    out = jnp.broadcast_to(tab[:, 0].reshape(hshape),
                           dist.shape[:head_axis] + (tab.shape[0],) + dist.shape[head_axis + 1:])
    for b, start in enumerate(_bucket_starts(), start=1):
        out = jnp.where(dist >= start, tab[:, b].reshape(hshape), out)
    return out


def _prompt_tables(rel_bias, t):
    h = rel_bias.shape[0]
    qo = jnp.arange(TQ, dtype=jnp.int32)
    n_near = WINDOW // L_SEL + 2 * (TQ // L_SEL)
    dd = (WINDOW // L_SEL) - jnp.arange(n_near, dtype=jnp.int32)
    ko = jnp.arange(L_SEL, dtype=jnp.int32)
    d = (L_SEL * dd[:, None, None] + qo[None, None, :] - ko[None, :, None]).reshape(n_near * L_SEL, TQ)
    ok = (d >= 0) & (d < WINDOW)
    near = jnp.where(ok[:, None, :], _bias_minus_far(rel_bias, d, 1), NEG)
    near = near.reshape(n_near * L_SEL, h * TQ)
    n_tiles = t // TQ
    per_tile = TQ // STRIDE
    e0 = per_tile * (n_tiles - 1)
    rows = e0 + t // STRIDE
    e = e0 - jnp.arange(rows, dtype=jnp.int32)
    dc = STRIDE * e[:, None] - (L_CMP - 1) + qo[None, :]
    cmp = jnp.where((dc >= 0)[:, None, :], _bias_minus_far(rel_bias, dc, 1), NEG)
    cmp = cmp.reshape(rows, h * TQ)
    return near.astype(F32), cmp.astype(F32), e0


def _cover_t(n_rows, n_c, n_blk):
    i = np.arange(n_rows)[None, :]
    j = np.arange(n_blk)[:, None]
    start = i * STRIDE
    end = start + L_CMP - 1
    cov = (start <= (j + 1) * L_SEL - 1) & (end >= j * L_SEL) & (i < n_c)
    return jnp.asarray(cov.astype(np.float32), BF16)


def _descending_rank(v, n_valid, n_live=None):
    rows, cols = v.shape
    sub = lax.broadcasted_iota(jnp.int32, (8, cols), 0)

    def count(rank, lo):
        mid = v[lo:lo + 8]
        for jp in range(lo, min(lo + 8, n_valid)):
            other = v[jp:jp + 1, :]
            parts = [jnp.where(other > mid, 1.0, jnp.where((other == mid) & (sub > jp - lo), 1.0, 0.0))]
            if lo > 0:
                parts.insert(0, jnp.where(other > v[:lo], 1.0, 0.0))
            if lo + 8 < rows:
                parts.append(jnp.where(other >= v[lo + 8:], 1.0, 0.0))
            rank = rank + jnp.concatenate(parts, axis=0)
        return rank

    rank = jnp.zeros((rows, cols), F32)
    for lo in range(0, n_valid, 8):
        if n_live is None:
            rank = count(rank, lo)
        else:
            rank = lax.cond(lo < n_live, functools.partial(count, lo=lo), lambda r: r, rank)
    return rank


def _nsa_prompt_body(q_ref, gt_ref, kc_ref, vct_ref, ks_ref, vst_ref, kw_ref, vwt_ref, tcmp_ref, near_ref, cov_ref,
                     o_ref, selb_ref, qaug_ref, *, e0, top_n):
    it = pl.program_id(1)
    hd = HEAD_DIM
    rep = q_ref.shape[2] // (N_KV * hd)
    gc = rep * TQ
    cols = N_KV * gc
    n_blk = cov_ref.shape[0]
    per = TQ // L_SEL

    def per_group(x):
        return jnp.concatenate([x[:, g * TQ:(g + 1) * TQ] for g in range(N_KV) for _ in range(rep)], axis=1)

    def pv(vt_ref, keys, p, rows=VT_ROWS):
        return jnp.concatenate(
            [jnp.dot(vt_ref[0, g * rows:(g + 1) * rows, keys], p[:, g * gc:(g + 1) * gc],
                     preferred_element_type=F32) for g in range(N_KV)], axis=1)

    qt = q_ref[0].astype(F32).T
    zero = jnp.zeros((hd, TQ), F32)
    tiles = []
    for g in range(N_KV):
        for r in range(rep):
            blk = qt[(g * rep + r) * hd:(g * rep + r + 1) * hd, :]
            tiles.append(jnp.concatenate([blk if gg == g else zero for gg in range(N_KV)], axis=0))
    qbd = jnp.concatenate(tiles, axis=1).astype(BF16)

    gate = [jnp.concatenate([gt_ref[0, g * GATE_ROWS + br * rep + r:g * GATE_ROWS + br * rep + r + 1, :]
                             for g in range(N_KV) for r in range(rep)], axis=1) for br in range(3)]

    n_ck = kc_ref.shape[1]
    sc = jnp.dot(kc_ref[0], qbd, preferred_element_type=F32)
    t0 = pl.multiple_of(e0 - (TQ // STRIDE) * it, 8)
    sc = sc + tcmp_ref[pl.ds(t0, n_ck), :]
    mc = jnp.max(sc, axis=0, keepdims=True)
    pc = jnp.where(sc > 0.1 * NEG, jnp.exp2(sc - mc), 0.0)
    lc = jnp.sum(pc, axis=0, keepdims=True)
    pn = (pc * jnp.where(lc > 0.0, 1.0 / lc, 0.0)).astype(BF16)
    oc = pv(vct_ref, slice(None), pn, hd)

    imp = jnp.dot(cov_ref[...], pn, preferred_element_type=F32)
    v = []
    for g in range(N_KV):
        vg = imp[:, g * gc:g * gc + TQ]
        for r in range(1, rep):
            vg = vg + imp[:, g * gc + r * TQ:g * gc + (r + 1) * TQ]
        v.append(vg)
    v = jnp.concatenate(v, axis=1)
    j = lax.broadcasted_iota(jnp.int32, v.shape, 0)
    lane = lax.broadcasted_iota(jnp.int32, v.shape, 1)
    cur = per * it + (lane % TQ) // L_SEL
    forced = (j == 0) | (j == cur) | (j == cur - 1)
    v = jnp.where(forced, 1e9, jnp.where(j <= cur, v, -1e9))
    rank = _descending_rank(v, n_blk, n_live=per * (it + 1))
    selb = jnp.where((rank < top_n) & (j <= cur), 0.0, NEG)

    n_wc = WINDOW // TQ + 1
    sw = []
    offs = []
    for k in range(n_wc):
        kidx = it - (n_wc - 1) + k
        offk = pl.multiple_of(jnp.maximum(kidx, 0) * TQ, TQ)
        offs.append(offk)
        s = jnp.dot(kw_ref[0, pl.ds(offk, TQ), :], qbd, preferred_element_type=F32)
        sw.append(s + near_ref[k * TQ:(k + 1) * TQ, :] + jnp.where(kidx >= 0, 0.0, NEG))
    sw = jnp.concatenate(sw, axis=0)
    mw = jnp.max(sw, axis=0, keepdims=True)
    pw = jnp.exp2(sw - mw).astype(BF16)
    ow = None
    for k in range(n_wc):
        part = pv(vwt_ref, pl.ds(offs[k], TQ), pw[k * TQ:(k + 1) * TQ, :])
        ow = part if ow is None else ow + part

    diag0 = jnp.maximum(it - 1, 0)
    selb_ref[...] = per_group(jnp.where(j < per * diag0, selb, NEG))
    gk = FAR_CHUNKS * TQ
    kvw = N_KV * hd
    qaug_ref[0:kvw, :] = qbd
    qaug_ref[kvw + 16:, :] = jnp.zeros((qaug_ref.shape[0] - kvw - 16, cols), BF16)

    def online_step(carry, s, keys):
        m, acc = carry
        m_new = jnp.maximum(m, jnp.max(s, axis=0, keepdims=True))
        p = jnp.exp2(s - m_new).astype(BF16)
        return m_new, jnp.exp2(m - m_new) * acc + pv(vst_ref, keys, p)

    def far_group(cb, carry):
        off = pl.multiple_of(cb * gk, gk)
        sel = selb_ref[pl.ds(pl.multiple_of(cb * FAR_BLOCKS, FAR_BLOCKS), FAR_BLOCKS), :]
        qaug_ref[kvw:kvw + 16, :] = jnp.concatenate([sel, jnp.zeros((16 - FAR_BLOCKS, cols), F32)], axis=0).astype(BF16)
        s = jnp.dot(ks_ref[0, pl.ds(off, gk), :], qaug_ref[...], preferred_element_type=F32)
        return online_step(carry, s, pl.ds(off, gk))

    carry = (jnp.full((1, cols), NEG, F32), jnp.zeros((VT_ROWS, cols), F32))
    carry = lax.fori_loop(0, (diag0 + FAR_CHUNKS - 1) // FAR_CHUNKS, far_group, carry)

    dk = 2 * TQ
    offd = pl.multiple_of(diag0 * TQ, TQ)
    near_off = pl.multiple_of((WINDOW // L_SEL - per * (it - diag0)) * L_SEL, TQ)
    sd = (jnp.dot(ks_ref[0, pl.ds(offd, dk), 0:kvw], qbd, preferred_element_type=F32)
          + near_ref[pl.ds(near_off, dk), :])
    rows = []
    for k in range(dk // L_SEL):
        rk = jnp.max(jnp.where(j == per * diag0 + k, selb, NEG), axis=0, keepdims=True)
        rows.append(jnp.broadcast_to(per_group(rk), (L_SEL, cols)))
    sd = sd + jnp.concatenate(rows, axis=0)
    _, acc_s = online_step(carry, sd, pl.ds(offd, dk))

    out_t = (gate[0] * oc + (gate[1] / acc_s[hd:hd + 1]) * acc_s[:hd]
             + (gate[2] / ow[hd:hd + 1]) * ow[:hd])
    pieces = []
    for c in range(0, N_KV * rep, 2):
        pair = jnp.concatenate([out_t[:, c * TQ:(c + 1) * TQ], out_t[:, (c + 1) * TQ:(c + 2) * TQ]], axis=0)
        pieces.append(pair.T)
    o_ref[0] = jnp.concatenate(pieces, axis=1).astype(BF16)


def _nsa_prompt(q, gt, kc, vct, ks, vst, kw, vwt, tcmp, near, cov_t, *, e0, top_n):
    nb, t, nsa_w = q.shape
    kv_w = ks.shape[2]
    cols = (nsa_w // HEAD_DIM) * TQ
    per_b = lambda shape: pl.BlockSpec((1,) + shape, lambda b, i: (b, 0, 0))
    return pl.pallas_call(
        functools.partial(_nsa_prompt_body, e0=e0, top_n=top_n),
        grid=(nb, t // TQ),
        in_specs=[pl.BlockSpec((1, TQ, nsa_w), lambda b, i: (b, i, 0)),
                  pl.BlockSpec((1, N_KV * GATE_ROWS, TQ), lambda b, i: (b, 0, i)),
                  per_b(kc.shape[1:]), per_b(vct.shape[1:]),
                  per_b(ks.shape[1:]), per_b(vst.shape[1:]), per_b(kw.shape[1:]), per_b(vwt.shape[1:]),
                  _const_spec(tcmp.shape), _const_spec(near.shape), _const_spec(cov_t.shape)],
        out_specs=pl.BlockSpec((1, TQ, nsa_w), lambda b, i: (b, i, 0)),
        out_shape=jax.ShapeDtypeStruct((nb, t, nsa_w), BF16),
        scratch_shapes=[pltpu.VMEM((cov_t.shape[0], cols), F32), pltpu.VMEM((ks.shape[2], cols), BF16)],
        compiler_params=_cparams(2),
        name="nsa_prompt",
    )(q, gt, kc, vct, ks, vst, kw, vwt, tcmp, near, cov_t)


def _dot_nt(a, b):
    return lax.dot_general(a, b, (((1,), (1,)), ((), ())), preferred_element_type=F32)


def _page_copies(pt_ref, pools, bufs, sems, batch, slot, dst):
    n_pages = pt_ref.shape[1]
    out = []
    for i, (pool, buf) in enumerate(zip(pools, bufs)):
        for p in range(n_pages):
            out.append(pltpu.make_async_copy(pool.at[pt_ref[batch, p]], dst(buf, slot, p), sems.at[i, slot]))
    return out


def _gather_step(pt_ref, pools, bufs, sems, dst):
    b = pl.program_id(0)
    slot = b % 2

    @pl.when(b == 0)
    def _():
        for cp in _page_copies(pt_ref, pools, bufs, sems, 0, 0, dst):
            cp.start()

    @pl.when(b + 1 < pl.num_programs(0))
    def _():
        for cp in _page_copies(pt_ref, pools, bufs, sems, b + 1, 1 - slot, dst):
            cp.start()

    for cp in _page_copies(pt_ref, pools, bufs, sems, b, slot, dst):
        cp.wait()
    return slot


def _compress_pages(buf, slot, perm_ref, x_ref, wexp_ref, w2_ref, pos_ref, sh_ref):
    n_pages = buf.shape[1]
    kvw = buf.shape[2]
    rpp = PAGE_SIZE // ROW_TOKENS
    nr = n_pages * rpp
    half = wexp_ref.shape[1] // 2

    for i in range(n_pages // 2):
        pair = buf[slot, 2 * i:2 * i + 2].reshape(2 * kvw, PAGE_SIZE).astype(BF16)
        xt = _dot_nt(perm_ref[...], pair)
        for pp in range(2):
            for l in range(ROW_TOKENS):
                row0 = (l // 2) * nr + (2 * i + pp) * rpp
                x_ref[row0:row0 + rpp, (l % 2) * kvw:(l % 2 + 1) * kvw] = (
                    xt[l * rpp:(l + 1) * rpp, pp * kvw:(pp + 1) * kvw])
    p = None
    for i in range(ROW_TOKENS // 2):
        part = jnp.dot(x_ref[i * nr:(i + 1) * nr, :].astype(BF16), wexp_ref[2 * i * kvw:2 * (i + 1) * kvw, :],
                       preferred_element_type=F32)
        p = part if p is None else p + part
    pp = jnp.dot(pos_ref[...], wexp_ref[...], preferred_element_type=F32)
    posb = pp[0:1, :half] + pp[1:2, half:]
    sh_ref[0:nr, :] = p[:, half:]
    sh_ref[nr:nr + 8, :] = jnp.zeros((8, half), F32)
    h = p[:, :half] + sh_ref[1:nr + 1, :] + posb
    return jnp.dot(jax.nn.gelu(h).astype(BF16), w2_ref[...], preferred_element_type=F32).astype(BF16)


def _sample_cmp_body(pt_ref, kpool_ref, vpool_ref, qbd_ref, tab_ref, perm_ref, wk_ref, w2k_ref, pk_ref, wv_ref, w2v_ref,
                     pv_ref, cov_ref, oc_ref, imp_ref, kbuf, vbuf, sems, x_ref, sh_ref, *, rep, t_new):
    slot = _gather_step(pt_ref, (kpool_ref, vpool_ref), (kbuf, vbuf), sems, lambda buf, sl, p: buf.at[sl, p])
    kc = _compress_pages(kbuf, slot, perm_ref, x_ref, wk_ref, w2k_ref, pk_ref, sh_ref)
    vc = _compress_pages(vbuf, slot, perm_ref, x_ref, wv_ref, w2v_ref, pv_ref, sh_ref)
    s = _dot_nt(qbd_ref[0], kc) + tab_ref[...]
    m = jnp.max(s, axis=1, keepdims=True)
    p = jnp.where(s > 0.1 * NEG, jnp.exp2(s - m), 0.0)
    l = jnp.sum(p, axis=1, keepdims=True)
    pn = (p * jnp.where(l > 0.0, 1.0 / l, 0.0)).astype(BF16)
    oc_ref[0] = jnp.dot(pn, vc, preferred_element_type=F32)
    imp = jnp.dot(pn, cov_ref[...], preferred_element_type=F32)
    per_g = rep * t_new
    parts = []
    for g in range(N_KV):
        acc = imp[g * per_g:g * per_g + t_new]
        for r in range(1, rep):
            acc = acc + imp[g * per_g + r * t_new:g * per_g + (r + 1) * t_new]
        parts.append(acc)
    imp_ref[0] = jnp.concatenate(parts, axis=0)


def _sample_cmp(page_table, kpool_t, vpool_t, qbd, tab, cmp_k, cmp_v, cov, *, rep, t_new):
    nb, n_pages = page_table.shape
    rpp = PAGE_SIZE // ROW_TOKENS
    n_rows = n_pages * rpp
    n_q = qbd.shape[1]
    kvw = N_KV * HEAD_DIM
    tok = np.arange(PAGE_SIZE)
    perm = jnp.asarray((tok[None, :] == (tok[:, None] % rpp) * ROW_TOKENS + tok[:, None] // rpp).astype(np.float32), BF16)
    const = lambda a: pl.BlockSpec(a.shape, lambda b, pt: (0,) * a.ndim, pipeline_mode=pl.Buffered(1))
    grid_spec = pltpu.PrefetchScalarGridSpec(
        num_scalar_prefetch=1,
        grid=(nb,),
        in_specs=[pl.BlockSpec(memory_space=pl.ANY), pl.BlockSpec(memory_space=pl.ANY),
                  pl.BlockSpec((1, n_q, kvw), lambda b, pt: (b, 0, 0)), const(tab), const(perm),
                  const(cmp_k[0]), const(cmp_k[1]), const(cmp_k[2]), const(cmp_v[0]), const(cmp_v[1]), const(cmp_v[2]),
                  const(cov)],
        out_specs=[pl.BlockSpec((1, n_q, kvw), lambda b, pt: (b, 0, 0)),
                   pl.BlockSpec((1, N_KV * t_new, cov.shape[1]), lambda b, pt: (b, 0, 0))],
        scratch_shapes=[pltpu.VMEM((2, n_pages, kvw, PAGE_SIZE), F32), pltpu.VMEM((2, n_pages, kvw, PAGE_SIZE), F32),
                        pltpu.SemaphoreType.DMA((2, 2)), pltpu.VMEM((n_pages * PAGE_SIZE // 2, 2 * kvw), F32),
                        pltpu.VMEM((n_rows + 8, kvw), F32)],
    )
    return pl.pallas_call(
        functools.partial(_sample_cmp_body, rep=rep, t_new=t_new),
        grid_spec=grid_spec,
        out_shape=[jax.ShapeDtypeStruct((nb, n_q, kvw), F32),
                   jax.ShapeDtypeStruct((nb, N_KV * t_new, cov.shape[1]), F32)],
        compiler_params=_cparams(1),
        name="sample_cmp",
    )(page_table, kpool_t, vpool_t, qbd, tab, perm, *cmp_k, *cmp_v, cov)


def _select_body(imp_ref, o_ref, *, top_n, n_blk, pos0, t_new):
    rows, cols = imp_ref.shape
    j = lax.broadcasted_iota(jnp.int32, (rows, cols), 0)
    col = pl.program_id(0) * cols + lax.broadcasted_iota(jnp.int32, (rows, cols), 1)
    cur = (pos0 + col % t_new) // L_SEL
    forced = (j == 0) | (j == cur) | (j == cur - 1)
    v = jnp.where(forced, 1e9, jnp.where(j <= cur, imp_ref[...], -1e9))
    v = jnp.where(j < n_blk, v, -3e38)
    rank = _descending_rank(v, n_blk)
    o_ref[...] = jnp.where((rank < top_n) & (j <= cur) & (j < n_blk), 0.0, NEG)


def _select_blocks(imp_t, *, top_n, n_blk, pos0, t_new):
    rows, cols = imp_t.shape
    tile = 4 * LANE
    return pl.pallas_call(
        functools.partial(_select_body, top_n=top_n, n_blk=n_blk, pos0=pos0, t_new=t_new),
        grid=(cols // tile,),
        in_specs=[pl.BlockSpec((rows, tile), lambda i: (0, i))],
        out_specs=pl.BlockSpec((rows, tile), lambda i: (0, i)),
        out_shape=jax.ShapeDtypeStruct((rows, cols), F32),
        compiler_params=_cparams(1),
        name="sample_select",
    )(imp_t)


def _softmax_pv(parts):
    m = None
    for s, _, _ in parts:
        mk = jnp.max(s, axis=1, keepdims=True)
        m = mk if m is None else jnp.maximum(m, mk)
    l = None
    o = None
    for s, v, channel_major in parts:
        p = jnp.exp2(s - m)
        lk = jnp.sum(p, axis=1, keepdims=True)
        pb = p.astype(BF16)
        ok = _dot_nt(pb, v) if channel_major else jnp.dot(pb, v, preferred_element_type=F32)
        l = lk if l is None else l + lk
        o = ok if o is None else o + ok
    return o / l


def _sample_slc_body(pt_ref, kpool_ref, vpool_ref, qbd_ref, selb_ref, gate_ref, oc_ref, ksn_ref, vsn_ref, kwn_ref,
                     vwn_ref, kwin_ref, vwin_ref, near_ref, newtab_ref, wtab_ref, expand_ref,
                     o_ref, kbuf, vbuf, sems, pad_ref, *, rep, t_new):
    b = pl.program_id(0)
    slot = _gather_step(pt_ref, (kpool_ref, vpool_ref), (kbuf, vbuf), sems,
                        lambda buf, sl, p: buf.at[sl, :, pl.ds(p * PAGE_SIZE, PAGE_SIZE)])
    hd = HEAD_DIM
    kvw = N_KV * hd

    @pl.when(b == 0)
    def _():
        pad_ref[...] = jnp.zeros(pad_ref.shape, F32)

    news = []
    for i, ref in enumerate((ksn_ref, vsn_ref, kwn_ref, vwn_ref)):
        pad_ref[i, 0:t_new, :] = ref[0]
        news.append(pad_ref[i].astype(BF16))
    ks_new, vs_new, kw_new, vw_new = news
    qbd = qbd_ref[0]

    per_g = rep * t_new
    sel = selb_ref[0]
    sel_rows = jnp.concatenate([sel[g * t_new:(g + 1) * t_new] for g in range(N_KV) for _ in range(rep)], axis=0)
    s_past = (jnp.dot(qbd, kbuf[slot].astype(BF16), preferred_element_type=F32)
              + jnp.dot(sel_rows.astype(BF16), expand_ref[...], preferred_element_type=F32))
    n_past = s_past.shape[1]
    n_near = near_ref.shape[1]
    s_past = jnp.concatenate([s_past[:, :n_past - n_near], s_past[:, n_past - n_near:] + near_ref[...]], axis=1)
    s_new = _dot_nt(qbd, ks_new) + newtab_ref[0]
    o_s = _softmax_pv([(s_past, vbuf[slot].astype(BF16), True), (s_new, vs_new, False)])

    s_win = jnp.dot(qbd, kwin_ref[0].astype(BF16), preferred_element_type=F32) + wtab_ref[...]
    s_wnew = _dot_nt(qbd, kw_new) + newtab_ref[1]
    o_w = _softmax_pv([(s_win, vwin_ref[0].astype(BF16), True), (s_wnew, vw_new, False)])

    gate = gate_ref[0]
    comb = gate[:, 0:1] * oc_ref[0] + gate[:, 1:2] * o_s + gate[:, 2:3] * o_w
    row = lax.broadcasted_iota(jnp.int32, comb.shape, 0)
    o_ref[0] = jnp.where(row < per_g, comb, pltpu.roll(comb, hd, axis=1))


def _sample_slc(page_table, kpool, vpool, qbd, selb, gate_rows, oc, ks_new, vs_new, kw_new, vw_new, kwin, vwin,
                near, newtab, wtab, expand, *, rep, t_new):
    nb, n_pages = page_table.shape
    n_keys = n_pages * PAGE_SIZE
    n_q = qbd.shape[1]
    kvw = N_KV * HEAD_DIM
    const = lambda a: pl.BlockSpec(a.shape, lambda b, pt: (0,) * a.ndim, pipeline_mode=pl.Buffered(1))
    per_b = lambda a: pl.BlockSpec((1,) + a.shape[1:], lambda b, pt: (b,) + (0,) * (a.ndim - 1))
    grid_spec = pltpu.PrefetchScalarGridSpec(
        num_scalar_prefetch=1,
        grid=(nb,),
        in_specs=[pl.BlockSpec(memory_space=pl.ANY), pl.BlockSpec(memory_space=pl.ANY),
                  per_b(qbd), per_b(selb), per_b(gate_rows), per_b(oc),
                  per_b(ks_new), per_b(vs_new), per_b(kw_new), per_b(vw_new), per_b(kwin), per_b(vwin),
                  const(near), const(newtab), const(wtab), const(expand)],
        out_specs=pl.BlockSpec((1, n_q, kvw), lambda b, pt: (b, 0, 0)),
        scratch_shapes=[pltpu.VMEM((2, kvw, n_keys), F32), pltpu.VMEM((2, kvw, n_keys), F32),
                        pltpu.SemaphoreType.DMA((2, 2)), pltpu.VMEM((4, LANE, kvw), F32)],
    )
    return pl.pallas_call(
        functools.partial(_sample_slc_body, rep=rep, t_new=t_new),
        grid_spec=grid_spec,
        out_shape=jax.ShapeDtypeStruct((nb, n_q, kvw), F32),
        compiler_params=_cparams(1),
        name="sample_slc_win",
    )(page_table, kpool, vpool, qbd, selb, gate_rows, oc, ks_new, vs_new, kw_new, vw_new, kwin, vwin,
      near, newtab, wtab, expand)


def _sample_tables(rel_bias, pos0, t_new, n_rows, n_c, wbuf):
    h = rel_bias.shape[0]
    qpos = pos0 + jnp.arange(t_new, dtype=jnp.int32)

    def table(kpos, ok_extra=None, upper=None):
        d = qpos[:, None] - kpos[None, :]
        ok = d >= 0
        if upper is not None:
            ok = ok & (d < upper)
        if ok_extra is not None:
            ok = ok & ok_extra[None, :]
        tab = jnp.where(ok[None], _bias_minus_far(rel_bias, d, 0), NEG)
        return tab.reshape(h * t_new, kpos.shape[0]).astype(F32)

    c = jnp.arange(n_rows, dtype=jnp.int32)
    cmp = table(c * STRIDE + (L_CMP - 1), ok_extra=c < n_c)
    near = table(pos0 - PAGE_SIZE + jnp.arange(PAGE_SIZE, dtype=jnp.int32))
    lane = jnp.arange(LANE, dtype=jnp.int32)
    new_slc = table(pos0 + lane, ok_extra=lane < t_new)
    new_win = table(pos0 + lane, ok_extra=lane < t_new, upper=WINDOW)
    kpos_w = pos0 - wbuf + jnp.arange(wbuf, dtype=jnp.int32)
    win = table(kpos_w, ok_extra=kpos_w >= 0, upper=WINDOW)
    return cmp, near, jnp.stack([new_slc, new_win]), win


def _layer_weights(p, n_heads):
    d = p["w_in"].shape[0]
    kv_end = d + 6 * N_KV * HEAD_DIM
    src = _gate_column_source(n_heads)
    gate_cols = jnp.where(jnp.asarray(src >= 0)[None, :], p["w_in"][:, kv_end + np.maximum(src, 0)], 0.0)
    ln = jnp.stack([p["ln_gain"][0], p["ln_bias"][0], p["ln_gain"][1], p["ln_bias"][1],
                    p["ln_gain"][2], p["ln_bias"][2]])
    return dict(
        wi1=p["w_ffn1_in"].astype(BF16), wo1=p["w_ffn1_out"].astype(BF16),
        wi2=p["w_ffn2_in"].astype(BF16), wo2=p["w_ffn2_out"].astype(BF16),
        wp=jnp.concatenate([p["w_in"][:, :kv_end], gate_cols], axis=1).astype(BF16),
        wout=p["w_out"].astype(BF16),
        wpool=p["w_pool"].astype(BF16), pool_scale=p["pool_scale"],
        ln1=ln[0:2], ln23=ln[2:6],
        cmp_k=_compress_weights(p["w_phi_k1"], p["w_phi_k2"], p["cmp_pos"]),
        cmp_v=_compress_weights(p["w_phi_v1"], p["w_phi_v2"], p["cmp_pos"]),
    )


PROMPT_TOKENS = 512


def _prompt_layer(x, mod, w, rel_bias, alpha):
    nb, t, d = x.shape
    assert t % TQ == 0 and t % PROMPT_TOKENS == 0
    (x1, u_pool, q, k_c, v_c, k_s, v_s, k_w, v_w, ks_bf, vs_t, kw_bf, vw_t, gates_t) = _ffn_in(
        x, mod, w["wi1"], w["wo1"], w["wp"], w["ln1"], alpha=alpha, nbk=1, tt=PROMPT_TOKENS, for_prompt=True)
    pool_out = _pool_mix(u_pool, None, w["wpool"], w["pool_scale"], pos0=0, nbk=1, tt=PROMPT_TOKENS,
                         name="pool_prompt")
    row_w = ROW_TOKENS * N_KV * HEAD_DIM
    kc_cmp = _compress(k_c.reshape(nb, t // ROW_TOKENS, row_w), *w["cmp_k"], transposed=False, name="compress_k_prompt")
    vc_cmp_t = _compress(v_c.reshape(nb, t // ROW_TOKENS, row_w), *w["cmp_v"], transposed=True, name="compress_v_prompt")
    n_c = (t - L_CMP) // STRIDE + 1
    n_blk = t // L_SEL
    near, tcmp, e0 = _prompt_tables(rel_bias, t)
    nsa = _nsa_prompt(q, gates_t, kc_cmp, vc_cmp_t, ks_bf, vs_t, kw_bf, vw_t, tcmp, near,
                      _cover_t(t // ROW_TOKENS, n_c, n_blk), e0=e0, top_n=min(TOP_N, n_blk))
    y = _ffn_out(pool_out, nsa, x1, mod, w["wout"], w["wi2"], w["wo2"], w["ln23"], alpha=alpha,
                 nbk=1, tt=PROMPT_TOKENS, name="outproj_ffn2_prompt")
    keep = min(WINDOW, t)
    heads = lambda a: a.reshape(nb, -1, N_KV, HEAD_DIM)
    states = (heads(k_c), heads(v_c), heads(k_s), heads(v_s), heads(k_w[:, t - keep:]), heads(v_w[:, t - keep:]),
              u_pool[:, t - POOL_STATE:])
    return y, states


SAMPLE_BATCHES = 32


def _sample_layer(x, mod, w, rel_bias, alpha, page_table, pools, kwin, vwin, pool_state):
    nb, t, d = x.shape
    hd = HEAD_DIM
    n_heads = rel_bias.shape[0]
    rep = n_heads // N_KV
    kvw = N_KV * hd
    pos0 = page_table.shape[1] * PAGE_SIZE
    n_c = (pos0 + t - L_CMP) // STRIDE + 1
    assert (n_c - 1) * STRIDE + L_CMP <= pos0 and pos0 % PAGE_SIZE == 0 and nb % SAMPLE_BATCHES == 0
    n_rows = pos0 // ROW_TOKENS
    n_blk = -(-(pos0 + t) // L_SEL)
    assert n_blk - 1 == pos0 // L_SEL
    blk_pad = -(-n_blk // LANE) * LANE
    wbuf = kwin.shape[1]

    x1, u_pool, q, k_c, v_c, k_s, v_s, k_w, v_w, gates = _ffn_in(
        x, mod, w["wi1"], w["wo1"], w["wp"], w["ln1"], alpha=alpha, nbk=SAMPLE_BATCHES, tt=t, for_prompt=False)
    pw = u_pool.shape[-1]
    prev = jnp.concatenate([jnp.zeros((nb, PREV_ROWS - POOL_STATE, pw), F32), pool_state], axis=1)
    pool_out = _pool_mix(u_pool, prev, w["wpool"], w["pool_scale"], pos0=pos0, nbk=SAMPLE_BATCHES, tt=t,
                         name="pool_sample")

    qh = q.reshape(nb, t, N_KV, rep, hd).transpose(0, 2, 3, 1, 4)
    own = jnp.eye(N_KV, dtype=q.dtype)[None, :, None, None, :, None]
    qbd = (qh[:, :, :, :, None, :] * own).reshape(nb, N_KV * rep * t, kvw)
    gate_rows = gates[:, :, :N_KV * GATE_ROWS].reshape(nb, t, N_KV, GATE_ROWS)[..., :3 * rep]
    gate_rows = gate_rows.reshape(nb, t, N_KV, 3, rep).transpose(0, 2, 4, 1, 3).reshape(nb, N_KV * rep * t, 3)
    gate_rows = jnp.pad(gate_rows, ((0, 0), (0, 0), (0, LANE - 3)))

    tab_c, near, newtab, wtab = _sample_tables(rel_bias, pos0, t, n_rows, n_c, wbuf)
    cov = np.zeros((n_rows, blk_pad), np.float32)
    ci = np.arange(n_rows)[:, None]
    bj = np.arange(blk_pad)[None, :]
    cov[:] = ((ci * STRIDE <= (bj + 1) * L_SEL - 1) & (ci * STRIDE + L_CMP - 1 >= bj * L_SEL)
              & (ci < n_c) & (bj < n_blk))
    channel_major = lambda a: a.transpose(0, 2, 3, 1).reshape(a.shape[0], kvw, a.shape[1])
    oc, imp = _sample_cmp(page_table, channel_major(pools[0]), channel_major(pools[1]), qbd, tab_c, w["cmp_k"],
                          w["cmp_v"], jnp.asarray(cov, BF16), rep=rep, t_new=t)

    sel_rows = -(-n_blk // 8) * 8
    imp_t = imp.transpose(2, 0, 1).reshape(blk_pad, nb * N_KV * t)[:sel_rows]
    selb_t = _select_blocks(imp_t, top_n=min(TOP_N, n_blk), n_blk=n_blk, pos0=pos0, t_new=t)
    selb_t = jnp.pad(selb_t, ((0, blk_pad - sel_rows), (0, 0)), constant_values=NEG)
    selb = selb_t.reshape(blk_pad, nb, N_KV * t).transpose(1, 2, 0)

    key_blk = lax.broadcasted_iota(jnp.int32, (blk_pad, pos0), 1) // L_SEL
    expand = (key_blk == lax.broadcasted_iota(jnp.int32, (blk_pad, pos0), 0)).astype(BF16)
    kvf = lambda a: a.reshape(a.shape[0], a.shape[1], kvw)
    o = _sample_slc(page_table, channel_major(pools[2]), channel_major(pools[3]), qbd, selb, gate_rows, oc,
                    k_s, v_s, k_w, v_w, channel_major(kwin), channel_major(vwin), near, newtab, wtab, expand,
                    rep=rep, t_new=t)
    nsa = o[:, :, :hd].reshape(nb, N_KV, rep, t, hd).transpose(0, 3, 1, 2, 4).reshape(nb, t, n_heads * hd).astype(BF16)

    y = _ffn_out(pool_out, nsa, x1, mod, w["wout"], w["wi2"], w["wo2"], w["ln23"], alpha=alpha,
                 nbk=SAMPLE_BATCHES, tt=t, name="outproj_ffn2_sample")
    keep = min(WINDOW, pos0 + t)
    heads = lambda a: a.reshape(nb, -1, N_KV, HEAD_DIM)
    kw_ext = jnp.concatenate([kvf(kwin), k_w], axis=1)
    vw_ext = jnp.concatenate([kvf(vwin), v_w], axis=1)
    pool_ext = jnp.concatenate([pool_state, u_pool], axis=1)
    states = (heads(k_c), heads(v_c), heads(k_s), heads(v_s), heads(kw_ext[:, kw_ext.shape[1] - keep:]),
              heads(vw_ext[:, vw_ext.shape[1] - keep:]), pool_ext[:, pool_ext.shape[1] - POOL_STATE:])
    return y, states


def kernel(x_prompt, x_sample, c_prompt, c_sample, cache_k_cmp, cache_v_cmp, cache_k_slc, cache_v_slc, page_table,
           state_k_win, state_v_win, state_pool, w_ada, b_ada, ln_gain, ln_bias, w_ffn1_in, w_ffn1_out, w_ffn2_in,
           w_ffn2_out, w_in, w_phi_k1, w_phi_k2, w_phi_v1, w_phi_v2, cmp_pos, w_pool, pool_scale, w_out, rel_bias):
    depth = w_ada.shape[0]
    alpha = (2.0 * depth) ** 0.25
    n_heads = rel_bias.shape[0]
    nb_p, nb_s = x_prompt.shape[0], x_sample.shape[0]
    d = x_prompt.shape[-1]
    xp, xs = x_prompt, x_sample
    c_all = jnp.concatenate([c_prompt, c_sample], axis=0)
    p_states, s_states = [], []
    for l in range(depth):
        p = dict(w_ada=w_ada[l], b_ada=b_ada[l], ln_gain=ln_gain[l], ln_bias=ln_bias[l], w_ffn1_in=w_ffn1_in[l],
                 w_ffn1_out=w_ffn1_out[l], w_ffn2_in=w_ffn2_in[l], w_ffn2_out=w_ffn2_out[l], w_in=w_in[l],
                 w_phi_k1=w_phi_k1[l], w_phi_k2=w_phi_k2[l], w_phi_v1=w_phi_v1[l], w_phi_v2=w_phi_v2[l],
                 cmp_pos=cmp_pos[l], w_pool=w_pool[l], pool_scale=pool_scale[l], w_out=w_out[l])
        w = _layer_weights(p, n_heads)
        mod = _ada(c_all, p["w_ada"], p["b_ada"]).reshape(nb_p + nb_s, 9, d)
        xp, sp = _prompt_layer(xp, mod[:nb_p], w, rel_bias, alpha)
        xs, ss = _sample_layer(xs, mod[nb_p:], w, rel_bias, alpha, page_table,
                               (cache_k_cmp[l], cache_v_cmp[l], cache_k_slc[l], cache_v_slc[l]),
                               state_k_win[l], state_v_win[l], state_pool[l])
        p_states.append(sp)
        s_states.append(ss)
    stack = lambda states: tuple(jnp.stack(a) for a in zip(*states))
    return (xp, xs) + stack(p_states) + stack(s_states)
```

```python
import functools
import math

import numpy as np
import jax
import jax.numpy as jnp
from jax import lax
from jax.experimental import pallas as pl
from jax.experimental.pallas import tpu as pltpu

F32 = jnp.float32
BF16 = jnp.bfloat16

HEAD_DIM = 64
N_KV = 2
L_CMP = 32
STRIDE = 16
L_SEL = 64
TOP_N = 16
WINDOW = 512
N_BUCKETS = 32
MAX_DIST = 128
POOL_WINDOWS = (2, 4, 8, 16)
POOL_STATE = max(POOL_WINDOWS) - 1
LN_EPS = 1e-5
NEG = -1e30
PAGE_SIZE = 128

Q_SCALE = HEAD_DIM ** -0.5 * math.log2(math.e)
TQ = 128
FAR_CHUNKS = 4
FAR_BLOCKS = FAR_CHUNKS * TQ // L_SEL
VT_ROWS = HEAD_DIM + 16

LANE = 128
MXU_COLS = 256
FF_CHUNK = MXU_COLS
VMEM_LIMIT = 56 * 1024 * 1024


def _cparams(n_axes):
    return pltpu.CompilerParams(dimension_semantics=("arbitrary",) * n_axes,
                                vmem_limit_bytes=VMEM_LIMIT)


def _const_spec(shape):
    nd = len(shape)
    return pl.BlockSpec(shape, lambda *_: (0,) * nd, pipeline_mode=pl.Buffered(1))


def _layer_norm(y, gain, bias):
    mu = jnp.mean(y, axis=-1, keepdims=True)
    yc = y - mu
    var = jnp.mean(yc * yc, axis=-1, keepdims=True)
    return yc * lax.rsqrt(var + LN_EPS) * gain + bias


def _swiglu(u_bf, wi_ref, wo_ref):
    d_ff = wo_ref.shape[0]
    acc = None
    for c in range(d_ff // FF_CHUNK):
        lo = c * FF_CHUNK
        a = jnp.dot(u_bf, wi_ref[:, lo:lo + FF_CHUNK], preferred_element_type=F32)
        b = jnp.dot(u_bf, wi_ref[:, d_ff + lo:d_ff + lo + FF_CHUNK], preferred_element_type=F32)
        hid = (a * jax.nn.sigmoid(a) * b).astype(BF16)
        part = jnp.dot(hid, wo_ref[lo:lo + FF_CHUNK, :], preferred_element_type=F32)
        acc = part if acc is None else acc + part
    return acc


def _ada_body(c_ref, w_ref, b_ref, o_ref):
    c = c_ref[...]
    act = (c * jax.nn.sigmoid(c)).astype(BF16)
    o_ref[...] = jnp.dot(act, w_ref[...].astype(BF16), preferred_element_type=F32) + b_ref[...]


def _ada(c_all, w_ada, b_ada):
    nb, d = c_all.shape
    n_out = w_ada.shape[1]
    cols = 9 * LANE
    return pl.pallas_call(
        _ada_body,
        grid=(n_out // cols,),
        in_specs=[pl.BlockSpec((nb, d), lambda j: (0, 0)),
                  pl.BlockSpec((d, cols), lambda j: (0, j)),
                  pl.BlockSpec((1, cols), lambda j: (0, j))],
        out_specs=pl.BlockSpec((nb, cols), lambda j: (0, j)),
        out_shape=jax.ShapeDtypeStruct((nb, n_out), F32),
        compiler_params=_cparams(1),
        name="ada",
    )(c_all, w_ada, b_ada.reshape(1, n_out))


KV_NAMES = ("k_c", "v_c", "k_s", "v_s", "k_w", "v_w")


def _ffn_in_body(x_ref, mod_ref, wi_ref, wo_ref, wp_ref, ln_ref, *out_refs, alpha, pool_w, nsa_w, kv_w, for_prompt):
    nbk, tt, d = x_ref.shape
    n = nbk * tt
    x = x_ref[...]
    mod = mod_ref[...]
    u = (x * (1.0 + mod[:, 1:2, :]) + mod[:, 0:1, :]).reshape(n, d).astype(BF16)
    h = _swiglu(u, wi_ref, wo_ref).reshape(nbk, tt, d)
    x1 = _layer_norm(alpha * x + 0.5 * mod[:, 2:3, :] * h, ln_ref[0:1, :], ln_ref[1:2, :])
    u1 = (x1 * (1.0 + mod[:, 4:5, :]) + mod[:, 3:4, :]).reshape(n, d).astype(BF16)
    proj = jnp.dot(u1, wp_ref[...], preferred_element_type=F32)

    x1_ref, upool_ref, q_ref = out_refs[:3]
    kv_refs = out_refs[3:9]
    x1_ref[...] = x1
    upool_ref[...] = proj[:, :pool_w].reshape(nbk, tt, pool_w)
    q_ref[...] = (proj[:, pool_w:pool_w + nsa_w] * Q_SCALE).astype(BF16).reshape(nbk, tt, nsa_w)
    off = pool_w + nsa_w
    kv = []
    for i in range(6):
        blk = proj[:, off + i * kv_w:off + (i + 1) * kv_w]
        kv.append(blk)
        kv_refs[i][...] = blk.reshape(nbk, tt, kv_w)
    gates = jax.nn.sigmoid(proj[:, off + 6 * kv_w:off + 6 * kv_w + LANE])
    if for_prompt:
        ksb_ref, vst_ref, kwb_ref, vwt_ref, gt_ref = out_refs[9:]
        row = lax.broadcasted_iota(jnp.int32, (n, kv_w), 0)
        lane = lax.broadcasted_iota(jnp.int32, (n, kv_w), 1)
        onehot = jnp.where((row // L_SEL) % FAR_BLOCKS == lane, 1.0, 0.0)
        ones_row = jnp.where(lax.broadcasted_iota(jnp.int32, (VT_ROWS - HEAD_DIM, n), 0) == 0, 1.0, 0.0)

        def values_t(x):
            xt = x.T
            return jnp.concatenate([piece for g in range(N_KV)
                                    for piece in (xt[g * HEAD_DIM:(g + 1) * HEAD_DIM], ones_row)], axis=0).astype(BF16)

        ksb_ref[0] = jnp.concatenate([kv[2], onehot], axis=1).astype(BF16)
        vst_ref[0] = values_t(kv[3])
        kwb_ref[0] = kv[4].astype(BF16)
        vwt_ref[0] = values_t(kv[5])
        gt_ref[0] = gates.T[:gt_ref.shape[1], :]
    else:
        out_refs[9][...] = gates.reshape(nbk, tt, LANE)


def _ffn_in(x, mod, wi, wo, wp, ln, *, alpha, nbk, tt, for_prompt):
    nb, t, d = x.shape
    pool_w = d // 2
    nsa_w = d - pool_w
    kv_w = N_KV * HEAD_DIM
    grid = (nb // nbk, t // tt)
    tok = lambda w: pl.BlockSpec((nbk, tt, w), lambda i, j: (i, j, 0))
    out_specs = [tok(d), tok(pool_w), tok(nsa_w)] + [tok(kv_w)] * 6
    out_shape = ([jax.ShapeDtypeStruct((nb, t, d), F32), jax.ShapeDtypeStruct((nb, t, pool_w), F32),
                  jax.ShapeDtypeStruct((nb, t, nsa_w), BF16)]
                 + [jax.ShapeDtypeStruct((nb, t, kv_w), F32)] * 6)
    if for_prompt:
        assert nbk == 1 and tt % (FAR_BLOCKS * L_SEL) == 0
        tr = lambda rows: pl.BlockSpec((1, rows, tt), lambda i, j: (i, 0, j))
        vt_rows = N_KV * VT_ROWS
        out_specs += [tok(2 * kv_w), tr(vt_rows), tok(kv_w), tr(vt_rows), tr(N_KV * GATE_ROWS)]
        out_shape += [jax.ShapeDtypeStruct((nb, t, 2 * kv_w), BF16), jax.ShapeDtypeStruct((nb, vt_rows, t), BF16),
                      jax.ShapeDtypeStruct((nb, t, kv_w), BF16), jax.ShapeDtypeStruct((nb, vt_rows, t), BF16),
                      jax.ShapeDtypeStruct((nb, N_KV * GATE_ROWS, t), F32)]
    else:
        out_specs += [tok(LANE)]
        out_shape += [jax.ShapeDtypeStruct((nb, t, LANE), F32)]
    body = functools.partial(_ffn_in_body, alpha=alpha, pool_w=pool_w, nsa_w=nsa_w, kv_w=kv_w,
                             for_prompt=for_prompt)
    return pl.pallas_call(
        body,
        grid=grid,
        in_specs=[tok(d),
                  pl.BlockSpec((nbk, mod.shape[1], d), lambda i, j: (i, 0, 0)),
                  _const_spec(wi.shape), _const_spec(wo.shape), _const_spec(wp.shape), _const_spec(ln.shape)],
        out_specs=out_specs,
        out_shape=out_shape,
        compiler_params=_cparams(2),
        name="ffn1_inproj_prompt" if for_prompt else "ffn1_inproj_sample",
    )(x, mod, wi, wo, wp, ln)


GATE_ROWS = 16


def _gate_column_source(n_heads):
    rep = n_heads // N_KV
    src = np.full((LANE,), -1, np.int32)
    for g in range(N_KV):
        for br in range(3):
            for r in range(rep):
                src[g * GATE_ROWS + br * rep + r] = br * n_heads + g * rep + r
    return src


def _ffn_out_body(pool_ref, nsa_ref, x1_ref, mod_ref, wout_ref, wi_ref, wo_ref, ln_ref, o_ref, *, alpha):
    nbk, tt, d = x1_ref.shape
    n = nbk * tt
    pw = pool_ref.shape[-1]
    mod = mod_ref[...]
    x1 = x1_ref[...]
    mix = (jnp.dot(pool_ref[...].reshape(n, pw), wout_ref[:pw, :], preferred_element_type=F32)
           + jnp.dot(nsa_ref[...].reshape(n, d - pw), wout_ref[pw:, :], preferred_element_type=F32))
    x2 = _layer_norm(alpha * x1 + mod[:, 5:6, :] * mix.reshape(nbk, tt, d), ln_ref[0:1, :], ln_ref[1:2, :])
    u = (x2 * (1.0 + mod[:, 7:8, :]) + mod[:, 6:7, :]).reshape(n, d).astype(BF16)
    h = _swiglu(u, wi_ref, wo_ref).reshape(nbk, tt, d)
    o_ref[...] = _layer_norm(alpha * x2 + 0.5 * mod[:, 8:9, :] * h, ln_ref[2:3, :], ln_ref[3:4, :])


def _ffn_out(pool_out, nsa_out, x1, mod, wout, wi, wo, ln, *, alpha, nbk, tt, name):
    nb, t, d = x1.shape
    tok = lambda w: pl.BlockSpec((nbk, tt, w), lambda i, j: (i, j, 0))
    return pl.pallas_call(
        functools.partial(_ffn_out_body, alpha=alpha),
        grid=(nb // nbk, t // tt),
        in_specs=[tok(pool_out.shape[-1]), tok(nsa_out.shape[-1]), tok(d),
                  pl.BlockSpec((nbk, mod.shape[1], d), lambda i, j: (i, 0, 0)),
                  _const_spec(wout.shape), _const_spec(wi.shape), _const_spec(wo.shape), _const_spec(ln.shape)],
        out_specs=tok(d),
        out_shape=jax.ShapeDtypeStruct((nb, t, d), F32),
        compiler_params=_cparams(2),
        name=name,
    )(pool_out, nsa_out, x1, mod, wout, wi, wo, ln)


PREV_ROWS = 16


def _pool_body(u_ref, prev_ref, w_ref, scale_ref, o_ref, ext_ref, *, pos0, zero_first):
    nbk, tt, width = u_ref.shape
    gw = width // len(POOL_WINDOWS)
    j = pl.program_id(1)
    u = u_ref[...]
    prev = prev_ref[...]
    if zero_first:
        prev = jnp.where(j == 0, 0.0, prev)
    ext_ref[:, 0:PREV_ROWS, :] = prev
    ext_ref[:, PREV_ROWS:, :] = u
    pos = pos0 + j * tt + lax.broadcasted_iota(jnp.int32, (1, tt, 1), 1)
    for g, w in enumerate(POOL_WINDOWS):
        lo = g * gw
        tot = None
        for k in range(w):
            part = ext_ref[:, PREV_ROWS - k:PREV_ROWS - k + tt, lo:lo + gw]
            tot = part if tot is None else tot + part
        cnt = jnp.minimum(pos + 1, w).astype(F32)
        pooled = (tot / cnt - u[:, :, lo:lo + gw]).astype(BF16).reshape(nbk * tt, gw)
        mixed = jnp.dot(pooled, w_ref[g], preferred_element_type=F32) * scale_ref[:, lo:lo + gw]
        o_ref[:, :, lo:lo + gw] = mixed.astype(BF16).reshape(nbk, tt, gw)


def _pool_mix(u_pool, prev, w_pool_bf, pool_scale, *, pos0, nbk, tt, name):
    nb, t, width = u_pool.shape
    tok = pl.BlockSpec((nbk, tt, width), lambda i, j: (i, j, 0))
    if prev is None:
        assert nbk == 1 and tt % PREV_ROWS == 0 and pos0 == 0
        step = tt // PREV_ROWS
        prev_arr = u_pool
        prev_spec = pl.BlockSpec((1, PREV_ROWS, width), lambda i, j: (i, jnp.maximum(j * step - 1, 0), 0))
    else:
        assert t == tt
        prev_arr = prev
        prev_spec = pl.BlockSpec((nbk, PREV_ROWS, width), lambda i, j: (i, 0, 0))
    return pl.pallas_call(
        functools.partial(_pool_body, pos0=pos0, zero_first=prev is None),
        grid=(nb // nbk, t // tt),
        in_specs=[tok, prev_spec, _const_spec(w_pool_bf.shape), _const_spec((1, width))],
        out_specs=tok,
        out_shape=jax.ShapeDtypeStruct((nb, t, width), BF16),
        scratch_shapes=[pltpu.VMEM((nbk, PREV_ROWS + tt, width), F32)],
        compiler_params=_cparams(2),
        name=name,
    )(u_pool, prev_arr, w_pool_bf, pool_scale.reshape(1, width))


ROW_TOKENS = STRIDE


def _compress_body(r_ref, wexp_ref, w2_ref, pos_ref, o_ref, sh_ref, *, transposed):
    nr = r_ref.shape[1]
    half = wexp_ref.shape[1] // 2
    p = jnp.dot(r_ref[0].astype(BF16), wexp_ref[...], preferred_element_type=F32)
    pp = jnp.dot(pos_ref[...], wexp_ref[...], preferred_element_type=F32)
    posb = pp[0:1, :half] + pp[1:2, half:]
    sh_ref[0:nr, :] = p[:, half:]
    sh_ref[nr:nr + 8, :] = jnp.zeros((8, half), F32)
    h = p[:, :half] + sh_ref[1:nr + 1, :] + posb
    c = jnp.dot(jax.nn.gelu(h).astype(BF16), w2_ref[...], preferred_element_type=F32)
    if transposed:
        ct = c.T
        ones_row = jnp.where(lax.broadcasted_iota(jnp.int32, (VT_ROWS - HEAD_DIM, nr), 0) == 0, 1.0, 0.0)
        c = jnp.concatenate([piece for g in range(N_KV)
                             for piece in (ct[g * HEAD_DIM:(g + 1) * HEAD_DIM], ones_row)], axis=0)
    o_ref[0] = c.astype(BF16)


def _compress(rows, wexp, w2bd, posrows, *, transposed, name):
    nb, nr, k = rows.shape
    half = wexp.shape[1] // 2
    oshape = (nb, N_KV * VT_ROWS, nr) if transposed else (nb, nr, half)
    return pl.pallas_call(
        functools.partial(_compress_body, transposed=transposed),
        grid=(nb,),
        in_specs=[pl.BlockSpec((1, nr, k), lambda b: (b, 0, 0)),
                  _const_spec(wexp.shape), _const_spec(w2bd.shape), _const_spec(posrows.shape)],
        out_specs=pl.BlockSpec((1,) + oshape[1:], lambda b: (b, 0, 0)),
        out_shape=jax.ShapeDtypeStruct(oshape, BF16),
        scratch_shapes=[pltpu.VMEM((nr + 8, half), F32)],
        compiler_params=_cparams(1),
        name=name,
    )(rows, wexp, w2bd, posrows)


def _compress_weights(w1, w2, cmp_pos):
    eye = jnp.eye(N_KV, dtype=F32)
    w_blk = w1.reshape(L_CMP, HEAD_DIM, HEAD_DIM)
    halves = []
    for part in range(L_CMP // ROW_TOKENS):
        wpart = w_blk[part * ROW_TOKENS:(part + 1) * ROW_TOKENS]
        halves.append(jnp.einsum("gh,lio->lgiho", eye, wpart).reshape(ROW_TOKENS * N_KV * HEAD_DIM, N_KV * HEAD_DIM))
    wexp = jnp.concatenate(halves, axis=1).astype(BF16)
    w2bd = jnp.einsum("gh,io->giho", eye, w2).reshape(N_KV * HEAD_DIM, N_KV * HEAD_DIM).astype(BF16)
    pos = jnp.broadcast_to(cmp_pos.reshape(L_CMP // ROW_TOKENS, ROW_TOKENS, 1, HEAD_DIM),
                           (L_CMP // ROW_TOKENS, ROW_TOKENS, N_KV, HEAD_DIM)).reshape(L_CMP // ROW_TOKENS, -1)
    posrows = jnp.zeros((8, pos.shape[1]), F32).at[:pos.shape[0]].set(pos).astype(BF16)
    return wexp, w2bd, posrows


def _bucket_starts():
    d = np.arange(0, 4 * MAX_DIST)
    exact = N_BUCKETS // 2
    nf = np.maximum(d, 1).astype(np.float32)
    large = exact + (np.log(nf / np.float32(exact)) / np.float32(math.log(MAX_DIST / exact))
                     * np.float32(N_BUCKETS - exact)).astype(np.int32)
    bucket = np.where(d < exact, d, np.minimum(large, N_BUCKETS - 1))
    assert bucket.max() == N_BUCKETS - 1 and np.all(np.diff(bucket) >= 0)
    return [int(np.argmax(bucket >= b)) for b in range(1, N_BUCKETS)]


def _bias_minus_far(rel_bias, dist, head_axis):
    tab = (rel_bias - rel_bias[:, N_BUCKETS - 1:]) * math.log2(math.e)
    hshape = (1,) * head_axis + (-1,) + (1,) * (dist.ndim - head_axis)
    dist = jnp.expand_dims(dist, head_axis)
    out = jnp.broadcast_to(tab[:, 0].reshape(hshape),
                           dist.shape[:head_axis] + (tab.shape[0],) + dist.shape[head_axis + 1:])
    for b, start in enumerate(_bucket_starts(), start=1):
        out = jnp.where(dist >= start, tab[:, b].reshape(hshape), out)
    return out


def _prompt_tables(rel_bias, t):
    h = rel_bias.shape[0]
    qo = jnp.arange(TQ, dtype=jnp.int32)
    n_near = WINDOW // L_SEL + 2 * (TQ // L_SEL)
    dd = (WINDOW // L_SEL) - jnp.arange(n_near, dtype=jnp.int32)
    ko = jnp.arange(L_SEL, dtype=jnp.int32)
    d = (L_SEL * dd[:, None, None] + qo[None, None, :] - ko[None, :, None]).reshape(n_near * L_SEL, TQ)
    ok = (d >= 0) & (d < WINDOW)
    near = jnp.where(ok[:, None, :], _bias_minus_far(rel_bias, d, 1), NEG)
    near = near.reshape(n_near * L_SEL, h * TQ)
    n_tiles = t // TQ
    per_tile = TQ // STRIDE
    e0 = per_tile * (n_tiles - 1)
    rows = e0 + t // STRIDE
    e = e0 - jnp.arange(rows, dtype=jnp.int32)
    dc = STRIDE * e[:, None] - (L_CMP - 1) + qo[None, :]
    cmp = jnp.where((dc >= 0)[:, None, :], _bias_minus_far(rel_bias, dc, 1), NEG)
    cmp = cmp.reshape(rows, h * TQ)
    return near.astype(F32), cmp.astype(F32), e0


def _cover_t(n_rows, n_c, n_blk):
    i = np.arange(n_rows)[None, :]
    j = np.arange(n_blk)[:, None]
    start = i * STRIDE
    end = start + L_CMP - 1
    cov = (start <= (j + 1) * L_SEL - 1) & (end >= j * L_SEL) & (i < n_c)
    return jnp.asarray(cov.astype(np.float32), BF16)


def _descending_rank(v, n_valid, n_live=None):
    rows, cols = v.shape
    sub = lax.broadcasted_iota(jnp.int32, (8, cols), 0)

    def count(rank, lo):
        mid = v[lo:lo + 8]
        for jp in range(lo, min(lo + 8, n_valid)):
            other = v[jp:jp + 1, :]
            parts = [jnp.where(other > mid, 1.0, jnp.where((other == mid) & (sub > jp - lo), 1.0, 0.0))]
            if lo > 0:
                parts.insert(0, jnp.where(other > v[:lo], 1.0, 0.0))
            if lo + 8 < rows:
                parts.append(jnp.where(other >= v[lo + 8:], 1.0, 0.0))
            rank = rank + jnp.concatenate(parts, axis=0)
        return rank

    rank = jnp.zeros((rows, cols), F32)
    for lo in range(0, n_valid, 8):
        if n_live is None:
            rank = count(rank, lo)
        else:
            rank = lax.cond(lo < n_live, functools.partial(count, lo=lo), lambda r: r, rank)
    return rank


def _nsa_prompt_body(q_ref, gt_ref, kc_ref, vct_ref, ks_ref, vst_ref, kw_ref, vwt_ref, tcmp_ref, near_ref, cov_ref,
                     o_ref, selb_ref, qaug_ref, qaug1_ref, s0_ref, s1_ref, sw_ref, sd_ref, *, e0, top_n):
    it = pl.program_id(1)
    hd = HEAD_DIM
    rep = q_ref.shape[2] // (N_KV * hd)
    gc = rep * TQ
    cols = N_KV * gc
    n_blk = cov_ref.shape[0]
    per = TQ // L_SEL

    def per_group(x):
        return jnp.concatenate([x[:, g * TQ:(g + 1) * TQ] for g in range(N_KV) for _ in range(rep)], axis=1)

    def pv(vt_ref, keys, p, rows=VT_ROWS):
        return jnp.concatenate(
            [jnp.dot(vt_ref[0, g * rows:(g + 1) * rows, keys], p[:, g * gc:(g + 1) * gc],
                     preferred_element_type=F32) for g in range(N_KV)], axis=1)

    qt = q_ref[0].astype(F32).T
    zero = jnp.zeros((hd, TQ), F32)
    tiles = []
    for g in range(N_KV):
        for r in range(rep):
            blk = qt[(g * rep + r) * hd:(g * rep + r + 1) * hd, :]
            tiles.append(jnp.concatenate([blk if gg == g else zero for gg in range(N_KV)], axis=0))
    qbd = jnp.concatenate(tiles, axis=1).astype(BF16)

    gate = [jnp.concatenate([gt_ref[0, g * GATE_ROWS + br * rep + r:g * GATE_ROWS + br * rep + r + 1, :]
                             for g in range(N_KV) for r in range(rep)], axis=1) for br in range(3)]

    n_ck = kc_ref.shape[1]
    sc = jnp.dot(kc_ref[0], qbd, preferred_element_type=F32)
    t0 = pl.multiple_of(e0 - (TQ // STRIDE) * it, 8)
    sc = sc + tcmp_ref[pl.ds(t0, n_ck), :]

    n_wc = WINDOW // TQ + 1
    offs = []
    for k in range(n_wc):
        kidx = it - (n_wc - 1) + k
        offs.append(pl.multiple_of(jnp.maximum(kidx, 0) * TQ, TQ))
        tab_off = pl.multiple_of(jnp.where(kidx >= 0, k * TQ, near_ref.shape[0] - TQ), TQ)
        sw_ref[k * TQ:(k + 1) * TQ, :] = (jnp.dot(kw_ref[0, pl.ds(offs[k], TQ), :], qbd, preferred_element_type=F32)
                                          + near_ref[pl.ds(tab_off, TQ), :])

    mc = jnp.max(sc, axis=0, keepdims=True)
    pc = jnp.exp2(sc - mc).astype(BF16)
    oc = pv(vct_ref, slice(None), pc)
    inv_lc = jnp.where(mc > 0.1 * NEG, 1.0 / oc[hd:hd + 1], 0.0)
    oc = oc[:hd] * inv_lc
    imp = jnp.dot(cov_ref[...], pc, preferred_element_type=F32) * inv_lc

    sw = sw_ref[...]
    pw = jnp.exp2(sw - jnp.max(sw, axis=0, keepdims=True)).astype(BF16)
    ow = None
    for k in range(n_wc):
        part = pv(vwt_ref, pl.ds(offs[k], TQ), pw[k * TQ:(k + 1) * TQ, :])
        ow = part if ow is None else ow + part

    diag0 = jnp.maximum(it - 1, 0)
    dk = 2 * TQ
    kvw = N_KV * hd
    offd = pl.multiple_of(diag0 * TQ, TQ)
    near_off = pl.multiple_of((WINDOW // L_SEL - per * (it - diag0)) * L_SEL, TQ)
    sd_ref[...] = (jnp.dot(ks_ref[0, pl.ds(offd, dk), 0:kvw], qbd, preferred_element_type=F32)
                   + near_ref[pl.ds(near_off, dk), :])

    v = []
    for g in range(N_KV):
        vg = imp[:, g * gc:g * gc + TQ]
        for r in range(1, rep):
            vg = vg + imp[:, g * gc + r * TQ:g * gc + (r + 1) * TQ]
        v.append(vg)
    v = jnp.concatenate(v, axis=1)
    j = lax.broadcasted_iota(jnp.int32, v.shape, 0)
    lane = lax.broadcasted_iota(jnp.int32, v.shape, 1)
    cur = per * it + (lane % TQ) // L_SEL
    forced = (j == 0) | (j == cur) | (j == cur - 1)
    v = jnp.where(forced, 1e9, jnp.where(j <= cur, v, -1e9))
    rank = _descending_rank(v, n_blk, n_live=per * (it + 1))
    selb = jnp.where((rank < top_n) & (j <= cur), 0.0, NEG)

    selb_ref[...] = per_group(jnp.where(j < per * diag0, selb, NEG))
    gk = FAR_CHUNKS * TQ
    for ref in (qaug_ref, qaug1_ref):
        ref[0:kvw, :] = qbd
        ref[kvw + 16:, :] = jnp.zeros((ref.shape[0] - kvw - 16, cols), BF16)

    def online_step(carry, s, keys):
        m, acc = carry
        m_new = jnp.maximum(m, jnp.max(s, axis=0, keepdims=True))
        p = jnp.exp2(s - m_new).astype(BF16)
        return m_new, jnp.exp2(m - m_new) * acc + pv(vst_ref, keys, p)

    def far_keys(cb):
        return pl.ds(pl.multiple_of(cb * gk, gk), gk)

    def far_scores(cb, q_ref2, s_out):
        sel = selb_ref[pl.ds(pl.multiple_of(cb * FAR_BLOCKS, FAR_BLOCKS), FAR_BLOCKS), :]
        q_ref2[kvw:kvw + 16, :] = jnp.concatenate([sel, jnp.zeros((16 - FAR_BLOCKS, cols), F32)], axis=0).astype(BF16)
        s_out[...] = jnp.dot(ks_ref[0, far_keys(cb), :], q_ref2[...], preferred_element_type=F32)

    def far_pair(jb, carry):
        far_scores(2 * jb, qaug_ref, s0_ref)
        far_scores(2 * jb + 1, qaug1_ref, s1_ref)
        carry = online_step(carry, s0_ref[...], far_keys(2 * jb))
        return online_step(carry, s1_ref[...], far_keys(2 * jb + 1))

    carry = (jnp.full((1, cols), NEG, F32), jnp.zeros((VT_ROWS, cols), F32))
    n_groups = (diag0 + FAR_CHUNKS - 1) // FAR_CHUNKS
    carry = lax.fori_loop(0, (n_groups + 1) // 2, far_pair, carry)

    sd = sd_ref[...]
    rows = []
    for k in range(dk // L_SEL):
        rk = jnp.max(jnp.where(j == per * diag0 + k, selb, NEG), axis=0, keepdims=True)
        rows.append(jnp.broadcast_to(per_group(rk), (L_SEL, cols)))
    sd = sd + jnp.concatenate(rows, axis=0)
    _, acc_s = online_step(carry, sd, pl.ds(offd, dk))

    out_t = (gate[0] * oc + (gate[1] / acc_s[hd:hd + 1]) * acc_s[:hd]
             + (gate[2] / ow[hd:hd + 1]) * ow[:hd])
    pieces = []
    for c in range(0, N_KV * rep, 2):
        pair = jnp.concatenate([out_t[:, c * TQ:(c + 1) * TQ], out_t[:, (c + 1) * TQ:(c + 2) * TQ]], axis=0)
        pieces.append(pair.T)
    o_ref[0] = jnp.concatenate(pieces, axis=1).astype(BF16)


def _nsa_prompt(q, gt, kc, vct, ks, vst, kw, vwt, tcmp, near, cov_t, *, e0, top_n):
    nb, t, nsa_w = q.shape
    kv_w = ks.shape[2]
    cols = (nsa_w // HEAD_DIM) * TQ
    per_b = lambda shape: pl.BlockSpec((1,) + shape, lambda b, i: (b, 0, 0))
    return pl.pallas_call(
        functools.partial(_nsa_prompt_body, e0=e0, top_n=top_n),
        grid=(nb, t // TQ),
        in_specs=[pl.BlockSpec((1, TQ, nsa_w), lambda b, i: (b, i, 0)),
                  pl.BlockSpec((1, N_KV * GATE_ROWS, TQ), lambda b, i: (b, 0, i)),
                  per_b(kc.shape[1:]), per_b(vct.shape[1:]),
                  per_b(ks.shape[1:]), per_b(vst.shape[1:]), per_b(kw.shape[1:]), per_b(vwt.shape[1:]),
                  _const_spec(tcmp.shape), _const_spec(near.shape), _const_spec(cov_t.shape)],
        out_specs=pl.BlockSpec((1, TQ, nsa_w), lambda b, i: (b, i, 0)),
        out_shape=jax.ShapeDtypeStruct((nb, t, nsa_w), BF16),
        scratch_shapes=[pltpu.VMEM((cov_t.shape[0], cols), F32),
                        pltpu.VMEM((ks.shape[2], cols), BF16), pltpu.VMEM((ks.shape[2], cols), BF16),
                        pltpu.VMEM((FAR_CHUNKS * TQ, cols), F32), pltpu.VMEM((FAR_CHUNKS * TQ, cols), F32),
                        pltpu.VMEM((WINDOW + TQ, cols), F32), pltpu.VMEM((2 * TQ, cols), F32)],
        compiler_params=_cparams(2),
        name="nsa_prompt",
    )(q, gt, kc, vct, ks, vst, kw, vwt, tcmp, near, cov_t)


def _dot_nt(a, b):
    return lax.dot_general(a, b, (((1,), (1,)), ((), ())), preferred_element_type=F32)


def _page_copies(pt_ref, pools, bufs, sems, batch, slot, dst):
    n_pages = pt_ref.shape[1]
    out = []
    for i, (pool, buf) in enumerate(zip(pools, bufs)):
        for p in range(n_pages):
            page = 0 if batch is None else pt_ref[batch, p]
            out.append(pltpu.make_async_copy(pool.at[page], dst(buf, slot, p), sems.at[i, slot]))
    return out


def _gather_step(pt_ref, pools, bufs, sems, dst):
    b = pl.program_id(0)
    slot = b % 2

    @pl.when(b == 0)
    def _():
        for cp in _page_copies(pt_ref, pools, bufs, sems, 0, 0, dst):
            cp.start()

    @pl.when(b + 1 < pl.num_programs(0))
    def _():
        for cp in _page_copies(pt_ref, pools, bufs, sems, b + 1, 1 - slot, dst):
            cp.start()

    for cp in _page_copies(pt_ref, pools, bufs, sems, None, slot, dst):
        cp.wait()
    return slot


def _position_bias(pos_ref, wexp_ref):
    half = wexp_ref.shape[1] // 2
    pp = jnp.dot(pos_ref[...], wexp_ref[...], preferred_element_type=F32)
    return pp[0:1, :half] + pp[1:2, half:]


def _compress_pages(buf, slot, perm_ref, x_ref, wexp_ref, w2_ref, posb, sh_ref):
    n_pages = buf.shape[1]
    kvw = buf.shape[2]
    rpp = PAGE_SIZE // ROW_TOKENS
    nr = n_pages * rpp
    half = wexp_ref.shape[1] // 2

    for i in range(n_pages // 2):
        pair = buf[slot, 2 * i:2 * i + 2].reshape(2 * kvw, PAGE_SIZE).astype(BF16)
        xt = _dot_nt(perm_ref[...], pair)
        for pp in range(2):
            for l in range(ROW_TOKENS):
                row0 = (l // 2) * nr + (2 * i + pp) * rpp
                x_ref[row0:row0 + rpp, (l % 2) * kvw:(l % 2 + 1) * kvw] = (
                    xt[l * rpp:(l + 1) * rpp, pp * kvw:(pp + 1) * kvw])
    p = None
    for i in range(ROW_TOKENS // 2):
        part = jnp.dot(x_ref[i * nr:(i + 1) * nr, :].astype(BF16), wexp_ref[2 * i * kvw:2 * (i + 1) * kvw, :],
                       preferred_element_type=F32)
        p = part if p is None else p + part
    sh_ref[0:nr, :] = p[:, half:]
    sh_ref[nr:nr + 8, :] = jnp.zeros((8, half), F32)
    h = p[:, :half] + sh_ref[1:nr + 1, :] + posb
    return jnp.dot(jax.nn.gelu(h).astype(BF16), w2_ref[...], preferred_element_type=F32).astype(BF16)


def _sample_cmp_body(pt_ref, kpool_ref, vpool_ref, qbd_ref, tab_ref, perm_ref, wk_ref, w2k_ref, pk_ref, wv_ref, w2v_ref,
                     pv_ref, cov_ref, oc_ref, imp_ref, kbuf, vbuf, sems, xk_ref, xv_ref, shk_ref, shv_ref, posb_ref,
                     *, rep, t_new):
    slot = _gather_step(pt_ref, (kpool_ref, vpool_ref), (kbuf, vbuf), sems, lambda buf, sl, p: buf.at[sl, p])

    @pl.when(pl.program_id(0) == 0)
    def _():
        posb_ref[0:1, :] = _position_bias(pk_ref, wk_ref)
        posb_ref[1:2, :] = _position_bias(pv_ref, wv_ref)

    kc = _compress_pages(kbuf, slot, perm_ref, xk_ref, wk_ref, w2k_ref, posb_ref[0:1, :], shk_ref)
    vc = _compress_pages(vbuf, slot, perm_ref, xv_ref, wv_ref, w2v_ref, posb_ref[1:2, :], shv_ref)
    s = _dot_nt(qbd_ref[0], kc) + tab_ref[...]
    m = jnp.max(s, axis=1, keepdims=True)
    p = jnp.where(s > 0.1 * NEG, jnp.exp2(s - m), 0.0)
    l = jnp.sum(p, axis=1, keepdims=True)
    pn = (p * jnp.where(l > 0.0, 1.0 / l, 0.0)).astype(BF16)
    oc_ref[0] = jnp.dot(pn, vc, preferred_element_type=F32)
    imp = jnp.dot(pn, cov_ref[...], preferred_element_type=F32)
    per_g = rep * t_new
    parts = []
    for g in range(N_KV):
        acc = imp[g * per_g:g * per_g + t_new]
        for r in range(1, rep):
            acc = acc + imp[g * per_g + r * t_new:g * per_g + (r + 1) * t_new]
        parts.append(acc)
    imp_ref[0] = jnp.concatenate(parts, axis=0)


def _sample_cmp(page_table, kpool_t, vpool_t, qbd, tab, cmp_k, cmp_v, cov, *, rep, t_new):
    nb, n_pages = page_table.shape
    rpp = PAGE_SIZE // ROW_TOKENS
    n_rows = n_pages * rpp
    n_q = qbd.shape[1]
    kvw = N_KV * HEAD_DIM
    tok = np.arange(PAGE_SIZE)
    perm = jnp.asarray((tok[None, :] == (tok[:, None] % rpp) * ROW_TOKENS + tok[:, None] // rpp).astype(np.float32), BF16)
    const = lambda a: pl.BlockSpec(a.shape, lambda b, pt: (0,) * a.ndim, pipeline_mode=pl.Buffered(1))
    grid_spec = pltpu.PrefetchScalarGridSpec(
        num_scalar_prefetch=1,
        grid=(nb,),
        in_specs=[pl.BlockSpec(memory_space=pl.ANY), pl.BlockSpec(memory_space=pl.ANY),
                  pl.BlockSpec((1, n_q, kvw), lambda b, pt: (b, 0, 0)), const(tab), const(perm),
                  const(cmp_k[0]), const(cmp_k[1]), const(cmp_k[2]), const(cmp_v[0]), const(cmp_v[1]), const(cmp_v[2]),
                  const(cov)],
        out_specs=[pl.BlockSpec((1, n_q, kvw), lambda b, pt: (b, 0, 0)),
                   pl.BlockSpec((1, N_KV * t_new, cov.shape[1]), lambda b, pt: (b, 0, 0))],
        scratch_shapes=[pltpu.VMEM((2, n_pages, kvw, PAGE_SIZE), F32), pltpu.VMEM((2, n_pages, kvw, PAGE_SIZE), F32),
                        pltpu.SemaphoreType.DMA((2, 2)),
                        pltpu.VMEM((n_pages * PAGE_SIZE // 2, 2 * kvw), F32),
                        pltpu.VMEM((n_pages * PAGE_SIZE // 2, 2 * kvw), F32),
                        pltpu.VMEM((n_rows + 8, kvw), F32), pltpu.VMEM((n_rows + 8, kvw), F32),
                        pltpu.VMEM((8, kvw), F32)],
    )
    return pl.pallas_call(
        functools.partial(_sample_cmp_body, rep=rep, t_new=t_new),
        grid_spec=grid_spec,
        out_shape=[jax.ShapeDtypeStruct((nb, n_q, kvw), F32),
                   jax.ShapeDtypeStruct((nb, N_KV * t_new, cov.shape[1]), F32)],
        compiler_params=_cparams(1),
        name="sample_cmp",
    )(page_table, kpool_t, vpool_t, qbd, tab, perm, *cmp_k, *cmp_v, cov)


def _select_body(imp_ref, o_ref, *, top_n, n_blk, pos0, t_new):
    rows, cols = imp_ref.shape
    j = lax.broadcasted_iota(jnp.int32, (rows, cols), 0)
    col = pl.program_id(0) * cols + lax.broadcasted_iota(jnp.int32, (rows, cols), 1)
    cur = (pos0 + col % t_new) // L_SEL
    forced = (j == 0) | (j == cur) | (j == cur - 1)
    v = jnp.where(forced, 1e9, jnp.where(j <= cur, imp_ref[...], -1e9))
    v = jnp.where(j < n_blk, v, -3e38)
    rank = _descending_rank(v, n_blk)
    o_ref[...] = jnp.where((rank < top_n) & (j <= cur) & (j < n_blk), 0.0, NEG)


def _select_blocks(imp_t, *, top_n, n_blk, pos0, t_new):
    rows, cols = imp_t.shape
    tile = 4 * LANE
    return pl.pallas_call(
        functools.partial(_select_body, top_n=top_n, n_blk=n_blk, pos0=pos0, t_new=t_new),
        grid=(cols // tile,),
        in_specs=[pl.BlockSpec((rows, tile), lambda i: (0, i))],
        out_specs=pl.BlockSpec((rows, tile), lambda i: (0, i)),
        out_shape=jax.ShapeDtypeStruct((rows, cols), F32),
        compiler_params=_cparams(1),
        name="sample_select",
    )(imp_t)


def _softmax_pv(parts):
    m = None
    for s, _, _ in parts:
        mk = jnp.max(s, axis=1, keepdims=True)
        m = mk if m is None else jnp.maximum(m, mk)
    l = None
    o = None
    for s, v, channel_major in parts:
        p = jnp.exp2(s - m)
        lk = jnp.sum(p, axis=1, keepdims=True)
        pb = p.astype(BF16)
        ok = _dot_nt(pb, v) if channel_major else jnp.dot(pb, v, preferred_element_type=F32)
        l = lk if l is None else l + lk
        o = ok if o is None else o + ok
    return o / l


def _sample_slc_body(pt_ref, kpool_ref, vpool_ref, qbd_ref, selb_ref, gate_ref, oc_ref, ksn_ref, vsn_ref, kwn_ref,
                     vwn_ref, kwin_ref, vwin_ref, near_ref, newtab_ref, wtab_ref, expand_ref,
                     o_ref, kbuf, vbuf, sems, pad_ref, *, rep, t_new):
    b = pl.program_id(0)
    slot = _gather_step(pt_ref, (kpool_ref, vpool_ref), (kbuf, vbuf), sems,
                        lambda buf, sl, p: buf.at[sl, :, pl.ds(p * PAGE_SIZE, PAGE_SIZE)])
    hd = HEAD_DIM
    kvw = N_KV * hd

    @pl.when(b == 0)
    def _():
        pad_ref[...] = jnp.zeros(pad_ref.shape, F32)

    news = []
    for i, ref in enumerate((ksn_ref, vsn_ref, kwn_ref, vwn_ref)):
        pad_ref[i, 0:t_new, :] = ref[0]
        news.append(pad_ref[i].astype(BF16))
    ks_new, vs_new, kw_new, vw_new = news
    qbd = qbd_ref[0]

    per_g = rep * t_new
    sel = selb_ref[0]
    sel_rows = jnp.concatenate([sel[g * t_new:(g + 1) * t_new] for g in range(N_KV) for _ in range(rep)], axis=0)
    s_past = (jnp.dot(qbd, kbuf[slot].astype(BF16), preferred_element_type=F32)
              + jnp.dot(sel_rows.astype(BF16), expand_ref[...], preferred_element_type=F32))
    n_past = s_past.shape[1]
    n_near = near_ref.shape[1]
    s_past = jnp.concatenate([s_past[:, :n_past - n_near], s_past[:, n_past - n_near:] + near_ref[...]], axis=1)
    s_new = _dot_nt(qbd, ks_new) + newtab_ref[0]
    o_s = _softmax_pv([(s_past, vbuf[slot].astype(BF16), True), (s_new, vs_new, False)])

    s_win = jnp.dot(qbd, kwin_ref[0].astype(BF16), preferred_element_type=F32) + wtab_ref[...]
    s_wnew = _dot_nt(qbd, kw_new) + newtab_ref[1]
    o_w = _softmax_pv([(s_win, vwin_ref[0].astype(BF16), True), (s_wnew, vw_new, False)])

    gate = gate_ref[0]
    comb = gate[:, 0:1] * oc_ref[0] + gate[:, 1:2] * o_s + gate[:, 2:3] * o_w
    row = lax.broadcasted_iota(jnp.int32, comb.shape, 0)
    o_ref[0] = jnp.where(row < per_g, comb, pltpu.roll(comb, hd, axis=1))


def _sample_slc(page_table, kpool, vpool, qbd, selb, gate_rows, oc, ks_new, vs_new, kw_new, vw_new, kwin, vwin,
                near, newtab, wtab, expand, *, rep, t_new):
    nb, n_pages = page_table.shape
    n_keys = n_pages * PAGE_SIZE
    n_q = qbd.shape[1]
    kvw = N_KV * HEAD_DIM
    const = lambda a: pl.BlockSpec(a.shape, lambda b, pt: (0,) * a.ndim, pipeline_mode=pl.Buffered(1))
    per_b = lambda a: pl.BlockSpec((1,) + a.shape[1:], lambda b, pt: (b,) + (0,) * (a.ndim - 1))
    grid_spec = pltpu.PrefetchScalarGridSpec(
        num_scalar_prefetch=1,
        grid=(nb,),
        in_specs=[pl.BlockSpec(memory_space=pl.ANY), pl.BlockSpec(memory_space=pl.ANY),
                  per_b(qbd), per_b(selb), per_b(gate_rows), per_b(oc),
                  per_b(ks_new), per_b(vs_new), per_b(kw_new), per_b(vw_new), per_b(kwin), per_b(vwin),
                  const(near), const(newtab), const(wtab), const(expand)],
        out_specs=pl.BlockSpec((1, n_q, kvw), lambda b, pt: (b, 0, 0)),
        scratch_shapes=[pltpu.VMEM((2, kvw, n_keys), F32), pltpu.VMEM((2, kvw, n_keys), F32),
                        pltpu.SemaphoreType.DMA((2, 2)), pltpu.VMEM((4, LANE, kvw), F32)],
    )
    return pl.pallas_call(
        functools.partial(_sample_slc_body, rep=rep, t_new=t_new),
        grid_spec=grid_spec,
        out_shape=jax.ShapeDtypeStruct((nb, n_q, kvw), F32),
        compiler_params=_cparams(1),
        name="sample_slc_win",
    )(page_table, kpool, vpool, qbd, selb, gate_rows, oc, ks_new, vs_new, kw_new, vw_new, kwin, vwin,
      near, newtab, wtab, expand)


def _sample_tables(rel_bias, pos0, t_new, n_rows, n_c, wbuf):
    h = rel_bias.shape[0]
    qpos = pos0 + jnp.arange(t_new, dtype=jnp.int32)

    def table(kpos, ok_extra=None, upper=None):
        d = qpos[:, None] - kpos[None, :]
        ok = d >= 0
        if upper is not None:
            ok = ok & (d < upper)
        if ok_extra is not None:
            ok = ok & ok_extra[None, :]
        tab = jnp.where(ok[None], _bias_minus_far(rel_bias, d, 0), NEG)
        return tab.reshape(h * t_new, kpos.shape[0]).astype(F32)

    c = jnp.arange(n_rows, dtype=jnp.int32)
    cmp = table(c * STRIDE + (L_CMP - 1), ok_extra=c < n_c)
    near = table(pos0 - PAGE_SIZE + jnp.arange(PAGE_SIZE, dtype=jnp.int32))
    lane = jnp.arange(LANE, dtype=jnp.int32)
    new_slc = table(pos0 + lane, ok_extra=lane < t_new)
    new_win = table(pos0 + lane, ok_extra=lane < t_new, upper=WINDOW)
    kpos_w = pos0 - wbuf + jnp.arange(wbuf, dtype=jnp.int32)
    win = table(kpos_w, ok_extra=kpos_w >= 0, upper=WINDOW)
    return cmp, near, jnp.stack([new_slc, new_win]), win


def _layer_weights(p, n_heads):
    d = p["w_in"].shape[0]
    kv_end = d + 6 * N_KV * HEAD_DIM
    src = _gate_column_source(n_heads)
    gate_cols = jnp.where(jnp.asarray(src >= 0)[None, :], p["w_in"][:, kv_end + np.maximum(src, 0)], 0.0)
    ln = jnp.stack([p["ln_gain"][0], p["ln_bias"][0], p["ln_gain"][1], p["ln_bias"][1],
                    p["ln_gain"][2], p["ln_bias"][2]])
    return dict(
        wi1=p["w_ffn1_in"].astype(BF16), wo1=p["w_ffn1_out"].astype(BF16),
        wi2=p["w_ffn2_in"].astype(BF16), wo2=p["w_ffn2_out"].astype(BF16),
        wp=jnp.concatenate([p["w_in"][:, :kv_end], gate_cols], axis=1).astype(BF16),
        wout=p["w_out"].astype(BF16),
        wpool=p["w_pool"].astype(BF16), pool_scale=p["pool_scale"],
        ln1=ln[0:2], ln23=ln[2:6],
        cmp_k=_compress_weights(p["w_phi_k1"], p["w_phi_k2"], p["cmp_pos"]),
        cmp_v=_compress_weights(p["w_phi_v1"], p["w_phi_v2"], p["cmp_pos"]),
    )


PROMPT_TOKENS = 512


def _prompt_layer(x, mod, w, rel_bias, alpha):
    nb, t, d = x.shape
    assert t % TQ == 0 and t % PROMPT_TOKENS == 0
    (x1, u_pool, q, k_c, v_c, k_s, v_s, k_w, v_w, ks_bf, vs_t, kw_bf, vw_t, gates_t) = _ffn_in(
        x, mod, w["wi1"], w["wo1"], w["wp"], w["ln1"], alpha=alpha, nbk=1, tt=PROMPT_TOKENS, for_prompt=True)
    pool_out = _pool_mix(u_pool, None, w["wpool"], w["pool_scale"], pos0=0, nbk=1, tt=PROMPT_TOKENS,
                         name="pool_prompt")
    row_w = ROW_TOKENS * N_KV * HEAD_DIM
    kc_cmp = _compress(k_c.reshape(nb, t // ROW_TOKENS, row_w), *w["cmp_k"], transposed=False, name="compress_k_prompt")
    vc_cmp_t = _compress(v_c.reshape(nb, t // ROW_TOKENS, row_w), *w["cmp_v"], transposed=True, name="compress_v_prompt")
    n_c = (t - L_CMP) // STRIDE + 1
    n_blk = t // L_SEL
    near, tcmp, e0 = _prompt_tables(rel_bias, t)
    nsa = _nsa_prompt(q, gates_t, kc_cmp, vc_cmp_t, ks_bf, vs_t, kw_bf, vw_t, tcmp, near,
                      _cover_t(t // ROW_TOKENS, n_c, n_blk), e0=e0, top_n=min(TOP_N, n_blk))
    y = _ffn_out(pool_out, nsa, x1, mod, w["wout"], w["wi2"], w["wo2"], w["ln23"], alpha=alpha,
                 nbk=1, tt=PROMPT_TOKENS, name="outproj_ffn2_prompt")
    keep = min(WINDOW, t)
    heads = lambda a: a.reshape(nb, -1, N_KV, HEAD_DIM)
    states = (heads(k_c), heads(v_c), heads(k_s), heads(v_s), heads(k_w[:, t - keep:]), heads(v_w[:, t - keep:]),
              u_pool[:, t - POOL_STATE:])
    return y, states


SAMPLE_BATCHES = 32


def _sample_layer(x, mod, w, rel_bias, alpha, page_table, pools, kwin, vwin, pool_state):
    nb, t, d = x.shape
    hd = HEAD_DIM
    n_heads = rel_bias.shape[0]
    rep = n_heads // N_KV
    kvw = N_KV * hd
    pos0 = page_table.shape[1] * PAGE_SIZE
    n_c = (pos0 + t - L_CMP) // STRIDE + 1
    assert (n_c - 1) * STRIDE + L_CMP <= pos0 and pos0 % PAGE_SIZE == 0 and nb % SAMPLE_BATCHES == 0
    n_rows = pos0 // ROW_TOKENS
    n_blk = -(-(pos0 + t) // L_SEL)
    assert n_blk - 1 == pos0 // L_SEL
    blk_pad = -(-n_blk // LANE) * LANE
    wbuf = kwin.shape[1]

    x1, u_pool, q, k_c, v_c, k_s, v_s, k_w, v_w, gates = _ffn_in(
        x, mod, w["wi1"], w["wo1"], w["wp"], w["ln1"], alpha=alpha, nbk=SAMPLE_BATCHES, tt=t, for_prompt=False)
    pw = u_pool.shape[-1]
    prev = jnp.concatenate([jnp.zeros((nb, PREV_ROWS - POOL_STATE, pw), F32), pool_state], axis=1)
    pool_out = _pool_mix(u_pool, prev, w["wpool"], w["pool_scale"], pos0=pos0, nbk=SAMPLE_BATCHES, tt=t,
                         name="pool_sample")

    qh = q.reshape(nb, t, N_KV, rep, hd).transpose(0, 2, 3, 1, 4)
    own = jnp.eye(N_KV, dtype=q.dtype)[None, :, None, None, :, None]
    qbd = (qh[:, :, :, :, None, :] * own).reshape(nb, N_KV * rep * t, kvw)
    gate_rows = gates[:, :, :N_KV * GATE_ROWS].reshape(nb, t, N_KV, GATE_ROWS)[..., :3 * rep]
    gate_rows = gate_rows.reshape(nb, t, N_KV, 3, rep).transpose(0, 2, 4, 1, 3).reshape(nb, N_KV * rep * t, 3)
    gate_rows = jnp.pad(gate_rows, ((0, 0), (0, 0), (0, LANE - 3)))

    tab_c, near, newtab, wtab = _sample_tables(rel_bias, pos0, t, n_rows, n_c, wbuf)
    cov = np.zeros((n_rows, blk_pad), np.float32)
    ci = np.arange(n_rows)[:, None]
    bj = np.arange(blk_pad)[None, :]
    cov[:] = ((ci * STRIDE <= (bj + 1) * L_SEL - 1) & (ci * STRIDE + L_CMP - 1 >= bj * L_SEL)
              & (ci < n_c) & (bj < n_blk))
    channel_major = lambda a: a.transpose(0, 2, 3, 1).reshape(a.shape[0], kvw, a.shape[1])
    oc, imp = _sample_cmp(page_table, channel_major(pools[0]), channel_major(pools[1]), qbd, tab_c, w["cmp_k"],
                          w["cmp_v"], jnp.asarray(cov, BF16), rep=rep, t_new=t)

    sel_rows = -(-n_blk // 8) * 8
    imp_t = imp.transpose(2, 0, 1).reshape(blk_pad, nb * N_KV * t)[:sel_rows]
    selb_t = _select_blocks(imp_t, top_n=min(TOP_N, n_blk), n_blk=n_blk, pos0=pos0, t_new=t)
    selb_t = jnp.pad(selb_t, ((0, blk_pad - sel_rows), (0, 0)), constant_values=NEG)
    selb = selb_t.reshape(blk_pad, nb, N_KV * t).transpose(1, 2, 0)

    key_blk = lax.broadcasted_iota(jnp.int32, (blk_pad, pos0), 1) // L_SEL
    expand = (key_blk == lax.broadcasted_iota(jnp.int32, (blk_pad, pos0), 0)).astype(BF16)
    kvf = lambda a: a.reshape(a.shape[0], a.shape[1], kvw)
    o = _sample_slc(page_table, channel_major(pools[2]), channel_major(pools[3]), qbd, selb, gate_rows, oc,
                    k_s, v_s, k_w, v_w, channel_major(kwin), channel_major(vwin), near, newtab, wtab, expand,
                    rep=rep, t_new=t)
    nsa = o[:, :, :hd].reshape(nb, N_KV, rep, t, hd).transpose(0, 3, 1, 2, 4).reshape(nb, t, n_heads * hd).astype(BF16)

    y = _ffn_out(pool_out, nsa, x1, mod, w["wout"], w["wi2"], w["wo2"], w["ln23"], alpha=alpha,
                 nbk=SAMPLE_BATCHES, tt=t, name="outproj_ffn2_sample")
    keep = min(WINDOW, pos0 + t)
    heads = lambda a: a.reshape(nb, -1, N_KV, HEAD_DIM)
    kw_ext = jnp.concatenate([kvf(kwin), k_w], axis=1)
    vw_ext = jnp.concatenate([kvf(vwin), v_w], axis=1)
    pool_ext = jnp.concatenate([pool_state, u_pool], axis=1)
    states = (heads(k_c), heads(v_c), heads(k_s), heads(v_s), heads(kw_ext[:, kw_ext.shape[1] - keep:]),
              heads(vw_ext[:, vw_ext.shape[1] - keep:]), pool_ext[:, pool_ext.shape[1] - POOL_STATE:])
    return y, states


def kernel(x_prompt, x_sample, c_prompt, c_sample, cache_k_cmp, cache_v_cmp, cache_k_slc, cache_v_slc, page_table,
           state_k_win, state_v_win, state_pool, w_ada, b_ada, ln_gain, ln_bias, w_ffn1_in, w_ffn1_out, w_ffn2_in,
           w_ffn2_out, w_in, w_phi_k1, w_phi_k2, w_phi_v1, w_phi_v2, cmp_pos, w_pool, pool_scale, w_out, rel_bias):
    depth = w_ada.shape[0]
    alpha = (2.0 * depth) ** 0.25
    n_heads = rel_bias.shape[0]
    nb_p, nb_s = x_prompt.shape[0], x_sample.shape[0]
    d = x_prompt.shape[-1]
    xp, xs = x_prompt, x_sample
    c_all = jnp.concatenate([c_prompt, c_sample], axis=0)
    p_states, s_states = [], []
    for l in range(depth):
        p = dict(w_ada=w_ada[l], b_ada=b_ada[l], ln_gain=ln_gain[l], ln_bias=ln_bias[l], w_ffn1_in=w_ffn1_in[l],
                 w_ffn1_out=w_ffn1_out[l], w_ffn2_in=w_ffn2_in[l], w_ffn2_out=w_ffn2_out[l], w_in=w_in[l],
                 w_phi_k1=w_phi_k1[l], w_phi_k2=w_phi_k2[l], w_phi_v1=w_phi_v1[l], w_phi_v2=w_phi_v2[l],
                 cmp_pos=cmp_pos[l], w_pool=w_pool[l], pool_scale=pool_scale[l], w_out=w_out[l])
        w = _layer_weights(p, n_heads)
        mod = _ada(c_all, p["w_ada"], p["b_ada"]).reshape(nb_p + nb_s, 9, d)
        xp, sp = _prompt_layer(xp, mod[:nb_p], w, rel_bias, alpha)
        xs, ss = _sample_layer(xs, mod[nb_p:], w, rel_bias, alpha, page_table,
                               (cache_k_cmp[l], cache_v_cmp[l], cache_k_slc[l], cache_v_slc[l]),
                               state_k_win[l], state_v_win[l], state_pool[l])
        p_states.append(sp)
        s_states.append(ss)
    stack = lambda states: tuple(jnp.stack(a) for a in zip(*states))
    return (xp, xs) + stack(p_states) + stack(s_states)
```

```python
import functools
import math

import numpy as np
import jax
import jax.numpy as jnp
from jax import lax
from jax.experimental import pallas as pl
from jax.experimental.pallas import tpu as pltpu

F32 = jnp.float32
BF16 = jnp.bfloat16

HEAD_DIM = 64
N_KV = 2
L_CMP = 32
STRIDE = 16
L_SEL = 64
TOP_N = 16
WINDOW = 512
N_BUCKETS = 32
MAX_DIST = 128
POOL_WINDOWS = (2, 4, 8, 16)
POOL_STATE = max(POOL_WINDOWS) - 1
LN_EPS = 1e-5
NEG = -1e30
PAGE_SIZE = 128

Q_SCALE = HEAD_DIM ** -0.5 * math.log2(math.e)
TQ = 128
FAR_CHUNKS = 4
FAR_BLOCKS = FAR_CHUNKS * TQ // L_SEL
VT_ROWS = HEAD_DIM + 16

LANE = 128
MXU_COLS = 256
FF_CHUNK = MXU_COLS
VMEM_LIMIT = 56 * 1024 * 1024


def _cparams(n_axes):
    return pltpu.CompilerParams(dimension_semantics=("arbitrary",) * n_axes,
                                vmem_limit_bytes=VMEM_LIMIT)


def _const_spec(shape):
    nd = len(shape)
    return pl.BlockSpec(shape, lambda *_: (0,) * nd, pipeline_mode=pl.Buffered(1))


def _layer_norm(y, gain, bias):
    mu = jnp.mean(y, axis=-1, keepdims=True)
    yc = y - mu
    var = jnp.mean(yc * yc, axis=-1, keepdims=True)
    return yc * lax.rsqrt(var + LN_EPS) * gain + bias


def _swiglu(u_bf, wi_ref, wo_ref):
    d_ff = wo_ref.shape[0]
    acc = None
    for c in range(d_ff // FF_CHUNK):
        lo = c * FF_CHUNK
        a = jnp.dot(u_bf, wi_ref[:, lo:lo + FF_CHUNK], preferred_element_type=F32)
        b = jnp.dot(u_bf, wi_ref[:, d_ff + lo:d_ff + lo + FF_CHUNK], preferred_element_type=F32)
        hid = (a * jax.nn.sigmoid(a) * b).astype(BF16)
        part = jnp.dot(hid, wo_ref[lo:lo + FF_CHUNK, :], preferred_element_type=F32)
        acc = part if acc is None else acc + part
    return acc


def _ada_body(c_ref, w_ref, b_ref, o_ref):
    c = c_ref[...]
    act = (c * jax.nn.sigmoid(c)).astype(BF16)
    o_ref[...] = jnp.dot(act, w_ref[...].astype(BF16), preferred_element_type=F32) + b_ref[...]


def _ada(c_all, w_ada, b_ada):
    nb, d = c_all.shape
    n_out = w_ada.shape[1]
    cols = 9 * LANE
    return pl.pallas_call(
        _ada_body,
        grid=(n_out // cols,),
        in_specs=[pl.BlockSpec((nb, d), lambda j: (0, 0)),
                  pl.BlockSpec((d, cols), lambda j: (0, j)),
                  pl.BlockSpec((1, cols), lambda j: (0, j))],
        out_specs=pl.BlockSpec((nb, cols), lambda j: (0, j)),
        out_shape=jax.ShapeDtypeStruct((nb, n_out), F32),
        compiler_params=_cparams(1),
        name="ada",
    )(c_all, w_ada, b_ada.reshape(1, n_out))


KV_NAMES = ("k_c", "v_c", "k_s", "v_s", "k_w", "v_w")


def _ffn_in_body(x_ref, mod_ref, wi_ref, wo_ref, wp_ref, ln_ref, *out_refs, alpha, pool_w, nsa_w, kv_w, for_prompt):
    nbk, tt, d = x_ref.shape
    n = nbk * tt
    x = x_ref[...]
    mod = mod_ref[...]
    u = (x * (1.0 + mod[:, 1:2, :]) + mod[:, 0:1, :]).reshape(n, d).astype(BF16)
    h = _swiglu(u, wi_ref, wo_ref).reshape(nbk, tt, d)
    x1 = _layer_norm(alpha * x + 0.5 * mod[:, 2:3, :] * h, ln_ref[0:1, :], ln_ref[1:2, :])
    u1 = (x1 * (1.0 + mod[:, 4:5, :]) + mod[:, 3:4, :]).reshape(n, d).astype(BF16)
    proj = jnp.dot(u1, wp_ref[...], preferred_element_type=F32)

    x1_ref, upool_ref, q_ref = out_refs[:3]
    kv_refs = out_refs[3:9]
    x1_ref[...] = x1
    upool_ref[...] = proj[:, :pool_w].reshape(nbk, tt, pool_w)
    q_ref[...] = (proj[:, pool_w:pool_w + nsa_w] * Q_SCALE).astype(BF16).reshape(nbk, tt, nsa_w)
    off = pool_w + nsa_w
    kv = []
    for i in range(6):
        blk = proj[:, off + i * kv_w:off + (i + 1) * kv_w]
        kv.append(blk)
        kv_refs[i][...] = blk.reshape(nbk, tt, kv_w)
    gates = jax.nn.sigmoid(proj[:, off + 6 * kv_w:off + 6 * kv_w + LANE])
    if for_prompt:
        ksb_ref, vst_ref, kwb_ref, vwt_ref, gt_ref = out_refs[9:]
        row = lax.broadcasted_iota(jnp.int32, (n, kv_w), 0)
        lane = lax.broadcasted_iota(jnp.int32, (n, kv_w), 1)
        onehot = jnp.where((row // L_SEL) % FAR_BLOCKS == lane, 1.0, 0.0)
        ones_row = jnp.where(lax.broadcasted_iota(jnp.int32, (VT_ROWS - HEAD_DIM, n), 0) == 0, 1.0, 0.0)

        def values_t(x):
            xt = x.T
            return jnp.concatenate([piece for g in range(N_KV)
                                    for piece in (xt[g * HEAD_DIM:(g + 1) * HEAD_DIM], ones_row)], axis=0).astype(BF16)

        ksb_ref[0] = jnp.concatenate([kv[2], onehot], axis=1).astype(BF16)
        vst_ref[0] = values_t(kv[3])
        kwb_ref[0] = kv[4].astype(BF16)
        vwt_ref[0] = values_t(kv[5])
        gt_ref[0] = gates.T[:gt_ref.shape[1], :]
    else:
        out_refs[9][...] = gates.reshape(nbk, tt, LANE)


def _ffn_in(x, mod, wi, wo, wp, ln, *, alpha, nbk, tt, for_prompt):
    nb, t, d = x.shape
    pool_w = d // 2
    nsa_w = d - pool_w
    kv_w = N_KV * HEAD_DIM
    grid = (nb // nbk, t // tt)
    tok = lambda w: pl.BlockSpec((nbk, tt, w), lambda i, j: (i, j, 0))
    out_specs = [tok(d), tok(pool_w), tok(nsa_w)] + [tok(kv_w)] * 6
    out_shape = ([jax.ShapeDtypeStruct((nb, t, d), F32), jax.ShapeDtypeStruct((nb, t, pool_w), F32),
                  jax.ShapeDtypeStruct((nb, t, nsa_w), BF16)]
                 + [jax.ShapeDtypeStruct((nb, t, kv_w), F32)] * 6)
    if for_prompt:
        assert nbk == 1 and tt % (FAR_BLOCKS * L_SEL) == 0
        tr = lambda rows: pl.BlockSpec((1, rows, tt), lambda i, j: (i, 0, j))
        vt_rows = N_KV * VT_ROWS
        out_specs += [tok(2 * kv_w), tr(vt_rows), tok(kv_w), tr(vt_rows), tr(N_KV * GATE_ROWS)]
        out_shape += [jax.ShapeDtypeStruct((nb, t, 2 * kv_w), BF16), jax.ShapeDtypeStruct((nb, vt_rows, t), BF16),
                      jax.ShapeDtypeStruct((nb, t, kv_w), BF16), jax.ShapeDtypeStruct((nb, vt_rows, t), BF16),
                      jax.ShapeDtypeStruct((nb, N_KV * GATE_ROWS, t), F32)]
    else:
        out_specs += [tok(LANE)]
        out_shape += [jax.ShapeDtypeStruct((nb, t, LANE), F32)]
    body = functools.partial(_ffn_in_body, alpha=alpha, pool_w=pool_w, nsa_w=nsa_w, kv_w=kv_w,
                             for_prompt=for_prompt)
    return pl.pallas_call(
        body,
        grid=grid,
        in_specs=[tok(d),
                  pl.BlockSpec((nbk, mod.shape[1], d), lambda i, j: (i, 0, 0)),
                  _const_spec(wi.shape), _const_spec(wo.shape), _const_spec(wp.shape), _const_spec(ln.shape)],
        out_specs=out_specs,
        out_shape=out_shape,
        compiler_params=_cparams(2),
        name="ffn1_inproj_prompt" if for_prompt else "ffn1_inproj_sample",
    )(x, mod, wi, wo, wp, ln)


GATE_ROWS = 16


def _gate_column_source(n_heads):
    rep = n_heads // N_KV
    src = np.full((LANE,), -1, np.int32)
    for g in range(N_KV):
        for br in range(3):
            for r in range(rep):
                src[g * GATE_ROWS + br * rep + r] = br * n_heads + g * rep + r
    return src


def _ffn_out_body(pool_ref, nsa_ref, x1_ref, mod_ref, wout_ref, wi_ref, wo_ref, ln_ref, o_ref, *, alpha):
    nbk, tt, d = x1_ref.shape
    n = nbk * tt
    pw = pool_ref.shape[-1]
    mod = mod_ref[...]
    x1 = x1_ref[...]
    mix = (jnp.dot(pool_ref[...].reshape(n, pw), wout_ref[:pw, :], preferred_element_type=F32)
           + jnp.dot(nsa_ref[...].reshape(n, d - pw), wout_ref[pw:, :], preferred_element_type=F32))
    x2 = _layer_norm(alpha * x1 + mod[:, 5:6, :] * mix.reshape(nbk, tt, d), ln_ref[0:1, :], ln_ref[1:2, :])
    u = (x2 * (1.0 + mod[:, 7:8, :]) + mod[:, 6:7, :]).reshape(n, d).astype(BF16)
    h = _swiglu(u, wi_ref, wo_ref).reshape(nbk, tt, d)
    o_ref[...] = _layer_norm(alpha * x2 + 0.5 * mod[:, 8:9, :] * h, ln_ref[2:3, :], ln_ref[3:4, :])


def _ffn_out(pool_out, nsa_out, x1, mod, wout, wi, wo, ln, *, alpha, nbk, tt, name):
    nb, t, d = x1.shape
    tok = lambda w: pl.BlockSpec((nbk, tt, w), lambda i, j: (i, j, 0))
    return pl.pallas_call(
        functools.partial(_ffn_out_body, alpha=alpha),
        grid=(nb // nbk, t // tt),
        in_specs=[tok(pool_out.shape[-1]), tok(nsa_out.shape[-1]), tok(d),
                  pl.BlockSpec((nbk, mod.shape[1], d), lambda i, j: (i, 0, 0)),
                  _const_spec(wout.shape), _const_spec(wi.shape), _const_spec(wo.shape), _const_spec(ln.shape)],
        out_specs=tok(d),
        out_shape=jax.ShapeDtypeStruct((nb, t, d), F32),
        compiler_params=_cparams(2),
        name=name,
    )(pool_out, nsa_out, x1, mod, wout, wi, wo, ln)


PREV_ROWS = 16


def _pool_body(u_ref, prev_ref, w_ref, scale_ref, o_ref, ext_ref, *, pos0, zero_first):
    nbk, tt, width = u_ref.shape
    gw = width // len(POOL_WINDOWS)
    j = pl.program_id(1)
    u = u_ref[...]
    prev = prev_ref[...]
    if zero_first:
        prev = jnp.where(j == 0, 0.0, prev)
    ext_ref[:, 0:PREV_ROWS, :] = prev
    ext_ref[:, PREV_ROWS:, :] = u
    pos = pos0 + j * tt + lax.broadcasted_iota(jnp.int32, (1, tt, 1), 1)
    for g, w in enumerate(POOL_WINDOWS):
        lo = g * gw
        tot = None
        for k in range(w):
            part = ext_ref[:, PREV_ROWS - k:PREV_ROWS - k + tt, lo:lo + gw]
            tot = part if tot is None else tot + part
        cnt = jnp.minimum(pos + 1, w).astype(F32)
        pooled = (tot / cnt - u[:, :, lo:lo + gw]).astype(BF16).reshape(nbk * tt, gw)
        mixed = jnp.dot(pooled, w_ref[g], preferred_element_type=F32) * scale_ref[:, lo:lo + gw]
        o_ref[:, :, lo:lo + gw] = mixed.astype(BF16).reshape(nbk, tt, gw)


def _pool_mix(u_pool, prev, w_pool_bf, pool_scale, *, pos0, nbk, tt, name):
    nb, t, width = u_pool.shape
    tok = pl.BlockSpec((nbk, tt, width), lambda i, j: (i, j, 0))
    if prev is None:
        assert nbk == 1 and tt % PREV_ROWS == 0 and pos0 == 0
        step = tt // PREV_ROWS
        prev_arr = u_pool
        prev_spec = pl.BlockSpec((1, PREV_ROWS, width), lambda i, j: (i, jnp.maximum(j * step - 1, 0), 0))
    else:
        assert t == tt
        prev_arr = prev
        prev_spec = pl.BlockSpec((nbk, PREV_ROWS, width), lambda i, j: (i, 0, 0))
    return pl.pallas_call(
        functools.partial(_pool_body, pos0=pos0, zero_first=prev is None),
        grid=(nb // nbk, t // tt),
        in_specs=[tok, prev_spec, _const_spec(w_pool_bf.shape), _const_spec((1, width))],
        out_specs=tok,
        out_shape=jax.ShapeDtypeStruct((nb, t, width), BF16),
        scratch_shapes=[pltpu.VMEM((nbk, PREV_ROWS + tt, width), F32)],
        compiler_params=_cparams(2),
        name=name,
    )(u_pool, prev_arr, w_pool_bf, pool_scale.reshape(1, width))


ROW_TOKENS = STRIDE


def _compress_body(k_ref, wexp_ref, w2_ref, pos_ref, o_ref, sh_ref, *, transposed):
    kvw = k_ref.shape[2]
    nr = k_ref.shape[1] // ROW_TOKENS
    half = wexp_ref.shape[1] // 2
    p = None
    for i in range(ROW_TOKENS // 2):
        lhs = jnp.concatenate([k_ref[0, pl.ds(2 * i + e, nr, stride=ROW_TOKENS), :] for e in range(2)], axis=1)
        part = jnp.dot(lhs.astype(BF16), wexp_ref[2 * i * kvw:2 * (i + 1) * kvw, :], preferred_element_type=F32)
        p = part if p is None else p + part
    pp = jnp.dot(pos_ref[...], wexp_ref[...], preferred_element_type=F32)
    posb = pp[0:1, :half] + pp[1:2, half:]
    sh_ref[0:nr, :] = p[:, half:]
    sh_ref[nr:nr + 8, :] = jnp.zeros((8, half), F32)
    h = p[:, :half] + sh_ref[1:nr + 1, :] + posb
    c = jnp.dot(jax.nn.gelu(h).astype(BF16), w2_ref[...], preferred_element_type=F32)
    if transposed:
        ct = c.T
        ones_row = jnp.where(lax.broadcasted_iota(jnp.int32, (VT_ROWS - HEAD_DIM, nr), 0) == 0, 1.0, 0.0)
        c = jnp.concatenate([piece for g in range(N_KV)
                             for piece in (ct[g * HEAD_DIM:(g + 1) * HEAD_DIM], ones_row)], axis=0)
    o_ref[0] = c.astype(BF16)


def _compress(tokens, wexp, w2bd, posrows, *, transposed, name):
    nb, t, k = tokens.shape
    nr = t // ROW_TOKENS
    half = wexp.shape[1] // 2
    oshape = (nb, N_KV * VT_ROWS, nr) if transposed else (nb, nr, half)
    return pl.pallas_call(
        functools.partial(_compress_body, transposed=transposed),
        grid=(nb,),
        in_specs=[pl.BlockSpec((1, t, k), lambda b: (b, 0, 0)),
                  _const_spec(wexp.shape), _const_spec(w2bd.shape), _const_spec(posrows.shape)],
        out_specs=pl.BlockSpec((1,) + oshape[1:], lambda b: (b, 0, 0)),
        out_shape=jax.ShapeDtypeStruct(oshape, BF16),
        scratch_shapes=[pltpu.VMEM((nr + 8, half), F32)],
        compiler_params=_cparams(1),
        name=name,
    )(tokens, wexp, w2bd, posrows)


def _compress_weights(w1, w2, cmp_pos):
    eye = jnp.eye(N_KV, dtype=F32)
    w_blk = w1.reshape(L_CMP, HEAD_DIM, HEAD_DIM)
    halves = []
    for part in range(L_CMP // ROW_TOKENS):
        wpart = w_blk[part * ROW_TOKENS:(part + 1) * ROW_TOKENS]
        halves.append(jnp.einsum("gh,lio->lgiho", eye, wpart).reshape(ROW_TOKENS * N_KV * HEAD_DIM, N_KV * HEAD_DIM))
    wexp = jnp.concatenate(halves, axis=1).astype(BF16)
    w2bd = jnp.einsum("gh,io->giho", eye, w2).reshape(N_KV * HEAD_DIM, N_KV * HEAD_DIM).astype(BF16)
    pos = jnp.broadcast_to(cmp_pos.reshape(L_CMP // ROW_TOKENS, ROW_TOKENS, 1, HEAD_DIM),
                           (L_CMP // ROW_TOKENS, ROW_TOKENS, N_KV, HEAD_DIM)).reshape(L_CMP // ROW_TOKENS, -1)
    posrows = jnp.zeros((8, pos.shape[1]), F32).at[:pos.shape[0]].set(pos).astype(BF16)
    return wexp, w2bd, posrows


def _bucket_starts():
    d = np.arange(0, 4 * MAX_DIST)
    exact = N_BUCKETS // 2
    nf = np.maximum(d, 1).astype(np.float32)
    large = exact + (np.log(nf / np.float32(exact)) / np.float32(math.log(MAX_DIST / exact))
                     * np.float32(N_BUCKETS - exact)).astype(np.int32)
    bucket = np.where(d < exact, d, np.minimum(large, N_BUCKETS - 1))
    assert bucket.max() == N_BUCKETS - 1 and np.all(np.diff(bucket) >= 0)
    return [int(np.argmax(bucket >= b)) for b in range(1, N_BUCKETS)]


def _bias_minus_far(rel_bias, dist, head_axis):
    tab = (rel_bias - rel_bias[:, N_BUCKETS - 1:]) * math.log2(math.e)
    hshape = (1,) * head_axis + (-1,) + (1,) * (dist.ndim - head_axis)
    dist = jnp.expand_dims(dist, head_axis)
    out = jnp.broadcast_to(tab[:, 0].reshape(hshape),
                           dist.shape[:head_axis] + (tab.shape[0],) + dist.shape[head_axis + 1:])
    for b, start in enumerate(_bucket_starts(), start=1):
        out = jnp.where(dist >= start, tab[:, b].reshape(hshape), out)
    return out


def _prompt_tables(rel_bias, t):
    h = rel_bias.shape[0]
    qo = jnp.arange(TQ, dtype=jnp.int32)
    n_near = WINDOW // L_SEL + 2 * (TQ // L_SEL)
    dd = (WINDOW // L_SEL) - jnp.arange(n_near, dtype=jnp.int32)
    ko = jnp.arange(L_SEL, dtype=jnp.int32)
    d = (L_SEL * dd[:, None, None] + qo[None, None, :] - ko[None, :, None]).reshape(n_near * L_SEL, TQ)
    ok = (d >= 0) & (d < WINDOW)
    near = jnp.where(ok[:, None, :], _bias_minus_far(rel_bias, d, 1), NEG)
    near = near.reshape(n_near * L_SEL, h * TQ)
    n_tiles = t // TQ
    per_tile = TQ // STRIDE
    e0 = per_tile * (n_tiles - 1)
    rows = e0 + t // STRIDE
    e = e0 - jnp.arange(rows, dtype=jnp.int32)
    dc = STRIDE * e[:, None] - (L_CMP - 1) + qo[None, :]
    cmp = jnp.where((dc >= 0)[:, None, :], _bias_minus_far(rel_bias, dc, 1), NEG)
    cmp = cmp.reshape(rows, h * TQ)
    return near.astype(F32), cmp.astype(F32), e0


def _cover_t(n_rows, n_c, n_blk):
    i = np.arange(n_rows)[None, :]
    j = np.arange(n_blk)[:, None]
    start = i * STRIDE
    end = start + L_CMP - 1
    cov = (start <= (j + 1) * L_SEL - 1) & (end >= j * L_SEL) & (i < n_c)
    return jnp.asarray(cov.astype(np.float32), BF16)


def _descending_rank(v, n_valid, n_live=None):
    rows, cols = v.shape
    sub = lax.broadcasted_iota(jnp.int32, (8, cols), 0)

    def count(rank, lo):
        mid = v[lo:lo + 8]
        for jp in range(lo, min(lo + 8, n_valid)):
            other = v[jp:jp + 1, :]
            parts = [jnp.where(other > mid, 1.0, jnp.where((other == mid) & (sub > jp - lo), 1.0, 0.0))]
            if lo > 0:
                parts.insert(0, jnp.where(other > v[:lo], 1.0, 0.0))
            if lo + 8 < rows:
                parts.append(jnp.where(other >= v[lo + 8:], 1.0, 0.0))
            rank = rank + jnp.concatenate(parts, axis=0)
        return rank

    rank = jnp.zeros((rows, cols), F32)
    for lo in range(0, n_valid, 8):
        if n_live is None:
            rank = count(rank, lo)
        else:
            rank = lax.cond(lo < n_live, functools.partial(count, lo=lo), lambda r: r, rank)
    return rank


def _nsa_prompt_body(q_ref, gt_ref, kc_ref, vct_ref, ks_ref, vst_ref, kw_ref, vwt_ref, tcmp_ref, near_ref, cov_ref,
                     o_ref, selb_ref, qaug_ref, qaug1_ref, s0_ref, s1_ref, sw_ref, sd_ref, *, e0, top_n):
    it = pl.program_id(1)
    hd = HEAD_DIM
    rep = q_ref.shape[2] // (N_KV * hd)
    gc = rep * TQ
    cols = N_KV * gc
    n_blk = cov_ref.shape[0]
    per = TQ // L_SEL

    def per_group(x):
        return jnp.concatenate([x[:, g * TQ:(g + 1) * TQ] for g in range(N_KV) for _ in range(rep)], axis=1)

    def pv(vt_ref, keys, p, rows=VT_ROWS):
        return jnp.concatenate(
            [jnp.dot(vt_ref[0, g * rows:(g + 1) * rows, keys], p[:, g * gc:(g + 1) * gc],
                     preferred_element_type=F32) for g in range(N_KV)], axis=1)

    qt = q_ref[0].astype(F32).T
    zero = jnp.zeros((hd, TQ), F32)
    tiles = []
    for g in range(N_KV):
        for r in range(rep):
            blk = qt[(g * rep + r) * hd:(g * rep + r + 1) * hd, :]
            tiles.append(jnp.concatenate([blk if gg == g else zero for gg in range(N_KV)], axis=0))
    qbd = jnp.concatenate(tiles, axis=1).astype(BF16)

    gate = [jnp.concatenate([gt_ref[0, g * GATE_ROWS + br * rep + r:g * GATE_ROWS + br * rep + r + 1, :]
                             for g in range(N_KV) for r in range(rep)], axis=1) for br in range(3)]

    n_ck = kc_ref.shape[1]
    sc = jnp.dot(kc_ref[0], qbd, preferred_element_type=F32)
    t0 = pl.multiple_of(e0 - (TQ // STRIDE) * it, 8)
    sc = sc + tcmp_ref[pl.ds(t0, n_ck), :]

    n_wc = WINDOW // TQ + 1
    offs = []
    for k in range(n_wc):
        kidx = it - (n_wc - 1) + k
        offs.append(pl.multiple_of(jnp.maximum(kidx, 0) * TQ, TQ))
        tab_off = pl.multiple_of(jnp.where(kidx >= 0, k * TQ, near_ref.shape[0] - TQ), TQ)
        sw_ref[k * TQ:(k + 1) * TQ, :] = (jnp.dot(kw_ref[0, pl.ds(offs[k], TQ), :], qbd, preferred_element_type=F32)
                                          + near_ref[pl.ds(tab_off, TQ), :])

    mc = jnp.max(sc, axis=0, keepdims=True)
    pc = jnp.exp2(sc - mc).astype(BF16)
    oc = pv(vct_ref, slice(None), pc)
    inv_lc = jnp.where(mc > 0.1 * NEG, 1.0 / oc[hd:hd + 1], 0.0)
    oc = oc[:hd] * inv_lc
    imp = jnp.dot(cov_ref[...], pc, preferred_element_type=F32) * inv_lc

    sw = sw_ref[...]
    pw = jnp.exp2(sw - jnp.max(sw, axis=0, keepdims=True)).astype(BF16)
    ow = None
    for k in range(n_wc):
        part = pv(vwt_ref, pl.ds(offs[k], TQ), pw[k * TQ:(k + 1) * TQ, :])
        ow = part if ow is None else ow + part

    diag0 = jnp.maximum(it - 1, 0)
    dk = 2 * TQ
    kvw = N_KV * hd
    offd = pl.multiple_of(diag0 * TQ, TQ)
    near_off = pl.multiple_of((WINDOW // L_SEL - per * (it - diag0)) * L_SEL, TQ)
    sd_ref[...] = (jnp.dot(ks_ref[0, pl.ds(offd, dk), 0:kvw], qbd, preferred_element_type=F32)
                   + near_ref[pl.ds(near_off, dk), :])

    v = []
    for g in range(N_KV):
        vg = imp[:, g * gc:g * gc + TQ]
        for r in range(1, rep):
            vg = vg + imp[:, g * gc + r * TQ:g * gc + (r + 1) * TQ]
        v.append(vg)
    v = jnp.concatenate(v, axis=1)
    j = lax.broadcasted_iota(jnp.int32, v.shape, 0)
    lane = lax.broadcasted_iota(jnp.int32, v.shape, 1)
    cur = per * it + (lane % TQ) // L_SEL
    forced = (j == 0) | (j == cur) | (j == cur - 1)
    v = jnp.where(forced, 1e9, jnp.where(j <= cur, v, -1e9))
    rank = _descending_rank(v, n_blk, n_live=per * (it + 1))
    selb = jnp.where((rank < top_n) & (j <= cur), 0.0, NEG)

    selb_ref[...] = per_group(jnp.where(j < per * diag0, selb, NEG))
    gk = FAR_CHUNKS * TQ
    for ref in (qaug_ref, qaug1_ref):
        ref[0:kvw, :] = qbd
        ref[kvw + 16:, :] = jnp.zeros((ref.shape[0] - kvw - 16, cols), BF16)

    def online_step(carry, s, keys):
        m, acc = carry
        m_new = jnp.maximum(m, jnp.max(s, axis=0, keepdims=True))
        p = jnp.exp2(s - m_new).astype(BF16)
        return m_new, jnp.exp2(m - m_new) * acc + pv(vst_ref, keys, p)

    def far_keys(cb):
        return pl.ds(pl.multiple_of(cb * gk, gk), gk)

    def far_scores(cb, q_ref2, s_out):
        sel = selb_ref[pl.ds(pl.multiple_of(cb * FAR_BLOCKS, FAR_BLOCKS), FAR_BLOCKS), :]
        q_ref2[kvw:kvw + 16, :] = jnp.concatenate([sel, jnp.zeros((16 - FAR_BLOCKS, cols), F32)], axis=0).astype(BF16)
        s_out[...] = jnp.dot(ks_ref[0, far_keys(cb), :], q_ref2[...], preferred_element_type=F32)

    def far_pair(jb, carry):
        far_scores(2 * jb, qaug_ref, s0_ref)
        far_scores(2 * jb + 1, qaug1_ref, s1_ref)
        carry = online_step(carry, s0_ref[...], far_keys(2 * jb))
        return online_step(carry, s1_ref[...], far_keys(2 * jb + 1))

    carry = (jnp.full((1, cols), NEG, F32), jnp.zeros((VT_ROWS, cols), F32))
    n_groups = (diag0 + FAR_CHUNKS - 1) // FAR_CHUNKS
    carry = lax.fori_loop(0, (n_groups + 1) // 2, far_pair, carry)

    sd = sd_ref[...]
    rows = []
    for k in range(dk // L_SEL):
        rk = jnp.max(jnp.where(j == per * diag0 + k, selb, NEG), axis=0, keepdims=True)
        rows.append(jnp.broadcast_to(per_group(rk), (L_SEL, cols)))
    sd = sd + jnp.concatenate(rows, axis=0)
    _, acc_s = online_step(carry, sd, pl.ds(offd, dk))

    out_t = (gate[0] * oc + (gate[1] / acc_s[hd:hd + 1]) * acc_s[:hd]
             + (gate[2] / ow[hd:hd + 1]) * ow[:hd])
    pieces = []
    for c in range(0, N_KV * rep, 2):
        pair = jnp.concatenate([out_t[:, c * TQ:(c + 1) * TQ], out_t[:, (c + 1) * TQ:(c + 2) * TQ]], axis=0)
        pieces.append(pair.T)
    o_ref[0] = jnp.concatenate(pieces, axis=1).astype(BF16)


def _nsa_prompt(q, gt, kc, vct, ks, vst, kw, vwt, tcmp, near, cov_t, *, e0, top_n):
    nb, t, nsa_w = q.shape
    kv_w = ks.shape[2]
    cols = (nsa_w // HEAD_DIM) * TQ
    per_b = lambda shape: pl.BlockSpec((1,) + shape, lambda b, i: (b, 0, 0))
    return pl.pallas_call(
        functools.partial(_nsa_prompt_body, e0=e0, top_n=top_n),
        grid=(nb, t // TQ),
        in_specs=[pl.BlockSpec((1, TQ, nsa_w), lambda b, i: (b, i, 0)),
                  pl.BlockSpec((1, N_KV * GATE_ROWS, TQ), lambda b, i: (b, 0, i)),
                  per_b(kc.shape[1:]), per_b(vct.shape[1:]),
                  per_b(ks.shape[1:]), per_b(vst.shape[1:]), per_b(kw.shape[1:]), per_b(vwt.shape[1:]),
                  _const_spec(tcmp.shape), _const_spec(near.shape), _const_spec(cov_t.shape)],
        out_specs=pl.BlockSpec((1, TQ, nsa_w), lambda b, i: (b, i, 0)),
        out_shape=jax.ShapeDtypeStruct((nb, t, nsa_w), BF16),
        scratch_shapes=[pltpu.VMEM((cov_t.shape[0], cols), F32),
                        pltpu.VMEM((ks.shape[2], cols), BF16), pltpu.VMEM((ks.shape[2], cols), BF16),
                        pltpu.VMEM((FAR_CHUNKS * TQ, cols), F32), pltpu.VMEM((FAR_CHUNKS * TQ, cols), F32),
                        pltpu.VMEM((WINDOW + TQ, cols), F32), pltpu.VMEM((2 * TQ, cols), F32)],
        compiler_params=_cparams(2),
        name="nsa_prompt",
    )(q, gt, kc, vct, ks, vst, kw, vwt, tcmp, near, cov_t)


def _dot_nt(a, b):
    return lax.dot_general(a, b, (((1,), (1,)), ((), ())), preferred_element_type=F32)


def _page_copies(pt_ref, pools, bufs, sems, batch, slot, dst):
    n_pages = pt_ref.shape[1]
    out = []
    for i, (pool, buf) in enumerate(zip(pools, bufs)):
        for p in range(n_pages):
            page = 0 if batch is None else pt_ref[batch, p]
            out.append(pltpu.make_async_copy(pool.at[page], dst(buf, slot, p), sems.at[i, slot]))
    return out


def _gather_step(pt_ref, pools, bufs, sems, dst):
    b = pl.program_id(0)
    slot = b % 2

    @pl.when(b == 0)
    def _():
        for cp in _page_copies(pt_ref, pools, bufs, sems, 0, 0, dst):
            cp.start()

    for cp in _page_copies(pt_ref, pools, bufs, sems, None, slot, dst):
        cp.wait()
    nxt = jnp.minimum(b + 1, pl.num_programs(0) - 1)
    for cp in _page_copies(pt_ref, pools, bufs, sems, nxt, 1 - slot, dst):
        cp.start()
    return slot


def _gather_drain(pt_ref, pools, bufs, sems, dst, slot):
    @pl.when(pl.program_id(0) == pl.num_programs(0) - 1)
    def _():
        for cp in _page_copies(pt_ref, pools, bufs, sems, None, 1 - slot, dst):
            cp.wait()


def _position_bias(pos_ref, wexp_ref):
    half = wexp_ref.shape[1] // 2
    pp = jnp.dot(pos_ref[...], wexp_ref[...], preferred_element_type=F32)
    return pp[0:1, :half] + pp[1:2, half:]


def _compress_pages(buf, slot, perm_ref, x_ref, wexp_ref, w2_ref, posb, sh_ref):
    n_pages = buf.shape[1]
    kvw = buf.shape[2]
    rpp = PAGE_SIZE // ROW_TOKENS
    nr = n_pages * rpp
    half = wexp_ref.shape[1] // 2

    for i in range(n_pages // 2):
        pair = buf[slot, 2 * i:2 * i + 2].reshape(2 * kvw, PAGE_SIZE).astype(BF16)
        xt = _dot_nt(perm_ref[...], pair)
        for pp in range(2):
            for l in range(ROW_TOKENS):
                row0 = (l // 2) * nr + (2 * i + pp) * rpp
                x_ref[row0:row0 + rpp, (l % 2) * kvw:(l % 2 + 1) * kvw] = (
                    xt[l * rpp:(l + 1) * rpp, pp * kvw:(pp + 1) * kvw])
    p = None
    for i in range(ROW_TOKENS // 2):
        part = jnp.dot(x_ref[i * nr:(i + 1) * nr, :].astype(BF16), wexp_ref[2 * i * kvw:2 * (i + 1) * kvw, :],
                       preferred_element_type=F32)
        p = part if p is None else p + part
    sh_ref[0:nr, :] = p[:, half:]
    sh_ref[nr:nr + 8, :] = jnp.zeros((8, half), F32)
    h = p[:, :half] + sh_ref[1:nr + 1, :] + posb
    return jnp.dot(jax.nn.gelu(h).astype(BF16), w2_ref[...], preferred_element_type=F32).astype(BF16)


def _sample_cmp_body(pt_ref, kpool_ref, vpool_ref, qbd_ref, tab_ref, perm_ref, wk_ref, w2k_ref, pk_ref, wv_ref, w2v_ref,
                     pv_ref, cov_ref, oc_ref, imp_ref, kbuf, vbuf, sems, xk_ref, xv_ref, shk_ref, shv_ref, posb_ref,
                     *, rep, t_new):
    page_dst = lambda buf, sl, p: buf.at[sl, p]
    slot = _gather_step(pt_ref, (kpool_ref, vpool_ref), (kbuf, vbuf), sems, page_dst)

    @pl.when(pl.program_id(0) == 0)
    def _():
        posb_ref[0:1, :] = _position_bias(pk_ref, wk_ref)
        posb_ref[1:2, :] = _position_bias(pv_ref, wv_ref)

    kc = _compress_pages(kbuf, slot, perm_ref, xk_ref, wk_ref, w2k_ref, posb_ref[0:1, :], shk_ref)
    vc = _compress_pages(vbuf, slot, perm_ref, xv_ref, wv_ref, w2v_ref, posb_ref[1:2, :], shv_ref)
    s = _dot_nt(qbd_ref[0], kc) + tab_ref[...]
    m = jnp.max(s, axis=1, keepdims=True)
    p = jnp.where(s > 0.1 * NEG, jnp.exp2(s - m), 0.0)
    l = jnp.sum(p, axis=1, keepdims=True)
    pn = (p * jnp.where(l > 0.0, 1.0 / l, 0.0)).astype(BF16)
    oc_ref[0] = jnp.dot(pn, vc, preferred_element_type=F32)
    imp = jnp.dot(pn, cov_ref[...], preferred_element_type=F32)
    per_g = rep * t_new
    parts = []
    for g in range(N_KV):
        acc = imp[g * per_g:g * per_g + t_new]
        for r in range(1, rep):
            acc = acc + imp[g * per_g + r * t_new:g * per_g + (r + 1) * t_new]
        parts.append(acc)
    imp_ref[0] = jnp.concatenate(parts, axis=0)
    _gather_drain(pt_ref, (kpool_ref, vpool_ref), (kbuf, vbuf), sems, page_dst, slot)


def _sample_cmp(page_table, kpool_t, vpool_t, qbd, tab, cmp_k, cmp_v, cov, *, rep, t_new):
    nb, n_pages = page_table.shape
    rpp = PAGE_SIZE // ROW_TOKENS
    n_rows = n_pages * rpp
    n_q = qbd.shape[1]
    kvw = N_KV * HEAD_DIM
    tok = np.arange(PAGE_SIZE)
    perm = jnp.asarray((tok[None, :] == (tok[:, None] % rpp) * ROW_TOKENS + tok[:, None] // rpp).astype(np.float32), BF16)
    const = lambda a: pl.BlockSpec(a.shape, lambda b, pt: (0,) * a.ndim, pipeline_mode=pl.Buffered(1))
    grid_spec = pltpu.PrefetchScalarGridSpec(
        num_scalar_prefetch=1,
        grid=(nb,),
        in_specs=[pl.BlockSpec(memory_space=pl.ANY), pl.BlockSpec(memory_space=pl.ANY),
                  pl.BlockSpec((1, n_q, kvw), lambda b, pt: (b, 0, 0)), const(tab), const(perm),
                  const(cmp_k[0]), const(cmp_k[1]), const(cmp_k[2]), const(cmp_v[0]), const(cmp_v[1]), const(cmp_v[2]),
                  const(cov)],
        out_specs=[pl.BlockSpec((1, n_q, kvw), lambda b, pt: (b, 0, 0)),
                   pl.BlockSpec((1, N_KV * t_new, cov.shape[1]), lambda b, pt: (b, 0, 0))],
        scratch_shapes=[pltpu.VMEM((2, n_pages, kvw, PAGE_SIZE), F32), pltpu.VMEM((2, n_pages, kvw, PAGE_SIZE), F32),
                        pltpu.SemaphoreType.DMA((2, 2)),
                        pltpu.VMEM((n_pages * PAGE_SIZE // 2, 2 * kvw), F32),
                        pltpu.VMEM((n_pages * PAGE_SIZE // 2, 2 * kvw), F32),
                        pltpu.VMEM((n_rows + 8, kvw), F32), pltpu.VMEM((n_rows + 8, kvw), F32),
                        pltpu.VMEM((8, kvw), F32)],
    )
    return pl.pallas_call(
        functools.partial(_sample_cmp_body, rep=rep, t_new=t_new),
        grid_spec=grid_spec,
        out_shape=[jax.ShapeDtypeStruct((nb, n_q, kvw), F32),
                   jax.ShapeDtypeStruct((nb, N_KV * t_new, cov.shape[1]), F32)],
        compiler_params=_cparams(1),
        name="sample_cmp",
    )(page_table, kpool_t, vpool_t, qbd, tab, perm, *cmp_k, *cmp_v, cov)


def _select_body(imp_ref, o_ref, *, top_n, n_blk, pos0, t_new):
    rows, cols = imp_ref.shape
    j = lax.broadcasted_iota(jnp.int32, (rows, cols), 0)
    col = pl.program_id(0) * cols + lax.broadcasted_iota(jnp.int32, (rows, cols), 1)
    cur = (pos0 + col % t_new) // L_SEL
    forced = (j == 0) | (j == cur) | (j == cur - 1)
    v = jnp.where(forced, 1e9, jnp.where(j <= cur, imp_ref[...], -1e9))
    v = jnp.where(j < n_blk, v, -3e38)
    rank = _descending_rank(v, n_blk)
    o_ref[...] = jnp.where((rank < top_n) & (j <= cur) & (j < n_blk), 0.0, NEG)


def _select_blocks(imp_t, *, top_n, n_blk, pos0, t_new):
    rows, cols = imp_t.shape
    tile = 4 * LANE
    return pl.pallas_call(
        functools.partial(_select_body, top_n=top_n, n_blk=n_blk, pos0=pos0, t_new=t_new),
        grid=(cols // tile,),
        in_specs=[pl.BlockSpec((rows, tile), lambda i: (0, i))],
        out_specs=pl.BlockSpec((rows, tile), lambda i: (0, i)),
        out_shape=jax.ShapeDtypeStruct((rows, cols), F32),
        compiler_params=_cparams(1),
        name="sample_select",
    )(imp_t)


def _softmax_pv(parts):
    m = None
    for s, _, _ in parts:
        mk = jnp.max(s, axis=1, keepdims=True)
        m = mk if m is None else jnp.maximum(m, mk)
    l = None
    o = None
    for s, v, channel_major in parts:
        p = jnp.exp2(s - m)
        lk = jnp.sum(p, axis=1, keepdims=True)
        pb = p.astype(BF16)
        ok = _dot_nt(pb, v) if channel_major else jnp.dot(pb, v, preferred_element_type=F32)
        l = lk if l is None else l + lk
        o = ok if o is None else o + ok
    return o / l


def _sample_slc_body(pt_ref, kpool_ref, vpool_ref, qbd_ref, selb_ref, gate_ref, oc_ref, ksn_ref, vsn_ref, kwn_ref,
                     vwn_ref, kwin_ref, vwin_ref, near_ref, newtab_ref, wtab_ref, expand_ref,
                     o_ref, kbuf, vbuf, sems, pad_ref, *, rep, t_new):
    b = pl.program_id(0)
    page_dst = lambda buf, sl, p: buf.at[sl, :, pl.ds(p * PAGE_SIZE, PAGE_SIZE)]
    slot = _gather_step(pt_ref, (kpool_ref, vpool_ref), (kbuf, vbuf), sems, page_dst)
    hd = HEAD_DIM
    kvw = N_KV * hd

    @pl.when(b == 0)
    def _():
        pad_ref[...] = jnp.zeros(pad_ref.shape, F32)

    news = []
    for i, ref in enumerate((ksn_ref, vsn_ref, kwn_ref, vwn_ref)):
        pad_ref[i, 0:t_new, :] = ref[0]
        news.append(pad_ref[i].astype(BF16))
    ks_new, vs_new, kw_new, vw_new = news
    qbd = qbd_ref[0]

    per_g = rep * t_new
    sel = selb_ref[0]
    sel_rows = jnp.concatenate([sel[g * t_new:(g + 1) * t_new] for g in range(N_KV) for _ in range(rep)], axis=0)
    n_past_blk = expand_ref.shape[0]
    s_past = jnp.dot(jnp.concatenate([qbd, sel_rows[:, :n_past_blk].astype(BF16)], axis=1),
                     jnp.concatenate([kbuf[slot].astype(BF16), expand_ref[...]], axis=0),
                     preferred_element_type=F32)
    n_past = s_past.shape[1]
    n_near = near_ref.shape[1]
    s_past = jnp.concatenate([s_past[:, :n_past - n_near], s_past[:, n_past - n_near:] + near_ref[...]], axis=1)
    s_new = _dot_nt(qbd, ks_new) + newtab_ref[0]
    s_win = jnp.dot(qbd, kwin_ref[0].astype(BF16), preferred_element_type=F32) + wtab_ref[...]
    s_wnew = _dot_nt(qbd, kw_new) + newtab_ref[1]
    o_s = _softmax_pv([(s_past, vbuf[slot].astype(BF16), True), (s_new, vs_new, False)])
    o_w = _softmax_pv([(s_win, vwin_ref[0].astype(BF16), True), (s_wnew, vw_new, False)])

    gate = gate_ref[0]
    comb = gate[:, 0:1] * oc_ref[0] + gate[:, 1:2] * o_s + gate[:, 2:3] * o_w
    row = lax.broadcasted_iota(jnp.int32, comb.shape, 0)
    o_ref[0] = jnp.where(row < per_g, comb, pltpu.roll(comb, hd, axis=1))
    _gather_drain(pt_ref, (kpool_ref, vpool_ref), (kbuf, vbuf), sems, page_dst, slot)


def _sample_slc(page_table, kpool, vpool, qbd, selb, gate_rows, oc, ks_new, vs_new, kw_new, vw_new, kwin, vwin,
                near, newtab, wtab, expand, *, rep, t_new):
    nb, n_pages = page_table.shape
    n_keys = n_pages * PAGE_SIZE
    n_q = qbd.shape[1]
    kvw = N_KV * HEAD_DIM
    const = lambda a: pl.BlockSpec(a.shape, lambda b, pt: (0,) * a.ndim, pipeline_mode=pl.Buffered(1))
    per_b = lambda a: pl.BlockSpec((1,) + a.shape[1:], lambda b, pt: (b,) + (0,) * (a.ndim - 1))
    grid_spec = pltpu.PrefetchScalarGridSpec(
        num_scalar_prefetch=1,
        grid=(nb,),
        in_specs=[pl.BlockSpec(memory_space=pl.ANY), pl.BlockSpec(memory_space=pl.ANY),
                  per_b(qbd), per_b(selb), per_b(gate_rows), per_b(oc),
                  per_b(ks_new), per_b(vs_new), per_b(kw_new), per_b(vw_new), per_b(kwin), per_b(vwin),
                  const(near), const(newtab), const(wtab), const(expand)],
        out_specs=pl.BlockSpec((1, n_q, kvw), lambda b, pt: (b, 0, 0)),
        scratch_shapes=[pltpu.VMEM((2, kvw, n_keys), F32), pltpu.VMEM((2, kvw, n_keys), F32),
                        pltpu.SemaphoreType.DMA((2, 2)), pltpu.VMEM((4, LANE, kvw), F32)],
    )
    return pl.pallas_call(
        functools.partial(_sample_slc_body, rep=rep, t_new=t_new),
        grid_spec=grid_spec,
        out_shape=jax.ShapeDtypeStruct((nb, n_q, kvw), F32),
        compiler_params=_cparams(1),
        name="sample_slc_win",
    )(page_table, kpool, vpool, qbd, selb, gate_rows, oc, ks_new, vs_new, kw_new, vw_new, kwin, vwin,
      near, newtab, wtab, expand)


def _sample_tables(rel_bias, pos0, t_new, n_rows, n_c, wbuf):
    h = rel_bias.shape[0]
    qpos = pos0 + jnp.arange(t_new, dtype=jnp.int32)

    def table(kpos, ok_extra=None, upper=None):
        d = qpos[:, None] - kpos[None, :]
        ok = d >= 0
        if upper is not None:
            ok = ok & (d < upper)
        if ok_extra is not None:
            ok = ok & ok_extra[None, :]
        tab = jnp.where(ok[None], _bias_minus_far(rel_bias, d, 0), NEG)
        return tab.reshape(h * t_new, kpos.shape[0]).astype(F32)

    c = jnp.arange(n_rows, dtype=jnp.int32)
    cmp = table(c * STRIDE + (L_CMP - 1), ok_extra=c < n_c)
    near = table(pos0 - PAGE_SIZE + jnp.arange(PAGE_SIZE, dtype=jnp.int32))
    lane = jnp.arange(LANE, dtype=jnp.int32)
    new_slc = table(pos0 + lane, ok_extra=lane < t_new)
    new_win = table(pos0 + lane, ok_extra=lane < t_new, upper=WINDOW)
    kpos_w = pos0 - wbuf + jnp.arange(wbuf, dtype=jnp.int32)
    win = table(kpos_w, ok_extra=kpos_w >= 0, upper=WINDOW)
    return cmp, near, jnp.stack([new_slc, new_win]), win


def _layer_weights(p, n_heads):
    d = p["w_in"].shape[0]
    kv_end = d + 6 * N_KV * HEAD_DIM
    src = _gate_column_source(n_heads)
    gate_cols = jnp.where(jnp.asarray(src >= 0)[None, :], p["w_in"][:, kv_end + np.maximum(src, 0)], 0.0)
    ln = jnp.stack([p["ln_gain"][0], p["ln_bias"][0], p["ln_gain"][1], p["ln_bias"][1],
                    p["ln_gain"][2], p["ln_bias"][2]])
    return dict(
        wi1=p["w_ffn1_in"].astype(BF16), wo1=p["w_ffn1_out"].astype(BF16),
        wi2=p["w_ffn2_in"].astype(BF16), wo2=p["w_ffn2_out"].astype(BF16),
        wp=jnp.concatenate([p["w_in"][:, :kv_end], gate_cols], axis=1).astype(BF16),
        wout=p["w_out"].astype(BF16),
        wpool=p["w_pool"].astype(BF16), pool_scale=p["pool_scale"],
        ln1=ln[0:2], ln23=ln[2:6],
        cmp_k=_compress_weights(p["w_phi_k1"], p["w_phi_k2"], p["cmp_pos"]),
        cmp_v=_compress_weights(p["w_phi_v1"], p["w_phi_v2"], p["cmp_pos"]),
    )


PROMPT_TOKENS = 512


def _prompt_layer(x, mod, w, rel_bias, alpha):
    nb, t, d = x.shape
    assert t % TQ == 0 and t % PROMPT_TOKENS == 0
    (x1, u_pool, q, k_c, v_c, k_s, v_s, k_w, v_w, ks_bf, vs_t, kw_bf, vw_t, gates_t) = _ffn_in(
        x, mod, w["wi1"], w["wo1"], w["wp"], w["ln1"], alpha=alpha, nbk=1, tt=PROMPT_TOKENS, for_prompt=True)
    pool_out = _pool_mix(u_pool, None, w["wpool"], w["pool_scale"], pos0=0, nbk=1, tt=PROMPT_TOKENS,
                         name="pool_prompt")
    kc_cmp = _compress(k_c, *w["cmp_k"], transposed=False, name="compress_k_prompt")
    vc_cmp_t = _compress(v_c, *w["cmp_v"], transposed=True, name="compress_v_prompt")
    n_c = (t - L_CMP) // STRIDE + 1
    n_blk = t // L_SEL
    near, tcmp, e0 = _prompt_tables(rel_bias, t)
    nsa = _nsa_prompt(q, gates_t, kc_cmp, vc_cmp_t, ks_bf, vs_t, kw_bf, vw_t, tcmp, near,
                      _cover_t(t // ROW_TOKENS, n_c, n_blk), e0=e0, top_n=min(TOP_N, n_blk))
    y = _ffn_out(pool_out, nsa, x1, mod, w["wout"], w["wi2"], w["wo2"], w["ln23"], alpha=alpha,
                 nbk=1, tt=PROMPT_TOKENS, name="outproj_ffn2_prompt")
    keep = min(WINDOW, t)
    heads = lambda a: a.reshape(nb, -1, N_KV, HEAD_DIM)
    states = (heads(k_c), heads(v_c), heads(k_s), heads(v_s), heads(k_w[:, t - keep:]), heads(v_w[:, t - keep:]),
              u_pool[:, t - POOL_STATE:])
    return y, states


SAMPLE_BATCHES = 32


def _sample_layer(x, mod, w, rel_bias, alpha, page_table, pools, kwin, vwin, pool_state):
    nb, t, d = x.shape
    hd = HEAD_DIM
    n_heads = rel_bias.shape[0]
    rep = n_heads // N_KV
    kvw = N_KV * hd
    pos0 = page_table.shape[1] * PAGE_SIZE
    n_c = (pos0 + t - L_CMP) // STRIDE + 1
    assert (n_c - 1) * STRIDE + L_CMP <= pos0 and pos0 % PAGE_SIZE == 0 and nb % SAMPLE_BATCHES == 0
    n_rows = pos0 // ROW_TOKENS
    n_blk = -(-(pos0 + t) // L_SEL)
    assert n_blk - 1 == pos0 // L_SEL
    blk_pad = -(-n_blk // LANE) * LANE
    wbuf = kwin.shape[1]

    x1, u_pool, q, k_c, v_c, k_s, v_s, k_w, v_w, gates = _ffn_in(
        x, mod, w["wi1"], w["wo1"], w["wp"], w["ln1"], alpha=alpha, nbk=SAMPLE_BATCHES, tt=t, for_prompt=False)
    pw = u_pool.shape[-1]
    prev = jnp.concatenate([jnp.zeros((nb, PREV_ROWS - POOL_STATE, pw), F32), pool_state], axis=1)
    pool_out = _pool_mix(u_pool, prev, w["wpool"], w["pool_scale"], pos0=pos0, nbk=SAMPLE_BATCHES, tt=t,
                         name="pool_sample")

    qh = q.reshape(nb, t, N_KV, rep, hd).transpose(0, 2, 3, 1, 4)
    own = jnp.eye(N_KV, dtype=q.dtype)[None, :, None, None, :, None]
    qbd = (qh[:, :, :, :, None, :] * own).reshape(nb, N_KV * rep * t, kvw)
    gate_rows = gates[:, :, :N_KV * GATE_ROWS].reshape(nb, t, N_KV, GATE_ROWS)[..., :3 * rep]
    gate_rows = gate_rows.reshape(nb, t, N_KV, 3, rep).transpose(0, 2, 4, 1, 3).reshape(nb, N_KV * rep * t, 3)
    gate_rows = jnp.pad(gate_rows, ((0, 0), (0, 0), (0, LANE - 3)))

    tab_c, near, newtab, wtab = _sample_tables(rel_bias, pos0, t, n_rows, n_c, wbuf)
    cov = np.zeros((n_rows, blk_pad), np.float32)
    ci = np.arange(n_rows)[:, None]
    bj = np.arange(blk_pad)[None, :]
    cov[:] = ((ci * STRIDE <= (bj + 1) * L_SEL - 1) & (ci * STRIDE + L_CMP - 1 >= bj * L_SEL)
              & (ci < n_c) & (bj < n_blk))
    channel_major = lambda a: a.transpose(0, 2, 3, 1).reshape(a.shape[0], kvw, a.shape[1])
    oc, imp = _sample_cmp(page_table, channel_major(pools[0]), channel_major(pools[1]), qbd, tab_c, w["cmp_k"],
                          w["cmp_v"], jnp.asarray(cov, BF16), rep=rep, t_new=t)

    sel_rows = -(-n_blk // 8) * 8
    imp_t = imp.transpose(2, 0, 1).reshape(blk_pad, nb * N_KV * t)[:sel_rows]
    selb_t = _select_blocks(imp_t, top_n=min(TOP_N, n_blk), n_blk=n_blk, pos0=pos0, t_new=t)
    selb_t = jnp.pad(selb_t, ((0, blk_pad - sel_rows), (0, 0)), constant_values=NEG)
    selb = selb_t.reshape(blk_pad, nb, N_KV * t).transpose(1, 2, 0)

    key_blk = lax.broadcasted_iota(jnp.int32, (pos0 // L_SEL, pos0), 1) // L_SEL
    expand = (key_blk == lax.broadcasted_iota(jnp.int32, (pos0 // L_SEL, pos0), 0)).astype(BF16)
    kvf = lambda a: a.reshape(a.shape[0], a.shape[1], kvw)
    o = _sample_slc(page_table, channel_major(pools[2]), channel_major(pools[3]), qbd, selb, gate_rows, oc,
                    k_s, v_s, k_w, v_w, channel_major(kwin), channel_major(vwin), near, newtab, wtab, expand,
                    rep=rep, t_new=t)
    nsa = o[:, :, :hd].reshape(nb, N_KV, rep, t, hd).transpose(0, 3, 1, 2, 4).reshape(nb, t, n_heads * hd).astype(BF16)

    y = _ffn_out(pool_out, nsa, x1, mod, w["wout"], w["wi2"], w["wo2"], w["ln23"], alpha=alpha,
                 nbk=SAMPLE_BATCHES, tt=t, name="outproj_ffn2_sample")
    keep = min(WINDOW, pos0 + t)
    heads = lambda a: a.reshape(nb, -1, N_KV, HEAD_DIM)
    kw_ext = jnp.concatenate([kvf(kwin), k_w], axis=1)
    vw_ext = jnp.concatenate([kvf(vwin), v_w], axis=1)
    pool_ext = jnp.concatenate([pool_state, u_pool], axis=1)
    states = (heads(k_c), heads(v_c), heads(k_s), heads(v_s), heads(kw_ext[:, kw_ext.shape[1] - keep:]),
              heads(vw_ext[:, vw_ext.shape[1] - keep:]), pool_ext[:, pool_ext.shape[1] - POOL_STATE:])
    return y, states


def kernel(x_prompt, x_sample, c_prompt, c_sample, cache_k_cmp, cache_v_cmp, cache_k_slc, cache_v_slc, page_table,
           state_k_win, state_v_win, state_pool, w_ada, b_ada, ln_gain, ln_bias, w_ffn1_in, w_ffn1_out, w_ffn2_in,
           w_ffn2_out, w_in, w_phi_k1, w_phi_k2, w_phi_v1, w_phi_v2, cmp_pos, w_pool, pool_scale, w_out, rel_bias):
    depth = w_ada.shape[0]
    alpha = (2.0 * depth) ** 0.25
    n_heads = rel_bias.shape[0]
    nb_p, nb_s = x_prompt.shape[0], x_sample.shape[0]
    d = x_prompt.shape[-1]
    xp, xs = x_prompt, x_sample
    c_all = jnp.concatenate([c_prompt, c_sample], axis=0)
    p_states, s_states = [], []
    for l in range(depth):
        p = dict(w_ada=w_ada[l], b_ada=b_ada[l], ln_gain=ln_gain[l], ln_bias=ln_bias[l], w_ffn1_in=w_ffn1_in[l],
                 w_ffn1_out=w_ffn1_out[l], w_ffn2_in=w_ffn2_in[l], w_ffn2_out=w_ffn2_out[l], w_in=w_in[l],
                 w_phi_k1=w_phi_k1[l], w_phi_k2=w_phi_k2[l], w_phi_v1=w_phi_v1[l], w_phi_v2=w_phi_v2[l],
                 cmp_pos=cmp_pos[l], w_pool=w_pool[l], pool_scale=pool_scale[l], w_out=w_out[l])
        w = _layer_weights(p, n_heads)
        mod = _ada(c_all, p["w_ada"], p["b_ada"]).reshape(nb_p + nb_s, 9, d)
        xp, sp = _prompt_layer(xp, mod[:nb_p], w, rel_bias, alpha)
        xs, ss = _sample_layer(xs, mod[nb_p:], w, rel_bias, alpha, page_table,
                               (cache_k_cmp[l], cache_v_cmp[l], cache_k_slc[l], cache_v_slc[l]),
                               state_k_win[l], state_v_win[l], state_pool[l])
        p_states.append(sp)
        s_states.append(ss)
    stack = lambda states: tuple(jnp.stack(a) for a in zip(*states))
    return (xp, xs) + stack(p_states) + stack(s_states)
```

```python
import functools
import math

import numpy as np
import jax
import jax.numpy as jnp
from jax import lax
from jax.experimental import pallas as pl
from jax.experimental.pallas import tpu as pltpu

F32 = jnp.float32
BF16 = jnp.bfloat16

HEAD_DIM = 64
N_KV = 2
L_CMP = 32
STRIDE = 16
L_SEL = 64
TOP_N = 16
WINDOW = 512
N_BUCKETS = 32
MAX_DIST = 128
POOL_WINDOWS = (2, 4, 8, 16)
POOL_STATE = max(POOL_WINDOWS) - 1
LN_EPS = 1e-5
NEG = -1e30
PAGE_SIZE = 128

Q_SCALE = HEAD_DIM ** -0.5 * math.log2(math.e)
TQ = 128
FAR_CHUNKS = 4
FAR_BLOCKS = FAR_CHUNKS * TQ // L_SEL
VT_ROWS = HEAD_DIM + 16

LANE = 128
MXU_COLS = 256
FF_CHUNK = MXU_COLS
VMEM_LIMIT = 56 * 1024 * 1024


def _cparams(n_axes):
    return pltpu.CompilerParams(dimension_semantics=("arbitrary",) * n_axes,
                                vmem_limit_bytes=VMEM_LIMIT)


def _const_spec(shape):
    nd = len(shape)
    return pl.BlockSpec(shape, lambda *_: (0,) * nd, pipeline_mode=pl.Buffered(1))


def _layer_norm(y, gain, bias):
    mu = jnp.mean(y, axis=-1, keepdims=True)
    yc = y - mu
    var = jnp.mean(yc * yc, axis=-1, keepdims=True)
    return yc * lax.rsqrt(var + LN_EPS) * gain + bias


def _swiglu(u_bf, wi_ref, wo_ref):
    d_ff = wo_ref.shape[0]
    acc = None
    for c in range(d_ff // FF_CHUNK):
        lo = c * FF_CHUNK
        a = jnp.dot(u_bf, wi_ref[:, lo:lo + FF_CHUNK], preferred_element_type=F32)
        b = jnp.dot(u_bf, wi_ref[:, d_ff + lo:d_ff + lo + FF_CHUNK], preferred_element_type=F32)
        hid = (a * jax.nn.sigmoid(a) * b).astype(BF16)
        part = jnp.dot(hid, wo_ref[lo:lo + FF_CHUNK, :], preferred_element_type=F32)
        acc = part if acc is None else acc + part
    return acc


def _ada_body(c_ref, w_ref, b_ref, o_ref):
    c = c_ref[...]
    act = (c * jax.nn.sigmoid(c)).astype(BF16)
    o_ref[...] = jnp.dot(act, w_ref[...].astype(BF16), preferred_element_type=F32) + b_ref[...]


def _ada(c_all, w_ada, b_ada):
    nb, d = c_all.shape
    n_out = w_ada.shape[1]
    cols = 9 * LANE
    return pl.pallas_call(
        _ada_body,
        grid=(n_out // cols,),
        in_specs=[pl.BlockSpec((nb, d), lambda j: (0, 0)),
                  pl.BlockSpec((d, cols), lambda j: (0, j)),
                  pl.BlockSpec((1, cols), lambda j: (0, j))],
        out_specs=pl.BlockSpec((nb, cols), lambda j: (0, j)),
        out_shape=jax.ShapeDtypeStruct((nb, n_out), F32),
        compiler_params=_cparams(1),
        name="ada",
    )(c_all, w_ada, b_ada.reshape(1, n_out))


KV_NAMES = ("k_c", "v_c", "k_s", "v_s", "k_w", "v_w")


def _ffn_in_body(x_ref, mod_ref, wi_ref, wo_ref, wp_ref, ln_ref, *out_refs, alpha, pool_w, nsa_w, kv_w, for_prompt):
    nbk, tt, d = x_ref.shape
    n = nbk * tt
    x = x_ref[...]
    mod = mod_ref[...]
    u = (x * (1.0 + mod[:, 1:2, :]) + mod[:, 0:1, :]).reshape(n, d).astype(BF16)
    h = _swiglu(u, wi_ref, wo_ref).reshape(nbk, tt, d)
    x1 = _layer_norm(alpha * x + 0.5 * mod[:, 2:3, :] * h, ln_ref[0:1, :], ln_ref[1:2, :])
    u1 = (x1 * (1.0 + mod[:, 4:5, :]) + mod[:, 3:4, :]).reshape(n, d).astype(BF16)
    proj = jnp.dot(u1, wp_ref[...], preferred_element_type=F32)

    x1_ref, upool_ref, q_ref = out_refs[:3]
    kv_refs = out_refs[3:9]
    x1_ref[...] = x1
    upool_ref[...] = proj[:, :pool_w].reshape(nbk, tt, pool_w)
    q_ref[...] = (proj[:, pool_w:pool_w + nsa_w] * Q_SCALE).astype(BF16).reshape(nbk, tt, nsa_w)
    off = pool_w + nsa_w
    kv = []
    for i in range(6):
        blk = proj[:, off + i * kv_w:off + (i + 1) * kv_w]
        kv.append(blk)
        kv_refs[i][...] = blk.reshape(nbk, tt, kv_w)
    gates = jax.nn.sigmoid(proj[:, off + 6 * kv_w:off + 6 * kv_w + LANE])
    if for_prompt:
        ksb_ref, vst_ref, kwb_ref, vwt_ref, gt_ref = out_refs[9:]
        row = lax.broadcasted_iota(jnp.int32, (n, kv_w), 0)
        lane = lax.broadcasted_iota(jnp.int32, (n, kv_w), 1)
        onehot = jnp.where((row // L_SEL) % FAR_BLOCKS == lane, 1.0, 0.0)
        ones_row = jnp.where(lax.broadcasted_iota(jnp.int32, (VT_ROWS - HEAD_DIM, n), 0) == 0, 1.0, 0.0)

        def values_t(x):
            xt = x.T
            return jnp.concatenate([piece for g in range(N_KV)
                                    for piece in (xt[g * HEAD_DIM:(g + 1) * HEAD_DIM], ones_row)], axis=0).astype(BF16)

        ksb_ref[0] = jnp.concatenate([kv[2], onehot], axis=1).astype(BF16)
        vst_ref[0] = values_t(kv[3])
        kwb_ref[0] = kv[4].astype(BF16)
        vwt_ref[0] = values_t(kv[5])
        gt_ref[0] = gates.T[:gt_ref.shape[1], :]
    else:
        out_refs[9][...] = gates.reshape(nbk, tt, LANE)


def _ffn_in(x, mod, wi, wo, wp, ln, *, alpha, nbk, tt, for_prompt):
    nb, t, d = x.shape
    pool_w = d // 2
    nsa_w = d - pool_w
    kv_w = N_KV * HEAD_DIM
    grid = (nb // nbk, t // tt)
    tok = lambda w: pl.BlockSpec((nbk, tt, w), lambda i, j: (i, j, 0))
    out_specs = [tok(d), tok(pool_w), tok(nsa_w)] + [tok(kv_w)] * 6
    out_shape = ([jax.ShapeDtypeStruct((nb, t, d), F32), jax.ShapeDtypeStruct((nb, t, pool_w), F32),
                  jax.ShapeDtypeStruct((nb, t, nsa_w), BF16)]
                 + [jax.ShapeDtypeStruct((nb, t, kv_w), F32)] * 6)
    if for_prompt:
        assert nbk == 1 and tt % (FAR_BLOCKS * L_SEL) == 0
        tr = lambda rows: pl.BlockSpec((1, rows, tt), lambda i, j: (i, 0, j))
        vt_rows = N_KV * VT_ROWS
        out_specs += [tok(2 * kv_w), tr(vt_rows), tok(kv_w), tr(vt_rows), tr(N_KV * GATE_ROWS)]
        out_shape += [jax.ShapeDtypeStruct((nb, t, 2 * kv_w), BF16), jax.ShapeDtypeStruct((nb, vt_rows, t), BF16),
                      jax.ShapeDtypeStruct((nb, t, kv_w), BF16), jax.ShapeDtypeStruct((nb, vt_rows, t), BF16),
                      jax.ShapeDtypeStruct((nb, N_KV * GATE_ROWS, t), F32)]
    else:
        out_specs += [tok(LANE)]
        out_shape += [jax.ShapeDtypeStruct((nb, t, LANE), F32)]
    body = functools.partial(_ffn_in_body, alpha=alpha, pool_w=pool_w, nsa_w=nsa_w, kv_w=kv_w,
                             for_prompt=for_prompt)
    return pl.pallas_call(
        body,
        grid=grid,
        in_specs=[tok(d),
                  pl.BlockSpec((nbk, mod.shape[1], d), lambda i, j: (i, 0, 0)),
                  _const_spec(wi.shape), _const_spec(wo.shape), _const_spec(wp.shape), _const_spec(ln.shape)],
        out_specs=out_specs,
        out_shape=out_shape,
        compiler_params=_cparams(2),
        name="ffn1_inproj_prompt" if for_prompt else "ffn1_inproj_sample",
    )(x, mod, wi, wo, wp, ln)


GATE_ROWS = 16


def _gate_column_source(n_heads):
    rep = n_heads // N_KV
    src = np.full((LANE,), -1, np.int32)
    for g in range(N_KV):
        for br in range(3):
            for r in range(rep):
                src[g * GATE_ROWS + br * rep + r] = br * n_heads + g * rep + r
    return src


def _ffn_out_body(pool_ref, nsa_ref, x1_ref, mod_ref, wout_ref, wi_ref, wo_ref, ln_ref, o_ref, *, alpha):
    nbk, tt, d = x1_ref.shape
    n = nbk * tt
    pw = pool_ref.shape[-1]
    mod = mod_ref[...]
    x1 = x1_ref[...]
    mix = (jnp.dot(pool_ref[...].reshape(n, pw), wout_ref[:pw, :], preferred_element_type=F32)
           + jnp.dot(nsa_ref[...].reshape(n, d - pw), wout_ref[pw:, :], preferred_element_type=F32))
    x2 = _layer_norm(alpha * x1 + mod[:, 5:6, :] * mix.reshape(nbk, tt, d), ln_ref[0:1, :], ln_ref[1:2, :])
    u = (x2 * (1.0 + mod[:, 7:8, :]) + mod[:, 6:7, :]).reshape(n, d).astype(BF16)
    h = _swiglu(u, wi_ref, wo_ref).reshape(nbk, tt, d)
    o_ref[...] = _layer_norm(alpha * x2 + 0.5 * mod[:, 8:9, :] * h, ln_ref[2:3, :], ln_ref[3:4, :])


def _ffn_out(pool_out, nsa_out, x1, mod, wout, wi, wo, ln, *, alpha, nbk, tt, name):
    nb, t, d = x1.shape
    tok = lambda w: pl.BlockSpec((nbk, tt, w), lambda i, j: (i, j, 0))
    return pl.pallas_call(
        functools.partial(_ffn_out_body, alpha=alpha),
        grid=(nb // nbk, t // tt),
        in_specs=[tok(pool_out.shape[-1]), tok(nsa_out.shape[-1]), tok(d),
                  pl.BlockSpec((nbk, mod.shape[1], d), lambda i, j: (i, 0, 0)),
                  _const_spec(wout.shape), _const_spec(wi.shape), _const_spec(wo.shape), _const_spec(ln.shape)],
        out_specs=tok(d),
        out_shape=jax.ShapeDtypeStruct((nb, t, d), F32),
        compiler_params=_cparams(2),
        name=name,
    )(pool_out, nsa_out, x1, mod, wout, wi, wo, ln)


PREV_ROWS = 16


def _pool_body(u_ref, prev_ref, w_ref, scale_ref, o_ref, ext_ref, *, pos0, zero_first):
    nbk, tt, width = u_ref.shape
    gw = width // len(POOL_WINDOWS)
    j = pl.program_id(1)
    u = u_ref[...]
    prev = prev_ref[...]
    if zero_first:
        prev = jnp.where(j == 0, 0.0, prev)
    ext_ref[:, 0:PREV_ROWS, :] = prev
    ext_ref[:, PREV_ROWS:, :] = u
    pos = pos0 + j * tt + lax.broadcasted_iota(jnp.int32, (1, tt, 1), 1)
    for g, w in enumerate(POOL_WINDOWS):
        lo = g * gw
        tot = None
        for k in range(w):
            part = ext_ref[:, PREV_ROWS - k:PREV_ROWS - k + tt, lo:lo + gw]
            tot = part if tot is None else tot + part
        cnt = jnp.minimum(pos + 1, w).astype(F32)
        pooled = (tot / cnt - u[:, :, lo:lo + gw]).astype(BF16).reshape(nbk * tt, gw)
        mixed = jnp.dot(pooled, w_ref[g], preferred_element_type=F32) * scale_ref[:, lo:lo + gw]
        o_ref[:, :, lo:lo + gw] = mixed.astype(BF16).reshape(nbk, tt, gw)


def _pool_mix(u_pool, prev, w_pool_bf, pool_scale, *, pos0, nbk, tt, name):
    nb, t, width = u_pool.shape
    tok = pl.BlockSpec((nbk, tt, width), lambda i, j: (i, j, 0))
    if prev is None:
        assert nbk == 1 and tt % PREV_ROWS == 0 and pos0 == 0
        step = tt // PREV_ROWS
        prev_arr = u_pool
        prev_spec = pl.BlockSpec((1, PREV_ROWS, width), lambda i, j: (i, jnp.maximum(j * step - 1, 0), 0))
    else:
        assert t == tt
        prev_arr = prev
        prev_spec = pl.BlockSpec((nbk, PREV_ROWS, width), lambda i, j: (i, 0, 0))
    return pl.pallas_call(
        functools.partial(_pool_body, pos0=pos0, zero_first=prev is None),
        grid=(nb // nbk, t // tt),
        in_specs=[tok, prev_spec, _const_spec(w_pool_bf.shape), _const_spec((1, width))],
        out_specs=tok,
        out_shape=jax.ShapeDtypeStruct((nb, t, width), BF16),
        scratch_shapes=[pltpu.VMEM((nbk, PREV_ROWS + tt, width), F32)],
        compiler_params=_cparams(2),
        name=name,
    )(u_pool, prev_arr, w_pool_bf, pool_scale.reshape(1, width))


ROW_TOKENS = STRIDE


def _compress_body(k_ref, wexp_ref, w2_ref, pos_ref, o_ref, sh_ref, *, transposed):
    kvw = k_ref.shape[2]
    nr = k_ref.shape[1] // ROW_TOKENS
    half = wexp_ref.shape[1] // 2
    p = None
    for i in range(ROW_TOKENS // 2):
        lhs = jnp.concatenate([k_ref[0, pl.ds(2 * i + e, nr, stride=ROW_TOKENS), :] for e in range(2)], axis=1)
        part = jnp.dot(lhs.astype(BF16), wexp_ref[2 * i * kvw:2 * (i + 1) * kvw, :], preferred_element_type=F32)
        p = part if p is None else p + part
    pp = jnp.dot(pos_ref[...], wexp_ref[...], preferred_element_type=F32)
    posb = pp[0:1, :half] + pp[1:2, half:]
    sh_ref[0:nr, :] = p[:, half:]
    sh_ref[nr:nr + 8, :] = jnp.zeros((8, half), F32)
    h = p[:, :half] + sh_ref[1:nr + 1, :] + posb
    c = jnp.dot(jax.nn.gelu(h).astype(BF16), w2_ref[...], preferred_element_type=F32)
    if transposed:
        ct = c.T
        ones_row = jnp.where(lax.broadcasted_iota(jnp.int32, (VT_ROWS - HEAD_DIM, nr), 0) == 0, 1.0, 0.0)
        c = jnp.concatenate([piece for g in range(N_KV)
                             for piece in (ct[g * HEAD_DIM:(g + 1) * HEAD_DIM], ones_row)], axis=0)
    o_ref[0] = c.astype(BF16)


def _compress(tokens, wexp, w2bd, posrows, *, transposed, name):
    nb, t, k = tokens.shape
    nr = t // ROW_TOKENS
    half = wexp.shape[1] // 2
    oshape = (nb, N_KV * VT_ROWS, nr) if transposed else (nb, nr, half)
    return pl.pallas_call(
        functools.partial(_compress_body, transposed=transposed),
        grid=(nb,),
        in_specs=[pl.BlockSpec((1, t, k), lambda b: (b, 0, 0)),
                  _const_spec(wexp.shape), _const_spec(w2bd.shape), _const_spec(posrows.shape)],
        out_specs=pl.BlockSpec((1,) + oshape[1:], lambda b: (b, 0, 0)),
        out_shape=jax.ShapeDtypeStruct(oshape, BF16),
        scratch_shapes=[pltpu.VMEM((nr + 8, half), F32)],
        compiler_params=_cparams(1),
        name=name,
    )(tokens, wexp, w2bd, posrows)


def _compress_weights(w1, w2, cmp_pos):
    eye = jnp.eye(N_KV, dtype=F32)
    w_blk = w1.reshape(L_CMP, HEAD_DIM, HEAD_DIM)
    halves = []
    for part in range(L_CMP // ROW_TOKENS):
        wpart = w_blk[part * ROW_TOKENS:(part + 1) * ROW_TOKENS]
        halves.append(jnp.einsum("gh,lio->lgiho", eye, wpart).reshape(ROW_TOKENS * N_KV * HEAD_DIM, N_KV * HEAD_DIM))
    wexp = jnp.concatenate(halves, axis=1).astype(BF16)
    w2bd = jnp.einsum("gh,io->giho", eye, w2).reshape(N_KV * HEAD_DIM, N_KV * HEAD_DIM).astype(BF16)
    pos = jnp.broadcast_to(cmp_pos.reshape(L_CMP // ROW_TOKENS, ROW_TOKENS, 1, HEAD_DIM),
                           (L_CMP // ROW_TOKENS, ROW_TOKENS, N_KV, HEAD_DIM)).reshape(L_CMP // ROW_TOKENS, -1)
    posrows = jnp.zeros((8, pos.shape[1]), F32).at[:pos.shape[0]].set(pos).astype(BF16)
    return wexp, w2bd, posrows


def _bucket_starts():
    d = np.arange(0, 4 * MAX_DIST)
    exact = N_BUCKETS // 2
    nf = np.maximum(d, 1).astype(np.float32)
    large = exact + (np.log(nf / np.float32(exact)) / np.float32(math.log(MAX_DIST / exact))
                     * np.float32(N_BUCKETS - exact)).astype(np.int32)
    bucket = np.where(d < exact, d, np.minimum(large, N_BUCKETS - 1))
    assert bucket.max() == N_BUCKETS - 1 and np.all(np.diff(bucket) >= 0)
    return [int(np.argmax(bucket >= b)) for b in range(1, N_BUCKETS)]


def _bias_minus_far(rel_bias, dist, head_axis):
    tab = (rel_bias - rel_bias[:, N_BUCKETS - 1:]) * math.log2(math.e)
    hshape = (1,) * head_axis + (-1,) + (1,) * (dist.ndim - head_axis)
    dist = jnp.expand_dims(dist, head_axis)
    out = jnp.broadcast_to(tab[:, 0].reshape(hshape),
                           dist.shape[:head_axis] + (tab.shape[0],) + dist.shape[head_axis + 1:])
    for b, start in enumerate(_bucket_starts(), start=1):
        out = jnp.where(dist >= start, tab[:, b].reshape(hshape), out)
    return out


def _prompt_tables(rel_bias, t):
    h = rel_bias.shape[0]
    qo = jnp.arange(TQ, dtype=jnp.int32)
    n_near = WINDOW // L_SEL + 2 * (TQ // L_SEL)
    dd = (WINDOW // L_SEL) - jnp.arange(n_near, dtype=jnp.int32)
    ko = jnp.arange(L_SEL, dtype=jnp.int32)
    d = (L_SEL * dd[:, None, None] + qo[None, None, :] - ko[None, :, None]).reshape(n_near * L_SEL, TQ)
    ok = (d >= 0) & (d < WINDOW)
    near = jnp.where(ok[:, None, :], _bias_minus_far(rel_bias, d, 1), NEG)
    near = near.reshape(n_near * L_SEL, h * TQ)
    n_tiles = t // TQ
    per_tile = TQ // STRIDE
    e0 = per_tile * (n_tiles - 1)
    rows = e0 + t // STRIDE
    e = e0 - jnp.arange(rows, dtype=jnp.int32)
    dc = STRIDE * e[:, None] - (L_CMP - 1) + qo[None, :]
    cmp = jnp.where((dc >= 0)[:, None, :], _bias_minus_far(rel_bias, dc, 1), NEG)
    cmp = cmp.reshape(rows, h * TQ)
    return near.astype(F32), cmp.astype(F32), e0


def _cover_t(n_rows, n_c, n_blk):
    i = np.arange(n_rows)[None, :]
    j = np.arange(n_blk)[:, None]
    start = i * STRIDE
    end = start + L_CMP - 1
    cov = (start <= (j + 1) * L_SEL - 1) & (end >= j * L_SEL) & (i < n_c)
    return jnp.asarray(cov.astype(np.float32), BF16)


def _descending_rank(v, n_valid, n_live=None):
    rows, cols = v.shape
    sub = lax.broadcasted_iota(jnp.int32, (8, cols), 0)

    def count(rank, lo):
        mid = v[lo:lo + 8]
        for jp in range(lo, min(lo + 8, n_valid)):
            other = v[jp:jp + 1, :]
            parts = [jnp.where(other > mid, 1.0, jnp.where((other == mid) & (sub > jp - lo), 1.0, 0.0))]
            if lo > 0:
                parts.insert(0, jnp.where(other > v[:lo], 1.0, 0.0))
            if lo + 8 < rows:
                parts.append(jnp.where(other >= v[lo + 8:], 1.0, 0.0))
            rank = rank + jnp.concatenate(parts, axis=0)
        return rank

    rank = jnp.zeros((rows, cols), F32)
    for lo in range(0, n_valid, 8):
        if n_live is None:
            rank = count(rank, lo)
        else:
            rank = lax.cond(lo < n_live, functools.partial(count, lo=lo), lambda r: r, rank)
    return rank


def _nsa_prompt_body(q_ref, gt_ref, kc_ref, vct_ref, ks_ref, vst_ref, kw_ref, vwt_ref, tcmp_ref, near_ref, cov_ref,
                     o_ref, selb_ref, qaug_ref, qaug1_ref, s0_ref, s1_ref, sw_ref, sd_ref, *, e0, top_n):
    it = pl.program_id(1)
    hd = HEAD_DIM
    rep = q_ref.shape[2] // (N_KV * hd)
    gc = rep * TQ
    cols = N_KV * gc
    n_blk = cov_ref.shape[0]
    per = TQ // L_SEL

    def per_group(x):
        return jnp.concatenate([x[:, g * TQ:(g + 1) * TQ] for g in range(N_KV) for _ in range(rep)], axis=1)

    def pv(vt_ref, keys, p, rows=VT_ROWS):
        return jnp.concatenate(
            [jnp.dot(vt_ref[0, g * rows:(g + 1) * rows, keys], p[:, g * gc:(g + 1) * gc],
                     preferred_element_type=F32) for g in range(N_KV)], axis=1)

    qt = q_ref[0].astype(F32).T
    zero = jnp.zeros((hd, TQ), F32)
    tiles = []
    for g in range(N_KV):
        for r in range(rep):
            blk = qt[(g * rep + r) * hd:(g * rep + r + 1) * hd, :]
            tiles.append(jnp.concatenate([blk if gg == g else zero for gg in range(N_KV)], axis=0))
    qbd = jnp.concatenate(tiles, axis=1).astype(BF16)

    gate = [jnp.concatenate([gt_ref[0, g * GATE_ROWS + br * rep + r:g * GATE_ROWS + br * rep + r + 1, :]
                             for g in range(N_KV) for r in range(rep)], axis=1) for br in range(3)]

    n_ck = kc_ref.shape[1]
    sc = jnp.dot(kc_ref[0], qbd, preferred_element_type=F32)
    t0 = pl.multiple_of(e0 - (TQ // STRIDE) * it, 8)
    sc = sc + tcmp_ref[pl.ds(t0, n_ck), :]

    n_wc = WINDOW // TQ + 1
    offs = []
    for k in range(n_wc):
        kidx = it - (n_wc - 1) + k
        offs.append(pl.multiple_of(jnp.maximum(kidx, 0) * TQ, TQ))
        tab_off = pl.multiple_of(jnp.where(kidx >= 0, k * TQ, near_ref.shape[0] - TQ), TQ)
        sw_ref[k * TQ:(k + 1) * TQ, :] = (jnp.dot(kw_ref[0, pl.ds(offs[k], TQ), :], qbd, preferred_element_type=F32)
                                          + near_ref[pl.ds(tab_off, TQ), :])

    mc = jnp.max(sc, axis=0, keepdims=True)
    pc = jnp.exp2(sc - mc).astype(BF16)
    oc = pv(vct_ref, slice(None), pc)
    inv_lc = jnp.where(mc > 0.1 * NEG, 1.0 / oc[hd:hd + 1], 0.0)
    oc = oc[:hd] * inv_lc
    imp = jnp.dot(cov_ref[...], pc, preferred_element_type=F32) * inv_lc

    sw = sw_ref[...]
    pw = jnp.exp2(sw - jnp.max(sw, axis=0, keepdims=True)).astype(BF16)
    ow = None
    for k in range(n_wc):
        part = pv(vwt_ref, pl.ds(offs[k], TQ), pw[k * TQ:(k + 1) * TQ, :])
        ow = part if ow is None else ow + part

    diag0 = jnp.maximum(it - 1, 0)
    dk = 2 * TQ
    kvw = N_KV * hd
    offd = pl.multiple_of(diag0 * TQ, TQ)
    near_off = pl.multiple_of((WINDOW // L_SEL - per * (it - diag0)) * L_SEL, TQ)
    sd_ref[...] = (jnp.dot(ks_ref[0, pl.ds(offd, dk), 0:kvw], qbd, preferred_element_type=F32)
                   + near_ref[pl.ds(near_off, dk), :])

    v = []
    for g in range(N_KV):
        vg = imp[:, g * gc:g * gc + TQ]
        for r in range(1, rep):
            vg = vg + imp[:, g * gc + r * TQ:g * gc + (r + 1) * TQ]
        v.append(vg)
    v = jnp.concatenate(v, axis=1)
    j = lax.broadcasted_iota(jnp.int32, v.shape, 0)
    lane = lax.broadcasted_iota(jnp.int32, v.shape, 1)
    cur = per * it + (lane % TQ) // L_SEL
    forced = (j == 0) | (j == cur) | (j == cur - 1)
    v = jnp.where(forced, 1e9, jnp.where(j <= cur, v, -1e9))
    rank = _descending_rank(v, n_blk, n_live=per * (it + 1))
    selb = jnp.where((rank < top_n) & (j <= cur), 0.0, NEG)

    selb_ref[...] = per_group(jnp.where(j < per * diag0, selb, NEG))
    gk = FAR_CHUNKS * TQ
    for ref in (qaug_ref, qaug1_ref):
        ref[0:kvw, :] = qbd
        ref[kvw + 16:, :] = jnp.zeros((ref.shape[0] - kvw - 16, cols), BF16)

    def online_step(carry, s, keys):
        m, acc = carry
        m_new = jnp.maximum(m, jnp.max(s, axis=0, keepdims=True))
        p = jnp.exp2(s - m_new).astype(BF16)
        return m_new, jnp.exp2(m - m_new) * acc + pv(vst_ref, keys, p)

    def far_keys(cb):
        return pl.ds(pl.multiple_of(cb * gk, gk), gk)

    def far_scores(cb, q_ref2, s_out):
        sel = selb_ref[pl.ds(pl.multiple_of(cb * FAR_BLOCKS, FAR_BLOCKS), FAR_BLOCKS), :]
        q_ref2[kvw:kvw + 16, :] = jnp.concatenate([sel, jnp.zeros((16 - FAR_BLOCKS, cols), F32)], axis=0).astype(BF16)
        s_out[...] = jnp.dot(ks_ref[0, far_keys(cb), :], q_ref2[...], preferred_element_type=F32)

    def far_pair(jb, carry):
        far_scores(2 * jb, qaug_ref, s0_ref)
        far_scores(2 * jb + 1, qaug1_ref, s1_ref)
        carry = online_step(carry, s0_ref[...], far_keys(2 * jb))
        return online_step(carry, s1_ref[...], far_keys(2 * jb + 1))

    carry = (jnp.full((1, cols), NEG, F32), jnp.zeros((VT_ROWS, cols), F32))
    n_groups = (diag0 + FAR_CHUNKS - 1) // FAR_CHUNKS
    carry = lax.fori_loop(0, (n_groups + 1) // 2, far_pair, carry)

    sd = sd_ref[...]
    rows = []
    for k in range(dk // L_SEL):
        rk = jnp.max(jnp.where(j == per * diag0 + k, selb, NEG), axis=0, keepdims=True)
        rows.append(jnp.broadcast_to(per_group(rk), (L_SEL, cols)))
    sd = sd + jnp.concatenate(rows, axis=0)
    _, acc_s = online_step(carry, sd, pl.ds(offd, dk))

    out_t = (gate[0] * oc + (gate[1] / acc_s[hd:hd + 1]) * acc_s[:hd]
             + (gate[2] / ow[hd:hd + 1]) * ow[:hd])
    pieces = []
    for c in range(0, N_KV * rep, 2):
        pair = jnp.concatenate([out_t[:, c * TQ:(c + 1) * TQ], out_t[:, (c + 1) * TQ:(c + 2) * TQ]], axis=0)
        pieces.append(pair.T)
    o_ref[0] = jnp.concatenate(pieces, axis=1).astype(BF16)


def _nsa_prompt(q, gt, kc, vct, ks, vst, kw, vwt, tcmp, near, cov_t, *, e0, top_n):
    nb, t, nsa_w = q.shape
    kv_w = ks.shape[2]
    cols = (nsa_w // HEAD_DIM) * TQ
    per_b = lambda shape: pl.BlockSpec((1,) + shape, lambda b, i: (b, 0, 0))
    return pl.pallas_call(
        functools.partial(_nsa_prompt_body, e0=e0, top_n=top_n),
        grid=(nb, t // TQ),
        in_specs=[pl.BlockSpec((1, TQ, nsa_w), lambda b, i: (b, i, 0)),
                  pl.BlockSpec((1, N_KV * GATE_ROWS, TQ), lambda b, i: (b, 0, i)),
                  per_b(kc.shape[1:]), per_b(vct.shape[1:]),
                  per_b(ks.shape[1:]), per_b(vst.shape[1:]), per_b(kw.shape[1:]), per_b(vwt.shape[1:]),
                  _const_spec(tcmp.shape), _const_spec(near.shape), _const_spec(cov_t.shape)],
        out_specs=pl.BlockSpec((1, TQ, nsa_w), lambda b, i: (b, i, 0)),
        out_shape=jax.ShapeDtypeStruct((nb, t, nsa_w), BF16),
        scratch_shapes=[pltpu.VMEM((cov_t.shape[0], cols), F32),
                        pltpu.VMEM((ks.shape[2], cols), BF16), pltpu.VMEM((ks.shape[2], cols), BF16),
                        pltpu.VMEM((FAR_CHUNKS * TQ, cols), F32), pltpu.VMEM((FAR_CHUNKS * TQ, cols), F32),
                        pltpu.VMEM((WINDOW + TQ, cols), F32), pltpu.VMEM((2 * TQ, cols), F32)],
        compiler_params=_cparams(2),
        name="nsa_prompt",
    )(q, gt, kc, vct, ks, vst, kw, vwt, tcmp, near, cov_t)


def _dot_nt(a, b):
    return lax.dot_general(a, b, (((1,), (1,)), ((), ())), preferred_element_type=F32)


def _page_copies(pt_ref, pools, bufs, sems, batch, slot, dst):
    n_pages = pt_ref.shape[1]
    out = []
    for i, (pool, buf) in enumerate(zip(pools, bufs)):
        for p in range(n_pages):
            page = 0 if batch is None else pt_ref[batch, p]
            out.append(pltpu.make_async_copy(pool.at[page], dst(buf, slot, p), sems.at[i, slot]))
    return out


def _gather_step(pt_ref, pools, bufs, sems, dst):
    b = pl.program_id(0)
    slot = b % 2

    @pl.when(b == 0)
    def _():
        for cp in _page_copies(pt_ref, pools, bufs, sems, 0, 0, dst):
            cp.start()

    for cp in _page_copies(pt_ref, pools, bufs, sems, None, slot, dst):
        cp.wait()
    nxt = jnp.minimum(b + 1, pl.num_programs(0) - 1)
    for n, cp in enumerate(_page_copies(pt_ref, pools, bufs, sems, nxt, 1 - slot, dst)):
        cp.start(priority=n % 2)
    return slot


def _gather_drain(pt_ref, pools, bufs, sems, dst, slot):
    @pl.when(pl.program_id(0) == pl.num_programs(0) - 1)
    def _():
        for cp in _page_copies(pt_ref, pools, bufs, sems, None, 1 - slot, dst):
            cp.wait()


def _position_bias(pos_ref, wexp_ref):
    half = wexp_ref.shape[1] // 2
    pp = jnp.dot(pos_ref[...], wexp_ref[...], preferred_element_type=F32)
    return pp[0:1, :half] + pp[1:2, half:]


def _compress_pages(buf, slot, perm_ref, x_ref, wexp_ref, w2_ref, posb, sh_ref):
    n_pages = buf.shape[1]
    kvw = buf.shape[2]
    rpp = PAGE_SIZE // ROW_TOKENS
    nr = n_pages * rpp
    half = wexp_ref.shape[1] // 2

    for i in range(n_pages // 2):
        pair = buf[slot, 2 * i:2 * i + 2].reshape(2 * kvw, PAGE_SIZE).astype(BF16)
        xt = _dot_nt(perm_ref[...], pair)
        for pp in range(2):
            for l in range(ROW_TOKENS):
                row0 = (l // 2) * nr + (2 * i + pp) * rpp
                x_ref[row0:row0 + rpp, (l % 2) * kvw:(l % 2 + 1) * kvw] = (
                    xt[l * rpp:(l + 1) * rpp, pp * kvw:(pp + 1) * kvw])
    p = None
    for i in range(ROW_TOKENS // 2):
        part = jnp.dot(x_ref[i * nr:(i + 1) * nr, :].astype(BF16), wexp_ref[2 * i * kvw:2 * (i + 1) * kvw, :],
                       preferred_element_type=F32)
        p = part if p is None else p + part
    sh_ref[0:nr, :] = p[:, half:]
    sh_ref[nr:nr + 8, :] = jnp.zeros((8, half), F32)
    h = p[:, :half] + sh_ref[1:nr + 1, :] + posb
    return jnp.dot(jax.nn.gelu(h).astype(BF16), w2_ref[...], preferred_element_type=F32).astype(BF16)


def _sample_cmp_body(pt_ref, kpool_ref, vpool_ref, qbd_ref, tab_ref, perm_ref, wk_ref, w2k_ref, pk_ref, wv_ref, w2v_ref,
                     pv_ref, cov_ref, oc_ref, imp_ref, kbuf, vbuf, sems, xk_ref, xv_ref, shk_ref, shv_ref, posb_ref,
                     *, rep, t_new):
    page_dst = lambda buf, sl, p: buf.at[sl, p]
    slot = _gather_step(pt_ref, (kpool_ref, vpool_ref), (kbuf, vbuf), sems, page_dst)

    @pl.when(pl.program_id(0) == 0)
    def _():
        posb_ref[0:1, :] = _position_bias(pk_ref, wk_ref)
        posb_ref[1:2, :] = _position_bias(pv_ref, wv_ref)

    kc = _compress_pages(kbuf, slot, perm_ref, xk_ref, wk_ref, w2k_ref, posb_ref[0:1, :], shk_ref)
    vc = _compress_pages(vbuf, slot, perm_ref, xv_ref, wv_ref, w2v_ref, posb_ref[1:2, :], shv_ref)
    s = _dot_nt(qbd_ref[0], kc) + tab_ref[...]
    m = jnp.max(s, axis=1, keepdims=True)
    p = jnp.where(s > 0.1 * NEG, jnp.exp2(s - m), 0.0)
    l = jnp.sum(p, axis=1, keepdims=True)
    pn = (p * jnp.where(l > 0.0, 1.0 / l, 0.0)).astype(BF16)
    oc_ref[0] = jnp.dot(pn, vc, preferred_element_type=F32)
    imp = jnp.dot(pn, cov_ref[...], preferred_element_type=F32)
    per_g = rep * t_new
    parts = []
    for g in range(N_KV):
        acc = imp[g * per_g:g * per_g + t_new]
        for r in range(1, rep):
            acc = acc + imp[g * per_g + r * t_new:g * per_g + (r + 1) * t_new]
        parts.append(acc)
    imp_ref[0] = jnp.concatenate(parts, axis=0)
    _gather_drain(pt_ref, (kpool_ref, vpool_ref), (kbuf, vbuf), sems, page_dst, slot)


def _sample_cmp(page_table, kpool_t, vpool_t, qbd, tab, cmp_k, cmp_v, cov, *, rep, t_new):
    nb, n_pages = page_table.shape
    rpp = PAGE_SIZE // ROW_TOKENS
    n_rows = n_pages * rpp
    n_q = qbd.shape[1]
    kvw = N_KV * HEAD_DIM
    tok = np.arange(PAGE_SIZE)
    perm = jnp.asarray((tok[None, :] == (tok[:, None] % rpp) * ROW_TOKENS + tok[:, None] // rpp).astype(np.float32), BF16)
    const = lambda a: pl.BlockSpec(a.shape, lambda b, pt: (0,) * a.ndim, pipeline_mode=pl.Buffered(1))
    grid_spec = pltpu.PrefetchScalarGridSpec(
        num_scalar_prefetch=1,
        grid=(nb,),
        in_specs=[pl.BlockSpec(memory_space=pl.ANY), pl.BlockSpec(memory_space=pl.ANY),
                  pl.BlockSpec((1, n_q, kvw), lambda b, pt: (b, 0, 0)), const(tab), const(perm),
                  const(cmp_k[0]), const(cmp_k[1]), const(cmp_k[2]), const(cmp_v[0]), const(cmp_v[1]), const(cmp_v[2]),
                  const(cov)],
        out_specs=[pl.BlockSpec((1, n_q, kvw), lambda b, pt: (b, 0, 0)),
                   pl.BlockSpec((1, N_KV * t_new, cov.shape[1]), lambda b, pt: (b, 0, 0))],
        scratch_shapes=[pltpu.VMEM((2, n_pages, kvw, PAGE_SIZE), F32), pltpu.VMEM((2, n_pages, kvw, PAGE_SIZE), F32),
                        pltpu.SemaphoreType.DMA((2, 2)),
                        pltpu.VMEM((n_pages * PAGE_SIZE // 2, 2 * kvw), F32),
                        pltpu.VMEM((n_pages * PAGE_SIZE // 2, 2 * kvw), F32),
                        pltpu.VMEM((n_rows + 8, kvw), F32), pltpu.VMEM((n_rows + 8, kvw), F32),
                        pltpu.VMEM((8, kvw), F32)],
    )
    return pl.pallas_call(
        functools.partial(_sample_cmp_body, rep=rep, t_new=t_new),
        grid_spec=grid_spec,
        out_shape=[jax.ShapeDtypeStruct((nb, n_q, kvw), F32),
                   jax.ShapeDtypeStruct((nb, N_KV * t_new, cov.shape[1]), F32)],
        compiler_params=_cparams(1),
        name="sample_cmp",
    )(page_table, kpool_t, vpool_t, qbd, tab, perm, *cmp_k, *cmp_v, cov)


def _select_body(imp_ref, o_ref, *, top_n, n_blk, pos0, t_new):
    rows, cols = imp_ref.shape
    j = lax.broadcasted_iota(jnp.int32, (rows, cols), 0)
    col = pl.program_id(0) * cols + lax.broadcasted_iota(jnp.int32, (rows, cols), 1)
    cur = (pos0 + col % t_new) // L_SEL
    forced = (j == 0) | (j == cur) | (j == cur - 1)
    v = jnp.where(forced, 1e9, jnp.where(j <= cur, imp_ref[...], -1e9))
    v = jnp.where(j < n_blk, v, -3e38)
    rank = _descending_rank(v, n_blk)
    o_ref[...] = jnp.where((rank < top_n) & (j <= cur) & (j < n_blk), 0.0, NEG)


def _select_blocks(imp_t, *, top_n, n_blk, pos0, t_new):
    rows, cols = imp_t.shape
    tile = 4 * LANE
    return pl.pallas_call(
        functools.partial(_select_body, top_n=top_n, n_blk=n_blk, pos0=pos0, t_new=t_new),
        grid=(cols // tile,),
        in_specs=[pl.BlockSpec((rows, tile), lambda i: (0, i))],
        out_specs=pl.BlockSpec((rows, tile), lambda i: (0, i)),
        out_shape=jax.ShapeDtypeStruct((rows, cols), F32),
        compiler_params=_cparams(1),
        name="sample_select",
    )(imp_t)


def _softmax_pv(parts):
    m = None
    for s, _, _ in parts:
        mk = jnp.max(s, axis=1, keepdims=True)
        m = mk if m is None else jnp.maximum(m, mk)
    l = None
    o = None
    for s, v, channel_major in parts:
        p = jnp.exp2(s - m)
        lk = jnp.sum(p, axis=1, keepdims=True)
        pb = p.astype(BF16)
        ok = _dot_nt(pb, v) if channel_major else jnp.dot(pb, v, preferred_element_type=F32)
        l = lk if l is None else l + lk
        o = ok if o is None else o + ok
    return o / l


def _sample_slc_body(pt_ref, kpool_ref, vpool_ref, qbd_ref, selb_ref, gate_ref, oc_ref, ksn_ref, vsn_ref, kwn_ref,
                     vwn_ref, kwin_ref, vwin_ref, near_ref, newtab_ref, wtab_ref, expand_ref,
                     o_ref, kbuf, vbuf, sems, pad_ref, *, rep, t_new):
    b = pl.program_id(0)
    page_dst = lambda buf, sl, p: buf.at[sl, :, pl.ds(p * PAGE_SIZE, PAGE_SIZE)]
    slot = _gather_step(pt_ref, (kpool_ref, vpool_ref), (kbuf, vbuf), sems, page_dst)
    hd = HEAD_DIM
    kvw = N_KV * hd

    @pl.when(b == 0)
    def _():
        pad_ref[...] = jnp.zeros(pad_ref.shape, F32)

    news = []
    for i, ref in enumerate((ksn_ref, vsn_ref, kwn_ref, vwn_ref)):
        pad_ref[i, 0:t_new, :] = ref[0]
        news.append(pad_ref[i].astype(BF16))
    ks_new, vs_new, kw_new, vw_new = news
    qbd = qbd_ref[0]

    per_g = rep * t_new
    sel = selb_ref[0]
    sel_rows = jnp.concatenate([sel[g * t_new:(g + 1) * t_new] for g in range(N_KV) for _ in range(rep)], axis=0)
    n_past_blk = expand_ref.shape[0]
    s_past = jnp.dot(jnp.concatenate([qbd, sel_rows[:, :n_past_blk].astype(BF16)], axis=1),
                     jnp.concatenate([kbuf[slot].astype(BF16), expand_ref[...]], axis=0),
                     preferred_element_type=F32)
    n_past = s_past.shape[1]
    n_near = near_ref.shape[1]
    s_past = jnp.concatenate([s_past[:, :n_past - n_near], s_past[:, n_past - n_near:] + near_ref[...]], axis=1)
    s_new = _dot_nt(qbd, ks_new) + newtab_ref[0]
    s_win = jnp.dot(qbd, kwin_ref[0].astype(BF16), preferred_element_type=F32) + wtab_ref[...]
    s_wnew = _dot_nt(qbd, kw_new) + newtab_ref[1]
    o_s = _softmax_pv([(s_past, vbuf[slot].astype(BF16), True), (s_new, vs_new, False)])
    o_w = _softmax_pv([(s_win, vwin_ref[0].astype(BF16), True), (s_wnew, vw_new, False)])

    gate = gate_ref[0]
    comb = gate[:, 0:1] * oc_ref[0] + gate[:, 1:2] * o_s + gate[:, 2:3] * o_w
    row = lax.broadcasted_iota(jnp.int32, comb.shape, 0)
    o_ref[0] = jnp.where(row < per_g, comb, pltpu.roll(comb, hd, axis=1))
    _gather_drain(pt_ref, (kpool_ref, vpool_ref), (kbuf, vbuf), sems, page_dst, slot)


def _sample_slc(page_table, kpool, vpool, qbd, selb, gate_rows, oc, ks_new, vs_new, kw_new, vw_new, kwin, vwin,
                near, newtab, wtab, expand, *, rep, t_new):
    nb, n_pages = page_table.shape
    n_keys = n_pages * PAGE_SIZE
    n_q = qbd.shape[1]
    kvw = N_KV * HEAD_DIM
    const = lambda a: pl.BlockSpec(a.shape, lambda b, pt: (0,) * a.ndim, pipeline_mode=pl.Buffered(1))
    per_b = lambda a: pl.BlockSpec((1,) + a.shape[1:], lambda b, pt: (b,) + (0,) * (a.ndim - 1))
    grid_spec = pltpu.PrefetchScalarGridSpec(
        num_scalar_prefetch=1,
        grid=(nb,),
        in_specs=[pl.BlockSpec(memory_space=pl.ANY), pl.BlockSpec(memory_space=pl.ANY),
                  per_b(qbd), per_b(selb), per_b(gate_rows), per_b(oc),
                  per_b(ks_new), per_b(vs_new), per_b(kw_new), per_b(vw_new), per_b(kwin), per_b(vwin),
                  const(near), const(newtab), const(wtab), const(expand)],
        out_specs=pl.BlockSpec((1, n_q, kvw), lambda b, pt: (b, 0, 0)),
        scratch_shapes=[pltpu.VMEM((2, kvw, n_keys), F32), pltpu.VMEM((2, kvw, n_keys), F32),
                        pltpu.SemaphoreType.DMA((2, 2)), pltpu.VMEM((4, LANE, kvw), F32)],
    )
    return pl.pallas_call(
        functools.partial(_sample_slc_body, rep=rep, t_new=t_new),
        grid_spec=grid_spec,
        out_shape=jax.ShapeDtypeStruct((nb, n_q, kvw), F32),
        compiler_params=_cparams(1),
        name="sample_slc_win",
    )(page_table, kpool, vpool, qbd, selb, gate_rows, oc, ks_new, vs_new, kw_new, vw_new, kwin, vwin,
      near, newtab, wtab, expand)


def _sample_tables(rel_bias, pos0, t_new, n_rows, n_c, wbuf):
    h = rel_bias.shape[0]
    qpos = pos0 + jnp.arange(t_new, dtype=jnp.int32)

    def table(kpos, ok_extra=None, upper=None):
        d = qpos[:, None] - kpos[None, :]
        ok = d >= 0
        if upper is not None:
            ok = ok & (d < upper)
        if ok_extra is not None:
            ok = ok & ok_extra[None, :]
        tab = jnp.where(ok[None], _bias_minus_far(rel_bias, d, 0), NEG)
        return tab.reshape(h * t_new, kpos.shape[0]).astype(F32)

    c = jnp.arange(n_rows, dtype=jnp.int32)
    cmp = table(c * STRIDE + (L_CMP - 1), ok_extra=c < n_c)
    near = table(pos0 - PAGE_SIZE + jnp.arange(PAGE_SIZE, dtype=jnp.int32))
    lane = jnp.arange(LANE, dtype=jnp.int32)
    new_slc = table(pos0 + lane, ok_extra=lane < t_new)
    new_win = table(pos0 + lane, ok_extra=lane < t_new, upper=WINDOW)
    kpos_w = pos0 - wbuf + jnp.arange(wbuf, dtype=jnp.int32)
    win = table(kpos_w, ok_extra=kpos_w >= 0, upper=WINDOW)
    return cmp, near, jnp.stack([new_slc, new_win]), win


def _layer_weights(p, n_heads):
    d = p["w_in"].shape[0]
    kv_end = d + 6 * N_KV * HEAD_DIM
    src = _gate_column_source(n_heads)
    gate_cols = jnp.where(jnp.asarray(src >= 0)[None, :], p["w_in"][:, kv_end + np.maximum(src, 0)], 0.0)
    ln = jnp.stack([p["ln_gain"][0], p["ln_bias"][0], p["ln_gain"][1], p["ln_bias"][1],
                    p["ln_gain"][2], p["ln_bias"][2]])
    return dict(
        wi1=p["w_ffn1_in"].astype(BF16), wo1=p["w_ffn1_out"].astype(BF16),
        wi2=p["w_ffn2_in"].astype(BF16), wo2=p["w_ffn2_out"].astype(BF16),
        wp=jnp.concatenate([p["w_in"][:, :kv_end], gate_cols], axis=1).astype(BF16),
        wout=p["w_out"].astype(BF16),
        wpool=p["w_pool"].astype(BF16), pool_scale=p["pool_scale"],
        ln1=ln[0:2], ln23=ln[2:6],
        cmp_k=_compress_weights(p["w_phi_k1"], p["w_phi_k2"], p["cmp_pos"]),
        cmp_v=_compress_weights(p["w_phi_v1"], p["w_phi_v2"], p["cmp_pos"]),
    )


PROMPT_TOKENS = 512


def _prompt_layer(x, mod, w, rel_bias, alpha):
    nb, t, d = x.shape
    assert t % TQ == 0 and t % PROMPT_TOKENS == 0
    (x1, u_pool, q, k_c, v_c, k_s, v_s, k_w, v_w, ks_bf, vs_t, kw_bf, vw_t, gates_t) = _ffn_in(
        x, mod, w["wi1"], w["wo1"], w["wp"], w["ln1"], alpha=alpha, nbk=1, tt=PROMPT_TOKENS, for_prompt=True)
    pool_out = _pool_mix(u_pool, None, w["wpool"], w["pool_scale"], pos0=0, nbk=1, tt=PROMPT_TOKENS,
                         name="pool_prompt")
    kc_cmp = _compress(k_c, *w["cmp_k"], transposed=False, name="compress_k_prompt")
    vc_cmp_t = _compress(v_c, *w["cmp_v"], transposed=True, name="compress_v_prompt")
    n_c = (t - L_CMP) // STRIDE + 1
    n_blk = t // L_SEL
    near, tcmp, e0 = _prompt_tables(rel_bias, t)
    nsa = _nsa_prompt(q, gates_t, kc_cmp, vc_cmp_t, ks_bf, vs_t, kw_bf, vw_t, tcmp, near,
                      _cover_t(t // ROW_TOKENS, n_c, n_blk), e0=e0, top_n=min(TOP_N, n_blk))
    y = _ffn_out(pool_out, nsa, x1, mod, w["wout"], w["wi2"], w["wo2"], w["ln23"], alpha=alpha,
                 nbk=1, tt=PROMPT_TOKENS, name="outproj_ffn2_prompt")
    keep = min(WINDOW, t)
    heads = lambda a: a.reshape(nb, -1, N_KV, HEAD_DIM)
    states = (heads(k_c), heads(v_c), heads(k_s), heads(v_s), heads(k_w[:, t - keep:]), heads(v_w[:, t - keep:]),
              u_pool[:, t - POOL_STATE:])
    return y, states


SAMPLE_BATCHES = 32


def _sample_layer(x, mod, w, rel_bias, alpha, page_table, pools, kwin, vwin, pool_state):
    nb, t, d = x.shape
    hd = HEAD_DIM
    n_heads = rel_bias.shape[0]
    rep = n_heads // N_KV
    kvw = N_KV * hd
    pos0 = page_table.shape[1] * PAGE_SIZE
    n_c = (pos0 + t - L_CMP) // STRIDE + 1
    assert (n_c - 1) * STRIDE + L_CMP <= pos0 and pos0 % PAGE_SIZE == 0 and nb % SAMPLE_BATCHES == 0
    n_rows = pos0 // ROW_TOKENS
    n_blk = -(-(pos0 + t) // L_SEL)
    assert n_blk - 1 == pos0 // L_SEL
    blk_pad = -(-n_blk // LANE) * LANE
    wbuf = kwin.shape[1]

    x1, u_pool, q, k_c, v_c, k_s, v_s, k_w, v_w, gates = _ffn_in(
        x, mod, w["wi1"], w["wo1"], w["wp"], w["ln1"], alpha=alpha, nbk=SAMPLE_BATCHES, tt=t, for_prompt=False)
    pw = u_pool.shape[-1]
    prev = jnp.concatenate([jnp.zeros((nb, PREV_ROWS - POOL_STATE, pw), F32), pool_state], axis=1)
    pool_out = _pool_mix(u_pool, prev, w["wpool"], w["pool_scale"], pos0=pos0, nbk=SAMPLE_BATCHES, tt=t,
                         name="pool_sample")

    qh = q.reshape(nb, t, N_KV, rep, hd).transpose(0, 2, 3, 1, 4)
    own = jnp.eye(N_KV, dtype=q.dtype)[None, :, None, None, :, None]
    qbd = (qh[:, :, :, :, None, :] * own).reshape(nb, N_KV * rep * t, kvw)
    gate_rows = gates[:, :, :N_KV * GATE_ROWS].reshape(nb, t, N_KV, GATE_ROWS)[..., :3 * rep]
    gate_rows = gate_rows.reshape(nb, t, N_KV, 3, rep).transpose(0, 2, 4, 1, 3).reshape(nb, N_KV * rep * t, 3)
    gate_rows = jnp.pad(gate_rows, ((0, 0), (0, 0), (0, LANE - 3)))

    tab_c, near, newtab, wtab = _sample_tables(rel_bias, pos0, t, n_rows, n_c, wbuf)
    cov = np.zeros((n_rows, blk_pad), np.float32)
    ci = np.arange(n_rows)[:, None]
    bj = np.arange(blk_pad)[None, :]
    cov[:] = ((ci * STRIDE <= (bj + 1) * L_SEL - 1) & (ci * STRIDE + L_CMP - 1 >= bj * L_SEL)
              & (ci < n_c) & (bj < n_blk))
    channel_major = lambda a: a.transpose(0, 2, 3, 1).reshape(a.shape[0], kvw, a.shape[1])
    oc, imp = _sample_cmp(page_table, channel_major(pools[0]), channel_major(pools[1]), qbd, tab_c, w["cmp_k"],
                          w["cmp_v"], jnp.asarray(cov, BF16), rep=rep, t_new=t)

    sel_rows = -(-n_blk // 8) * 8
    imp_t = imp.transpose(2, 0, 1).reshape(blk_pad, nb * N_KV * t)[:sel_rows]
    selb_t = _select_blocks(imp_t, top_n=min(TOP_N, n_blk), n_blk=n_blk, pos0=pos0, t_new=t)
    selb_t = jnp.pad(selb_t, ((0, blk_pad - sel_rows), (0, 0)), constant_values=NEG)
    selb = selb_t.reshape(blk_pad, nb, N_KV * t).transpose(1, 2, 0)

    key_blk = lax.broadcasted_iota(jnp.int32, (pos0 // L_SEL, pos0), 1) // L_SEL
    expand = (key_blk == lax.broadcasted_iota(jnp.int32, (pos0 // L_SEL, pos0), 0)).astype(BF16)
    kvf = lambda a: a.reshape(a.shape[0], a.shape[1], kvw)
    o = _sample_slc(page_table, channel_major(pools[2]), channel_major(pools[3]), qbd, selb, gate_rows, oc,
                    k_s, v_s, k_w, v_w, channel_major(kwin), channel_major(vwin), near, newtab, wtab, expand,
                    rep=rep, t_new=t)
    nsa = o[:, :, :hd].reshape(nb, N_KV, rep, t, hd).transpose(0, 3, 1, 2, 4).reshape(nb, t, n_heads * hd).astype(BF16)

    y = _ffn_out(pool_out, nsa, x1, mod, w["wout"], w["wi2"], w["wo2"], w["ln23"], alpha=alpha,
                 nbk=SAMPLE_BATCHES, tt=t, name="outproj_ffn2_sample")
    keep = min(WINDOW, pos0 + t)
    heads = lambda a: a.reshape(nb, -1, N_KV, HEAD_DIM)
    kw_ext = jnp.concatenate([kvf(kwin), k_w], axis=1)
    vw_ext = jnp.concatenate([kvf(vwin), v_w], axis=1)
    pool_ext = jnp.concatenate([pool_state, u_pool], axis=1)
    states = (heads(k_c), heads(v_c), heads(k_s), heads(v_s), heads(kw_ext[:, kw_ext.shape[1] - keep:]),
              heads(vw_ext[:, vw_ext.shape[1] - keep:]), pool_ext[:, pool_ext.shape[1] - POOL_STATE:])
    return y, states


def kernel(x_prompt, x_sample, c_prompt, c_sample, cache_k_cmp, cache_v_cmp, cache_k_slc, cache_v_slc, page_table,
           state_k_win, state_v_win, state_pool, w_ada, b_ada, ln_gain, ln_bias, w_ffn1_in, w_ffn1_out, w_ffn2_in,
           w_ffn2_out, w_in, w_phi_k1, w_phi_k2, w_phi_v1, w_phi_v2, cmp_pos, w_pool, pool_scale, w_out, rel_bias):
    depth = w_ada.shape[0]
    alpha = (2.0 * depth) ** 0.25
    n_heads = rel_bias.shape[0]
    nb_p, nb_s = x_prompt.shape[0], x_sample.shape[0]
    d = x_prompt.shape[-1]
    xp, xs = x_prompt, x_sample
    c_all = jnp.concatenate([c_prompt, c_sample], axis=0)
    p_states, s_states = [], []
    for l in range(depth):
        p = dict(w_ada=w_ada[l], b_ada=b_ada[l], ln_gain=ln_gain[l], ln_bias=ln_bias[l], w_ffn1_in=w_ffn1_in[l],
                 w_ffn1_out=w_ffn1_out[l], w_ffn2_in=w_ffn2_in[l], w_ffn2_out=w_ffn2_out[l], w_in=w_in[l],
                 w_phi_k1=w_phi_k1[l], w_phi_k2=w_phi_k2[l], w_phi_v1=w_phi_v1[l], w_phi_v2=w_phi_v2[l],
                 cmp_pos=cmp_pos[l], w_pool=w_pool[l], pool_scale=pool_scale[l], w_out=w_out[l])
        w = _layer_weights(p, n_heads)
        mod = _ada(c_all, p["w_ada"], p["b_ada"]).reshape(nb_p + nb_s, 9, d)
        xp, sp = _prompt_layer(xp, mod[:nb_p], w, rel_bias, alpha)
        xs, ss = _sample_layer(xs, mod[nb_p:], w, rel_bias, alpha, page_table,
                               (cache_k_cmp[l], cache_v_cmp[l], cache_k_slc[l], cache_v_slc[l]),
                               state_k_win[l], state_v_win[l], state_pool[l])
        p_states.append(sp)
        s_states.append(ss)
    stack = lambda states: tuple(jnp.stack(a) for a in zip(*states))
    return (xp, xs) + stack(p_states) + stack(s_states)
```

```python
import functools
import math

import numpy as np
import jax
import jax.numpy as jnp
from jax import lax
from jax.experimental import pallas as pl
from jax.experimental.pallas import tpu as pltpu

F32 = jnp.float32
BF16 = jnp.bfloat16

HEAD_DIM = 64
N_KV = 2
L_CMP = 32
STRIDE = 16
L_SEL = 64
TOP_N = 16
WINDOW = 512
N_BUCKETS = 32
MAX_DIST = 128
POOL_WINDOWS = (2, 4, 8, 16)
POOL_STATE = max(POOL_WINDOWS) - 1
LN_EPS = 1e-5
NEG = -1e30
PAGE_SIZE = 128

Q_SCALE = HEAD_DIM ** -0.5 * math.log2(math.e)
TQ = 128
FAR_CHUNKS = 4
FAR_BLOCKS = FAR_CHUNKS * TQ // L_SEL
VT_ROWS = HEAD_DIM + 16

LANE = 128
MXU_COLS = 256
FF_CHUNK = MXU_COLS
VMEM_LIMIT = 56 * 1024 * 1024


def _cparams(n_axes):
    return pltpu.CompilerParams(dimension_semantics=("arbitrary",) * n_axes,
                                vmem_limit_bytes=VMEM_LIMIT)


def _const_spec(shape):
    nd = len(shape)
    return pl.BlockSpec(shape, lambda *_: (0,) * nd, pipeline_mode=pl.Buffered(1))


def _layer_norm(y, gain, bias):
    mu = jnp.mean(y, axis=-1, keepdims=True)
    yc = y - mu
    var = jnp.mean(yc * yc, axis=-1, keepdims=True)
    return yc * lax.rsqrt(var + LN_EPS) * gain + bias


def _swiglu(u_bf, wi_ref, wo_ref):
    d_ff = wo_ref.shape[0]
    acc = None
    for c in range(d_ff // FF_CHUNK):
        lo = c * FF_CHUNK
        a = jnp.dot(u_bf, wi_ref[:, lo:lo + FF_CHUNK], preferred_element_type=F32)
        b = jnp.dot(u_bf, wi_ref[:, d_ff + lo:d_ff + lo + FF_CHUNK], preferred_element_type=F32)
        hid = (a * jax.nn.sigmoid(a) * b).astype(BF16)
        part = jnp.dot(hid, wo_ref[lo:lo + FF_CHUNK, :], preferred_element_type=F32)
        acc = part if acc is None else acc + part
    return acc


def _ada_body(c_ref, w_ref, b_ref, o_ref):
    c = c_ref[...]
    act = (c * jax.nn.sigmoid(c)).astype(BF16)
    o_ref[...] = jnp.dot(act, w_ref[...].astype(BF16), preferred_element_type=F32) + b_ref[...]


def _ada(c_all, w_ada, b_ada):
    nb, d = c_all.shape
    n_out = w_ada.shape[1]
    cols = 9 * LANE
    return pl.pallas_call(
        _ada_body,
        grid=(n_out // cols,),
        in_specs=[pl.BlockSpec((nb, d), lambda j: (0, 0)),
                  pl.BlockSpec((d, cols), lambda j: (0, j)),
                  pl.BlockSpec((1, cols), lambda j: (0, j))],
        out_specs=pl.BlockSpec((nb, cols), lambda j: (0, j)),
        out_shape=jax.ShapeDtypeStruct((nb, n_out), F32),
        compiler_params=_cparams(1),
        name="ada",
    )(c_all, w_ada, b_ada.reshape(1, n_out))


def _ffn_in_body(x_ref, mod_ref, wi_ref, wo_ref, wp_ref, ln_ref, *out_refs, alpha, pool_w, nsa_w, kv_w, for_prompt):
    nbk, tt, d = x_ref.shape
    n = nbk * tt
    x = x_ref[...]
    mod = mod_ref[...]
    u = (x * (1.0 + mod[:, 1:2, :]) + mod[:, 0:1, :]).reshape(n, d).astype(BF16)
    h = _swiglu(u, wi_ref, wo_ref).reshape(nbk, tt, d)
    x1 = _layer_norm(alpha * x + 0.5 * mod[:, 2:3, :] * h, ln_ref[0:1, :], ln_ref[1:2, :])
    u1 = (x1 * (1.0 + mod[:, 4:5, :]) + mod[:, 3:4, :]).reshape(n, d).astype(BF16)
    proj = jnp.dot(u1, wp_ref[...], preferred_element_type=F32)

    x1_ref, upool_ref, q_ref = out_refs[:3]
    kv_refs = out_refs[3:9]
    x1_ref[...] = x1
    upool_ref[...] = proj[:, :pool_w].reshape(nbk, tt, pool_w)
    q_ref[...] = (proj[:, pool_w:pool_w + nsa_w] * Q_SCALE).astype(BF16).reshape(nbk, tt, nsa_w)
    off = pool_w + nsa_w
    kv = []
    for i in range(6):
        blk = proj[:, off + i * kv_w:off + (i + 1) * kv_w]
        kv.append(blk)
        kv_refs[i][...] = blk.reshape(nbk, tt, kv_w)
    gates = jax.nn.sigmoid(proj[:, off + 6 * kv_w:off + 6 * kv_w + LANE])
    if for_prompt:
        ksb_ref, vst_ref, kwb_ref, vwt_ref, gt_ref = out_refs[9:]
        row = lax.broadcasted_iota(jnp.int32, (n, kv_w), 0)
        lane = lax.broadcasted_iota(jnp.int32, (n, kv_w), 1)
        onehot = jnp.where((row // L_SEL) % FAR_BLOCKS == lane, 1.0, 0.0)
        ones_row = jnp.where(lax.broadcasted_iota(jnp.int32, (VT_ROWS - HEAD_DIM, n), 0) == 0, 1.0, 0.0)

        def values_t(x):
            xt = x.T
            return jnp.concatenate([piece for g in range(N_KV)
                                    for piece in (xt[g * HEAD_DIM:(g + 1) * HEAD_DIM], ones_row)], axis=0).astype(BF16)

        ksb_ref[0] = jnp.concatenate([kv[2], onehot], axis=1).astype(BF16)
        vst_ref[0] = values_t(kv[3])
        kwb_ref[0] = kv[4].astype(BF16)
        vwt_ref[0] = values_t(kv[5])
        gt_ref[0] = gates.T[:gt_ref.shape[1], :]
    else:
        out_refs[9][...] = gates.reshape(nbk, tt, LANE)


def _ffn_in(x, mod, wi, wo, wp, ln, *, alpha, nbk, tt, for_prompt):
    nb, t, d = x.shape
    pool_w = d // 2
    nsa_w = d - pool_w
    kv_w = N_KV * HEAD_DIM
    grid = (nb // nbk, t // tt)
    tok = lambda w: pl.BlockSpec((nbk, tt, w), lambda i, j: (i, j, 0))
    out_specs = [tok(d), tok(pool_w), tok(nsa_w)] + [tok(kv_w)] * 6
    out_shape = ([jax.ShapeDtypeStruct((nb, t, d), F32), jax.ShapeDtypeStruct((nb, t, pool_w), F32),
                  jax.ShapeDtypeStruct((nb, t, nsa_w), BF16)]
                 + [jax.ShapeDtypeStruct((nb, t, kv_w), F32)] * 6)
    if for_prompt:
        assert nbk == 1 and tt % (FAR_BLOCKS * L_SEL) == 0
        tr = lambda rows: pl.BlockSpec((1, rows, tt), lambda i, j: (i, 0, j))
        vt_rows = N_KV * VT_ROWS
        out_specs += [tok(2 * kv_w), tr(vt_rows), tok(kv_w), tr(vt_rows), tr(N_KV * GATE_ROWS)]
        out_shape += [jax.ShapeDtypeStruct((nb, t, 2 * kv_w), BF16), jax.ShapeDtypeStruct((nb, vt_rows, t), BF16),
                      jax.ShapeDtypeStruct((nb, t, kv_w), BF16), jax.ShapeDtypeStruct((nb, vt_rows, t), BF16),
                      jax.ShapeDtypeStruct((nb, N_KV * GATE_ROWS, t), F32)]
    else:
        out_specs += [tok(LANE)]
        out_shape += [jax.ShapeDtypeStruct((nb, t, LANE), F32)]
    body = functools.partial(_ffn_in_body, alpha=alpha, pool_w=pool_w, nsa_w=nsa_w, kv_w=kv_w,
                             for_prompt=for_prompt)
    return pl.pallas_call(
        body,
        grid=grid,
        in_specs=[tok(d),
                  pl.BlockSpec((nbk, mod.shape[1], d), lambda i, j: (i, 0, 0)),
                  _const_spec(wi.shape), _const_spec(wo.shape), _const_spec(wp.shape), _const_spec(ln.shape)],
        out_specs=out_specs,
        out_shape=out_shape,
        compiler_params=_cparams(2),
        name="ffn1_inproj_prompt" if for_prompt else "ffn1_inproj_sample",
    )(x, mod, wi, wo, wp, ln)


GATE_ROWS = 16


def _gate_column_source(n_heads):
    rep = n_heads // N_KV
    src = np.full((LANE,), -1, np.int32)
    for g in range(N_KV):
        for br in range(3):
            for r in range(rep):
                src[g * GATE_ROWS + br * rep + r] = br * n_heads + g * rep + r
    return src


def _ffn_out_body(pool_ref, nsa_ref, x1_ref, mod_ref, wout_ref, wi_ref, wo_ref, ln_ref, o_ref, *, alpha):
    nbk, tt, d = x1_ref.shape
    n = nbk * tt
    pw = pool_ref.shape[-1]
    mod = mod_ref[...]
    x1 = x1_ref[...]
    mix = (jnp.dot(pool_ref[...].reshape(n, pw), wout_ref[:pw, :], preferred_element_type=F32)
           + jnp.dot(nsa_ref[...].reshape(n, d - pw), wout_ref[pw:, :], preferred_element_type=F32))
    x2 = _layer_norm(alpha * x1 + mod[:, 5:6, :] * mix.reshape(nbk, tt, d), ln_ref[0:1, :], ln_ref[1:2, :])
    u = (x2 * (1.0 + mod[:, 7:8, :]) + mod[:, 6:7, :]).reshape(n, d).astype(BF16)
    h = _swiglu(u, wi_ref, wo_ref).reshape(nbk, tt, d)
    o_ref[...] = _layer_norm(alpha * x2 + 0.5 * mod[:, 8:9, :] * h, ln_ref[2:3, :], ln_ref[3:4, :])


def _ffn_out(pool_out, nsa_out, x1, mod, wout, wi, wo, ln, *, alpha, nbk, tt, name):
    nb, t, d = x1.shape
    tok = lambda w: pl.BlockSpec((nbk, tt, w), lambda i, j: (i, j, 0))
    return pl.pallas_call(
        functools.partial(_ffn_out_body, alpha=alpha),
        grid=(nb // nbk, t // tt),
        in_specs=[tok(pool_out.shape[-1]), tok(nsa_out.shape[-1]), tok(d),
                  pl.BlockSpec((nbk, mod.shape[1], d), lambda i, j: (i, 0, 0)),
                  _const_spec(wout.shape), _const_spec(wi.shape), _const_spec(wo.shape), _const_spec(ln.shape)],
        out_specs=tok(d),
        out_shape=jax.ShapeDtypeStruct((nb, t, d), F32),
        compiler_params=_cparams(2),
        name=name,
    )(pool_out, nsa_out, x1, mod, wout, wi, wo, ln)


PREV_ROWS = 16


def _pool_body(u_ref, prev_ref, w_ref, scale_ref, o_ref, ext_ref, *, pos0, zero_first):
    nbk, tt, width = u_ref.shape
    gw = width // len(POOL_WINDOWS)
    j = pl.program_id(1)
    u = u_ref[...]
    prev = prev_ref[...]
    if zero_first:
        prev = jnp.where(j == 0, 0.0, prev)
    ext_ref[:, 0:PREV_ROWS, :] = prev
    ext_ref[:, PREV_ROWS:, :] = u
    pos = pos0 + j * tt + lax.broadcasted_iota(jnp.int32, (1, tt, 1), 1)
    for g, w in enumerate(POOL_WINDOWS):
        lo = g * gw
        tot = None
        for k in range(w):
            part = ext_ref[:, PREV_ROWS - k:PREV_ROWS - k + tt, lo:lo + gw]
            tot = part if tot is None else tot + part
        cnt = jnp.minimum(pos + 1, w).astype(F32)
        pooled = (tot / cnt - u[:, :, lo:lo + gw]).astype(BF16).reshape(nbk * tt, gw)
        mixed = jnp.dot(pooled, w_ref[g], preferred_element_type=F32) * scale_ref[:, lo:lo + gw]
        o_ref[:, :, lo:lo + gw] = mixed.astype(BF16).reshape(nbk, tt, gw)


def _pool_mix(u_pool, prev, w_pool_bf, pool_scale, *, pos0, nbk, tt, name):
    nb, t, width = u_pool.shape
    tok = pl.BlockSpec((nbk, tt, width), lambda i, j: (i, j, 0))
    if prev is None:
        assert nbk == 1 and tt % PREV_ROWS == 0 and pos0 == 0
        step = tt // PREV_ROWS
        prev_arr = u_pool
        prev_spec = pl.BlockSpec((1, PREV_ROWS, width), lambda i, j: (i, jnp.maximum(j * step - 1, 0), 0))
    else:
        assert t == tt
        prev_arr = prev
        prev_spec = pl.BlockSpec((nbk, PREV_ROWS, width), lambda i, j: (i, 0, 0))
    return pl.pallas_call(
        functools.partial(_pool_body, pos0=pos0, zero_first=prev is None),
        grid=(nb // nbk, t // tt),
        in_specs=[tok, prev_spec, _const_spec(w_pool_bf.shape), _const_spec((1, width))],
        out_specs=tok,
        out_shape=jax.ShapeDtypeStruct((nb, t, width), BF16),
        scratch_shapes=[pltpu.VMEM((nbk, PREV_ROWS + tt, width), F32)],
        compiler_params=_cparams(2),
        name=name,
    )(u_pool, prev_arr, w_pool_bf, pool_scale.reshape(1, width))


ROW_TOKENS = STRIDE


def _compress_body(k_ref, wexp_ref, w2_ref, pos_ref, o_ref, sh_ref, *, transposed):
    kvw = k_ref.shape[2]
    nr = k_ref.shape[1] // ROW_TOKENS
    half = wexp_ref.shape[1] // 2
    p = None
    for i in range(ROW_TOKENS // 2):
        lhs = jnp.concatenate([k_ref[0, pl.ds(2 * i + e, nr, stride=ROW_TOKENS), :] for e in range(2)], axis=1)
        part = jnp.dot(lhs.astype(BF16), wexp_ref[2 * i * kvw:2 * (i + 1) * kvw, :], preferred_element_type=F32)
        p = part if p is None else p + part
    pp = jnp.dot(pos_ref[...], wexp_ref[...], preferred_element_type=F32)
    posb = pp[0:1, :half] + pp[1:2, half:]
    sh_ref[0:nr, :] = p[:, half:]
    sh_ref[nr:nr + 8, :] = jnp.zeros((8, half), F32)
    h = p[:, :half] + sh_ref[1:nr + 1, :] + posb
    c = jnp.dot(jax.nn.gelu(h).astype(BF16), w2_ref[...], preferred_element_type=F32)
    if transposed:
        ct = c.T
        ones_row = jnp.where(lax.broadcasted_iota(jnp.int32, (VT_ROWS - HEAD_DIM, nr), 0) == 0, 1.0, 0.0)
        c = jnp.concatenate([piece for g in range(N_KV)
                             for piece in (ct[g * HEAD_DIM:(g + 1) * HEAD_DIM], ones_row)], axis=0)
    o_ref[0] = c.astype(BF16)


def _compress(tokens, wexp, w2bd, posrows, *, transposed, name):
    nb, t, k = tokens.shape
    nr = t // ROW_TOKENS
    half = wexp.shape[1] // 2
    oshape = (nb, N_KV * VT_ROWS, nr) if transposed else (nb, nr, half)
    return pl.pallas_call(
        functools.partial(_compress_body, transposed=transposed),
        grid=(nb,),
        in_specs=[pl.BlockSpec((1, t, k), lambda b: (b, 0, 0)),
                  _const_spec(wexp.shape), _const_spec(w2bd.shape), _const_spec(posrows.shape)],
        out_specs=pl.BlockSpec((1,) + oshape[1:], lambda b: (b, 0, 0)),
        out_shape=jax.ShapeDtypeStruct(oshape, BF16),
        scratch_shapes=[pltpu.VMEM((nr + 8, half), F32)],
        compiler_params=_cparams(1),
        name=name,
    )(tokens, wexp, w2bd, posrows)


def _compress_weights(w1, w2, cmp_pos):
    eye = jnp.eye(N_KV, dtype=F32)
    w_blk = w1.reshape(L_CMP, HEAD_DIM, HEAD_DIM)
    halves = []
    for part in range(L_CMP // ROW_TOKENS):
        wpart = w_blk[part * ROW_TOKENS:(part + 1) * ROW_TOKENS]
        halves.append(jnp.einsum("gh,lio->lgiho", eye, wpart).reshape(ROW_TOKENS * N_KV * HEAD_DIM, N_KV * HEAD_DIM))
    wexp = jnp.concatenate(halves, axis=1).astype(BF16)
    w2bd = jnp.einsum("gh,io->giho", eye, w2).reshape(N_KV * HEAD_DIM, N_KV * HEAD_DIM).astype(BF16)
    pos = jnp.broadcast_to(cmp_pos.reshape(L_CMP // ROW_TOKENS, ROW_TOKENS, 1, HEAD_DIM),
                           (L_CMP // ROW_TOKENS, ROW_TOKENS, N_KV, HEAD_DIM)).reshape(L_CMP // ROW_TOKENS, -1)
    posrows = jnp.zeros((8, pos.shape[1]), F32).at[:pos.shape[0]].set(pos).astype(BF16)
    return wexp, w2bd, posrows


def _bucket_starts():
    d = np.arange(0, 4 * MAX_DIST)
    exact = N_BUCKETS // 2
    nf = np.maximum(d, 1).astype(np.float32)
    large = exact + (np.log(nf / np.float32(exact)) / np.float32(math.log(MAX_DIST / exact))
                     * np.float32(N_BUCKETS - exact)).astype(np.int32)
    bucket = np.where(d < exact, d, np.minimum(large, N_BUCKETS - 1))
    assert bucket.max() == N_BUCKETS - 1 and np.all(np.diff(bucket) >= 0)
    return [int(np.argmax(bucket >= b)) for b in range(1, N_BUCKETS)]


def _bias_minus_far(rel_bias, dist, head_axis):
    tab = (rel_bias - rel_bias[:, N_BUCKETS - 1:]) * math.log2(math.e)
    hshape = (1,) * head_axis + (-1,) + (1,) * (dist.ndim - head_axis)
    dist = jnp.expand_dims(dist, head_axis)
    out = jnp.broadcast_to(tab[:, 0].reshape(hshape),
                           dist.shape[:head_axis] + (tab.shape[0],) + dist.shape[head_axis + 1:])
    for b, start in enumerate(_bucket_starts(), start=1):
        out = jnp.where(dist >= start, tab[:, b].reshape(hshape), out)
    return out


def _prompt_tables(rel_bias, t):
    h = rel_bias.shape[0]
    qo = jnp.arange(TQ, dtype=jnp.int32)
    n_near = WINDOW // L_SEL + 2 * (TQ // L_SEL)
    dd = (WINDOW // L_SEL) - jnp.arange(n_near, dtype=jnp.int32)
    ko = jnp.arange(L_SEL, dtype=jnp.int32)
    d = (L_SEL * dd[:, None, None] + qo[None, None, :] - ko[None, :, None]).reshape(n_near * L_SEL, TQ)
    ok = (d >= 0) & (d < WINDOW)
    near = jnp.where(ok[:, None, :], _bias_minus_far(rel_bias, d, 1), NEG)
    near = near.reshape(n_near * L_SEL, h * TQ)
    n_tiles = t // TQ
    per_tile = TQ // STRIDE
    e0 = per_tile * (n_tiles - 1)
    rows = e0 + t // STRIDE
    e = e0 - jnp.arange(rows, dtype=jnp.int32)
    dc = STRIDE * e[:, None] - (L_CMP - 1) + qo[None, :]
    cmp = jnp.where((dc >= 0)[:, None, :], _bias_minus_far(rel_bias, dc, 1), NEG)
    cmp = cmp.reshape(rows, h * TQ)
    return near.astype(F32), cmp.astype(F32), e0


def _cover_t(n_rows, n_c, n_blk):
    i = np.arange(n_rows)[None, :]
    j = np.arange(n_blk)[:, None]
    start = i * STRIDE
    end = start + L_CMP - 1
    cov = (start <= (j + 1) * L_SEL - 1) & (end >= j * L_SEL) & (i < n_c)
    return jnp.asarray(cov.astype(np.float32), BF16)


def _descending_rank(v, n_valid, n_live=None):
    rows, cols = v.shape
    sub = lax.broadcasted_iota(jnp.int32, (8, cols), 0)

    def count(rank, lo):
        mid = v[lo:lo + 8]
        for jp in range(lo, min(lo + 8, n_valid)):
            other = v[jp:jp + 1, :]
            parts = [jnp.where(other > mid, 1.0, jnp.where((other == mid) & (sub > jp - lo), 1.0, 0.0))]
            if lo > 0:
                parts.insert(0, jnp.where(other > v[:lo], 1.0, 0.0))
            if lo + 8 < rows:
                parts.append(jnp.where(other >= v[lo + 8:], 1.0, 0.0))
            rank = rank + jnp.concatenate(parts, axis=0)
        return rank

    rank = jnp.zeros((rows, cols), F32)
    for lo in range(0, n_valid, 8):
        if n_live is None:
            rank = count(rank, lo)
        else:
            rank = lax.cond(lo < n_live, functools.partial(count, lo=lo), lambda r: r, rank)
    return rank


def _nsa_prompt_body(q_ref, gt_ref, kc_ref, vct_ref, ks_ref, vst_ref, kw_ref, vwt_ref, tcmp_ref, near_ref, cov_ref,
                     o_ref, selb_ref, qaug_ref, qaug1_ref, s0_ref, s1_ref, sw_ref, sd_ref, *, e0, top_n):
    it = pl.program_id(1)
    hd = HEAD_DIM
    rep = q_ref.shape[2] // (N_KV * hd)
    gc = rep * TQ
    cols = N_KV * gc
    n_blk = cov_ref.shape[0]
    per = TQ // L_SEL

    def per_group(x):
        return jnp.concatenate([x[:, g * TQ:(g + 1) * TQ] for g in range(N_KV) for _ in range(rep)], axis=1)

    def pv(vt_ref, keys, p, rows=VT_ROWS):
        return jnp.concatenate(
            [jnp.dot(vt_ref[0, g * rows:(g + 1) * rows, keys], p[:, g * gc:(g + 1) * gc],
                     preferred_element_type=F32) for g in range(N_KV)], axis=1)

    qt = q_ref[0].astype(F32).T
    zero = jnp.zeros((hd, TQ), F32)
    tiles = []
    for g in range(N_KV):
        for r in range(rep):
            blk = qt[(g * rep + r) * hd:(g * rep + r + 1) * hd, :]
            tiles.append(jnp.concatenate([blk if gg == g else zero for gg in range(N_KV)], axis=0))
    qbd = jnp.concatenate(tiles, axis=1).astype(BF16)

    gate = [jnp.concatenate([gt_ref[0, g * GATE_ROWS + br * rep + r:g * GATE_ROWS + br * rep + r + 1, :]
                             for g in range(N_KV) for r in range(rep)], axis=1) for br in range(3)]

    n_ck = kc_ref.shape[1]
    sc = jnp.dot(kc_ref[0], qbd, preferred_element_type=F32)
    t0 = pl.multiple_of(e0 - (TQ // STRIDE) * it, 8)
    sc = sc + tcmp_ref[pl.ds(t0, n_ck), :]

    n_wc = WINDOW // TQ + 1
    offs = []
    for k in range(n_wc):
        kidx = it - (n_wc - 1) + k
        offs.append(pl.multiple_of(jnp.maximum(kidx, 0) * TQ, TQ))
        tab_off = pl.multiple_of(jnp.where(kidx >= 0, k * TQ, near_ref.shape[0] - TQ), TQ)
        sw_ref[k * TQ:(k + 1) * TQ, :] = (jnp.dot(kw_ref[0, pl.ds(offs[k], TQ), :], qbd, preferred_element_type=F32)
                                          + near_ref[pl.ds(tab_off, TQ), :])

    mc = jnp.max(sc, axis=0, keepdims=True)
    pc = jnp.exp2(sc - mc).astype(BF16)
    oc = pv(vct_ref, slice(None), pc)
    inv_lc = jnp.where(mc > 0.1 * NEG, 1.0 / oc[hd:hd + 1], 0.0)
    oc = oc[:hd] * inv_lc
    imp = jnp.dot(cov_ref[...], pc, preferred_element_type=F32) * inv_lc

    sw = sw_ref[...]
    pw = jnp.exp2(sw - jnp.max(sw, axis=0, keepdims=True)).astype(BF16)
    ow = None
    for k in range(n_wc):
        part = pv(vwt_ref, pl.ds(offs[k], TQ), pw[k * TQ:(k + 1) * TQ, :])
        ow = part if ow is None else ow + part

    diag0 = jnp.maximum(it - 1, 0)
    dk = 2 * TQ
    kvw = N_KV * hd
    offd = pl.multiple_of(diag0 * TQ, TQ)
    near_off = pl.multiple_of((WINDOW // L_SEL - per * (it - diag0)) * L_SEL, TQ)
    sd_ref[...] = (jnp.dot(ks_ref[0, pl.ds(offd, dk), 0:kvw], qbd, preferred_element_type=F32)
                   + near_ref[pl.ds(near_off, dk), :])

    v = []
    for g in range(N_KV):
        vg = imp[:, g * gc:g * gc + TQ]
        for r in range(1, rep):
            vg = vg + imp[:, g * gc + r * TQ:g * gc + (r + 1) * TQ]
        v.append(vg)
    v = jnp.concatenate(v, axis=1)
    j = lax.broadcasted_iota(jnp.int32, v.shape, 0)
    lane = lax.broadcasted_iota(jnp.int32, v.shape, 1)
    cur = per * it + (lane % TQ) // L_SEL
    forced = (j == 0) | (j == cur) | (j == cur - 1)
    v = jnp.where(forced, 1e9, jnp.where(j <= cur, v, -1e9))
    rank = _descending_rank(v, n_blk, n_live=per * (it + 1))
    selb = jnp.where((rank < top_n) & (j <= cur), 0.0, NEG)

    selb_ref[...] = per_group(jnp.where(j < per * diag0, selb, NEG))
    gk = FAR_CHUNKS * TQ
    for ref in (qaug_ref, qaug1_ref):
        ref[0:kvw, :] = qbd
        ref[kvw + 16:, :] = jnp.zeros((ref.shape[0] - kvw - 16, cols), BF16)

    def online_step(carry, s, keys):
        m, acc = carry
        m_new = jnp.maximum(m, jnp.max(s, axis=0, keepdims=True))
        p = jnp.exp2(s - m_new).astype(BF16)
        return m_new, jnp.exp2(m - m_new) * acc + pv(vst_ref, keys, p)

    def far_keys(cb):
        return pl.ds(pl.multiple_of(cb * gk, gk), gk)

    def far_scores(cb, q_ref2, s_out):
        sel = selb_ref[pl.ds(pl.multiple_of(cb * FAR_BLOCKS, FAR_BLOCKS), FAR_BLOCKS), :]
        q_ref2[kvw:kvw + 16, :] = jnp.concatenate([sel, jnp.zeros((16 - FAR_BLOCKS, cols), F32)], axis=0).astype(BF16)
        s_out[...] = jnp.dot(ks_ref[0, far_keys(cb), :], q_ref2[...], preferred_element_type=F32)

    def far_pair(jb, carry):
        far_scores(2 * jb, qaug_ref, s0_ref)
        far_scores(2 * jb + 1, qaug1_ref, s1_ref)
        carry = online_step(carry, s0_ref[...], far_keys(2 * jb))
        return online_step(carry, s1_ref[...], far_keys(2 * jb + 1))

    carry = (jnp.full((1, cols), NEG, F32), jnp.zeros((VT_ROWS, cols), F32))
    n_groups = (diag0 + FAR_CHUNKS - 1) // FAR_CHUNKS
    carry = lax.fori_loop(0, (n_groups + 1) // 2, far_pair, carry)

    sd = sd_ref[...]
    rows = []
    for k in range(dk // L_SEL):
        rk = jnp.max(jnp.where(j == per * diag0 + k, selb, NEG), axis=0, keepdims=True)
        rows.append(jnp.broadcast_to(per_group(rk), (L_SEL, cols)))
    sd = sd + jnp.concatenate(rows, axis=0)
    _, acc_s = online_step(carry, sd, pl.ds(offd, dk))

    out_t = (gate[0] * oc + (gate[1] / acc_s[hd:hd + 1]) * acc_s[:hd]
             + (gate[2] / ow[hd:hd + 1]) * ow[:hd])
    pieces = []
    for c in range(0, N_KV * rep, 2):
        pair = jnp.concatenate([out_t[:, c * TQ:(c + 1) * TQ], out_t[:, (c + 1) * TQ:(c + 2) * TQ]], axis=0)
        pieces.append(pair.T)
    o_ref[0] = jnp.concatenate(pieces, axis=1).astype(BF16)


def _nsa_prompt(q, gt, kc, vct, ks, vst, kw, vwt, tcmp, near, cov_t, *, e0, top_n):
    nb, t, nsa_w = q.shape
    assert t % (2 * FAR_CHUNKS * TQ) == 0
    cols = (nsa_w // HEAD_DIM) * TQ
    per_b = lambda shape: pl.BlockSpec((1,) + shape, lambda b, i: (b, 0, 0))
    return pl.pallas_call(
        functools.partial(_nsa_prompt_body, e0=e0, top_n=top_n),
        grid=(nb, t // TQ),
        in_specs=[pl.BlockSpec((1, TQ, nsa_w), lambda b, i: (b, i, 0)),
                  pl.BlockSpec((1, N_KV * GATE_ROWS, TQ), lambda b, i: (b, 0, i)),
                  per_b(kc.shape[1:]), per_b(vct.shape[1:]),
                  per_b(ks.shape[1:]), per_b(vst.shape[1:]), per_b(kw.shape[1:]), per_b(vwt.shape[1:]),
                  _const_spec(tcmp.shape), _const_spec(near.shape), _const_spec(cov_t.shape)],
        out_specs=pl.BlockSpec((1, TQ, nsa_w), lambda b, i: (b, i, 0)),
        out_shape=jax.ShapeDtypeStruct((nb, t, nsa_w), BF16),
        scratch_shapes=[pltpu.VMEM((cov_t.shape[0], cols), F32),
                        pltpu.VMEM((ks.shape[2], cols), BF16), pltpu.VMEM((ks.shape[2], cols), BF16),
                        pltpu.VMEM((FAR_CHUNKS * TQ, cols), F32), pltpu.VMEM((FAR_CHUNKS * TQ, cols), F32),
                        pltpu.VMEM((WINDOW + TQ, cols), F32), pltpu.VMEM((2 * TQ, cols), F32)],
        compiler_params=_cparams(2),
        name="nsa_prompt",
    )(q, gt, kc, vct, ks, vst, kw, vwt, tcmp, near, cov_t)


def _dot_nt(a, b):
    return lax.dot_general(a, b, (((1,), (1,)), ((), ())), preferred_element_type=F32)


def _page_copies(pt_ref, pools, bufs, sems, batch, slot, dst):
    n_pages = pt_ref.shape[1]
    out = []
    for i, (pool, buf) in enumerate(zip(pools, bufs)):
        for p in range(n_pages):
            page = 0 if batch is None else pt_ref[batch, p]
            out.append(pltpu.make_async_copy(pool.at[page], dst(buf, slot, p), sems.at[i, slot]))
    return out


def _gather_step(pt_ref, pools, bufs, sems, dst):
    b = pl.program_id(0)
    slot = b % 2

    @pl.when(b == 0)
    def _():
        for cp in _page_copies(pt_ref, pools, bufs, sems, 0, 0, dst):
            cp.start()

    for cp in _page_copies(pt_ref, pools, bufs, sems, None, slot, dst):
        cp.wait()
    nxt = jnp.minimum(b + 1, pl.num_programs(0) - 1)
    for cp in _page_copies(pt_ref, pools, bufs, sems, nxt, 1 - slot, dst):
        cp.start()
    return slot


def _gather_drain(pt_ref, pools, bufs, sems, dst, slot):
    @pl.when(pl.program_id(0) == pl.num_programs(0) - 1)
    def _():
        for cp in _page_copies(pt_ref, pools, bufs, sems, None, 1 - slot, dst):
            cp.wait()


def _position_bias(pos_ref, wexp_ref):
    half = wexp_ref.shape[1] // 2
    pp = jnp.dot(pos_ref[...], wexp_ref[...], preferred_element_type=F32)
    return pp[0:1, :half] + pp[1:2, half:]


def _regroup_pages(buf, slot, perm_ref, x_ref):
    n_pages = buf.shape[1]
    kvw = buf.shape[2]
    rpp = PAGE_SIZE // ROW_TOKENS
    nr = n_pages * rpp
    for i in range(n_pages // 2):
        pair = buf[slot, 2 * i:2 * i + 2].reshape(2 * kvw, PAGE_SIZE).astype(BF16)
        xt = _dot_nt(perm_ref[...], pair)
        for pp in range(2):
            for l in range(ROW_TOKENS):
                row0 = (l // 2) * nr + (2 * i + pp) * rpp
                x_ref[row0:row0 + rpp, (l % 2) * kvw:(l % 2 + 1) * kvw] = (
                    xt[l * rpp:(l + 1) * rpp, pp * kvw:(pp + 1) * kvw])


def _compress_contract(x_ref, wexp_ref):
    nr = x_ref.shape[0] // (ROW_TOKENS // 2)
    kvw = x_ref.shape[1] // 2
    p = None
    for i in range(ROW_TOKENS // 2):
        part = jnp.dot(x_ref[i * nr:(i + 1) * nr, :].astype(BF16), wexp_ref[2 * i * kvw:2 * (i + 1) * kvw, :],
                       preferred_element_type=F32)
        p = part if p is None else p + part
    return p


def _compress_finish(p, w2_ref, posb, sh_ref):
    nr = p.shape[0]
    half = p.shape[1] // 2
    sh_ref[0:nr, :] = p[:, half:]
    sh_ref[nr:nr + 8, :] = jnp.zeros((8, half), F32)
    h = p[:, :half] + sh_ref[1:nr + 1, :] + posb
    return jnp.dot(jax.nn.gelu(h).astype(BF16), w2_ref[...], preferred_element_type=F32).astype(BF16)


def _sample_cmp_body(pt_ref, kpool_ref, vpool_ref, qbd_ref, tab_ref, perm_ref, wk_ref, w2k_ref, pk_ref, wv_ref, w2v_ref,
                     pv_ref, cov_ref, oc_ref, imp_ref, kbuf, vbuf, sems, xk_ref, xv_ref, shk_ref, shv_ref, posb_ref,
                     *, rep, t_new):
    page_dst = lambda buf, sl, p: buf.at[sl, p]
    slot = _gather_step(pt_ref, (kpool_ref, vpool_ref), (kbuf, vbuf), sems, page_dst)

    @pl.when(pl.program_id(0) == 0)
    def _():
        posb_ref[0:1, :] = _position_bias(pk_ref, wk_ref)
        posb_ref[1:2, :] = _position_bias(pv_ref, wv_ref)

    _regroup_pages(kbuf, slot, perm_ref, xk_ref)
    _regroup_pages(vbuf, slot, perm_ref, xv_ref)
    pk = _compress_contract(xk_ref, wk_ref)
    pv = _compress_contract(xv_ref, wv_ref)
    kc = _compress_finish(pk, w2k_ref, posb_ref[0:1, :], shk_ref)
    vc = _compress_finish(pv, w2v_ref, posb_ref[1:2, :], shv_ref)
    s = _dot_nt(qbd_ref[0], kc) + tab_ref[...]
    m = jnp.max(s, axis=1, keepdims=True)
    p = jnp.where(s > 0.1 * NEG, jnp.exp2(s - m), 0.0)
    l = jnp.sum(p, axis=1, keepdims=True)
    pn = (p * jnp.where(l > 0.0, 1.0 / l, 0.0)).astype(BF16)
    oc_ref[0] = jnp.dot(pn, vc, preferred_element_type=F32)
    imp = jnp.dot(pn, cov_ref[...], preferred_element_type=F32)
    per_g = rep * t_new
    parts = []
    for g in range(N_KV):
        acc = imp[g * per_g:g * per_g + t_new]
        for r in range(1, rep):
            acc = acc + imp[g * per_g + r * t_new:g * per_g + (r + 1) * t_new]
        parts.append(acc)
    imp_ref[0] = jnp.concatenate(parts, axis=0)
    _gather_drain(pt_ref, (kpool_ref, vpool_ref), (kbuf, vbuf), sems, page_dst, slot)


def _sample_cmp(page_table, kpool_t, vpool_t, qbd, tab, cmp_k, cmp_v, cov, *, rep, t_new):
    nb, n_pages = page_table.shape
    rpp = PAGE_SIZE // ROW_TOKENS
    n_rows = n_pages * rpp
    n_q = qbd.shape[1]
    kvw = N_KV * HEAD_DIM
    tok = np.arange(PAGE_SIZE)
    perm = jnp.asarray((tok[None, :] == (tok[:, None] % rpp) * ROW_TOKENS + tok[:, None] // rpp).astype(np.float32), BF16)
    const = lambda a: pl.BlockSpec(a.shape, lambda b, pt: (0,) * a.ndim, pipeline_mode=pl.Buffered(1))
    grid_spec = pltpu.PrefetchScalarGridSpec(
        num_scalar_prefetch=1,
        grid=(nb,),
        in_specs=[pl.BlockSpec(memory_space=pl.ANY), pl.BlockSpec(memory_space=pl.ANY),
                  pl.BlockSpec((1, n_q, kvw), lambda b, pt: (b, 0, 0)), const(tab), const(perm),
                  const(cmp_k[0]), const(cmp_k[1]), const(cmp_k[2]), const(cmp_v[0]), const(cmp_v[1]), const(cmp_v[2]),
                  const(cov)],
        out_specs=[pl.BlockSpec((1, n_q, kvw), lambda b, pt: (b, 0, 0)),
                   pl.BlockSpec((1, N_KV * t_new, cov.shape[1]), lambda b, pt: (b, 0, 0))],
        scratch_shapes=[pltpu.VMEM((2, n_pages, kvw, PAGE_SIZE), F32), pltpu.VMEM((2, n_pages, kvw, PAGE_SIZE), F32),
                        pltpu.SemaphoreType.DMA((2, 2)),
                        pltpu.VMEM((n_pages * PAGE_SIZE // 2, 2 * kvw), F32),
                        pltpu.VMEM((n_pages * PAGE_SIZE // 2, 2 * kvw), F32),
                        pltpu.VMEM((n_rows + 8, kvw), F32), pltpu.VMEM((n_rows + 8, kvw), F32),
                        pltpu.VMEM((8, kvw), F32)],
    )
    return pl.pallas_call(
        functools.partial(_sample_cmp_body, rep=rep, t_new=t_new),
        grid_spec=grid_spec,
        out_shape=[jax.ShapeDtypeStruct((nb, n_q, kvw), F32),
                   jax.ShapeDtypeStruct((nb, N_KV * t_new, cov.shape[1]), F32)],
        compiler_params=_cparams(1),
        name="sample_cmp",
    )(page_table, kpool_t, vpool_t, qbd, tab, perm, *cmp_k, *cmp_v, cov)


def _select_body(imp_ref, o_ref, *, top_n, n_blk, pos0, t_new):
    rows, cols = imp_ref.shape
    j = lax.broadcasted_iota(jnp.int32, (rows, cols), 0)
    col = pl.program_id(0) * cols + lax.broadcasted_iota(jnp.int32, (rows, cols), 1)
    cur = (pos0 + col % t_new) // L_SEL
    forced = (j == 0) | (j == cur) | (j == cur - 1)
    v = jnp.where(forced, 1e9, jnp.where(j <= cur, imp_ref[...], -1e9))
    v = jnp.where(j < n_blk, v, -3e38)
    rank = _descending_rank(v, n_blk)
    o_ref[...] = jnp.where((rank < top_n) & (j <= cur) & (j < n_blk), 0.0, NEG)


def _select_blocks(imp_t, *, top_n, n_blk, pos0, t_new):
    rows, cols = imp_t.shape
    tile = 4 * LANE
    return pl.pallas_call(
        functools.partial(_select_body, top_n=top_n, n_blk=n_blk, pos0=pos0, t_new=t_new),
        grid=(cols // tile,),
        in_specs=[pl.BlockSpec((rows, tile), lambda i: (0, i))],
        out_specs=pl.BlockSpec((rows, tile), lambda i: (0, i)),
        out_shape=jax.ShapeDtypeStruct((rows, cols), F32),
        compiler_params=_cparams(1),
        name="sample_select",
    )(imp_t)


def _softmax_pv(parts):
    m = None
    for s, _, _ in parts:
        mk = jnp.max(s, axis=1, keepdims=True)
        m = mk if m is None else jnp.maximum(m, mk)
    l = None
    o = None
    for s, v, channel_major in parts:
        p = jnp.exp2(s - m)
        lk = jnp.sum(p, axis=1, keepdims=True)
        pb = p.astype(BF16)
        ok = _dot_nt(pb, v) if channel_major else jnp.dot(pb, v, preferred_element_type=F32)
        l = lk if l is None else l + lk
        o = ok if o is None else o + ok
    return o / l


def _sample_slc_body(pt_ref, kpool_ref, vpool_ref, qbd_ref, selb_ref, gate_ref, oc_ref, ksn_ref, vsn_ref, kwn_ref,
                     vwn_ref, kwin_ref, vwin_ref, near_ref, newtab_ref, wtab_ref, expand_ref,
                     o_ref, kbuf, vbuf, sems, pad_ref, *, rep, t_new):
    b = pl.program_id(0)
    page_dst = lambda buf, sl, p: buf.at[sl, :, pl.ds(p * PAGE_SIZE, PAGE_SIZE)]
    slot = _gather_step(pt_ref, (kpool_ref, vpool_ref), (kbuf, vbuf), sems, page_dst)
    hd = HEAD_DIM
    kvw = N_KV * hd

    @pl.when(b == 0)
    def _():
        pad_ref[...] = jnp.zeros(pad_ref.shape, F32)

    news = []
    for i, ref in enumerate((ksn_ref, vsn_ref, kwn_ref, vwn_ref)):
        pad_ref[i, 0:t_new, :] = ref[0]
        news.append(pad_ref[i].astype(BF16))
    ks_new, vs_new, kw_new, vw_new = news
    qbd = qbd_ref[0]

    per_g = rep * t_new
    sel = selb_ref[0]
    sel_rows = jnp.concatenate([sel[g * t_new:(g + 1) * t_new] for g in range(N_KV) for _ in range(rep)], axis=0)
    n_past_blk = expand_ref.shape[0]
    s_past = jnp.dot(jnp.concatenate([qbd, sel_rows[:, :n_past_blk].astype(BF16)], axis=1),
                     jnp.concatenate([kbuf[slot].astype(BF16), expand_ref[...]], axis=0),
                     preferred_element_type=F32)
    n_past = s_past.shape[1]
    n_near = near_ref.shape[1]
    s_past = jnp.concatenate([s_past[:, :n_past - n_near], s_past[:, n_past - n_near:] + near_ref[...]], axis=1)
    s_new = _dot_nt(qbd, ks_new) + newtab_ref[0]
    s_win = jnp.dot(qbd, kwin_ref[0].astype(BF16), preferred_element_type=F32) + wtab_ref[...]
    s_wnew = _dot_nt(qbd, kw_new) + newtab_ref[1]
    o_s = _softmax_pv([(s_past, vbuf[slot].astype(BF16), True), (s_new, vs_new, False)])
    o_w = _softmax_pv([(s_win, vwin_ref[0].astype(BF16), True), (s_wnew, vw_new, False)])

    gate = gate_ref[0]
    comb = gate[:, 0:1] * oc_ref[0] + gate[:, 1:2] * o_s + gate[:, 2:3] * o_w
    row = lax.broadcasted_iota(jnp.int32, comb.shape, 0)
    o_ref[0] = jnp.where(row < per_g, comb, pltpu.roll(comb, hd, axis=1))
    _gather_drain(pt_ref, (kpool_ref, vpool_ref), (kbuf, vbuf), sems, page_dst, slot)


def _sample_slc(page_table, kpool, vpool, qbd, selb, gate_rows, oc, ks_new, vs_new, kw_new, vw_new, kwin, vwin,
                near, newtab, wtab, expand, *, rep, t_new):
    nb, n_pages = page_table.shape
    n_keys = n_pages * PAGE_SIZE
    n_q = qbd.shape[1]
    kvw = N_KV * HEAD_DIM
    const = lambda a: pl.BlockSpec(a.shape, lambda b, pt: (0,) * a.ndim, pipeline_mode=pl.Buffered(1))
    per_b = lambda a: pl.BlockSpec((1,) + a.shape[1:], lambda b, pt: (b,) + (0,) * (a.ndim - 1))
    grid_spec = pltpu.PrefetchScalarGridSpec(
        num_scalar_prefetch=1,
        grid=(nb,),
        in_specs=[pl.BlockSpec(memory_space=pl.ANY), pl.BlockSpec(memory_space=pl.ANY),
                  per_b(qbd), per_b(selb), per_b(gate_rows), per_b(oc),
                  per_b(ks_new), per_b(vs_new), per_b(kw_new), per_b(vw_new), per_b(kwin), per_b(vwin),
                  const(near), const(newtab), const(wtab), const(expand)],
        out_specs=pl.BlockSpec((1, n_q, kvw), lambda b, pt: (b, 0, 0)),
        scratch_shapes=[pltpu.VMEM((2, kvw, n_keys), F32), pltpu.VMEM((2, kvw, n_keys), F32),
                        pltpu.SemaphoreType.DMA((2, 2)), pltpu.VMEM((4, LANE, kvw), F32)],
    )
    return pl.pallas_call(
        functools.partial(_sample_slc_body, rep=rep, t_new=t_new),
        grid_spec=grid_spec,
        out_shape=jax.ShapeDtypeStruct((nb, n_q, kvw), F32),
        compiler_params=_cparams(1),
        name="sample_slc_win",
    )(page_table, kpool, vpool, qbd, selb, gate_rows, oc, ks_new, vs_new, kw_new, vw_new, kwin, vwin,
      near, newtab, wtab, expand)


def _sample_tables(rel_bias, pos0, t_new, n_rows, n_c, wbuf):
    h = rel_bias.shape[0]
    qpos = pos0 + jnp.arange(t_new, dtype=jnp.int32)

    def table(kpos, ok_extra=None, upper=None):
        d = qpos[:, None] - kpos[None, :]
        ok = d >= 0
        if upper is not None:
            ok = ok & (d < upper)
        if ok_extra is not None:
            ok = ok & ok_extra[None, :]
        tab = jnp.where(ok[None], _bias_minus_far(rel_bias, d, 0), NEG)
        return tab.reshape(h * t_new, kpos.shape[0]).astype(F32)

    c = jnp.arange(n_rows, dtype=jnp.int32)
    cmp = table(c * STRIDE + (L_CMP - 1), ok_extra=c < n_c)
    near = table(pos0 - PAGE_SIZE + jnp.arange(PAGE_SIZE, dtype=jnp.int32))
    lane = jnp.arange(LANE, dtype=jnp.int32)
    new_slc = table(pos0 + lane, ok_extra=lane < t_new)
    new_win = table(pos0 + lane, ok_extra=lane < t_new, upper=WINDOW)
    kpos_w = pos0 - wbuf + jnp.arange(wbuf, dtype=jnp.int32)
    win = table(kpos_w, ok_extra=kpos_w >= 0, upper=WINDOW)
    return cmp, near, jnp.stack([new_slc, new_win]), win


def _layer_weights(p, n_heads):
    d = p["w_in"].shape[0]
    kv_end = d + 6 * N_KV * HEAD_DIM
    src = _gate_column_source(n_heads)
    gate_cols = jnp.where(jnp.asarray(src >= 0)[None, :], p["w_in"][:, kv_end + np.maximum(src, 0)], 0.0)
    ln = jnp.stack([p["ln_gain"][0], p["ln_bias"][0], p["ln_gain"][1], p["ln_bias"][1],
                    p["ln_gain"][2], p["ln_bias"][2]])
    return dict(
        wi1=p["w_ffn1_in"].astype(BF16), wo1=p["w_ffn1_out"].astype(BF16),
        wi2=p["w_ffn2_in"].astype(BF16), wo2=p["w_ffn2_out"].astype(BF16),
        wp=jnp.concatenate([p["w_in"][:, :kv_end], gate_cols], axis=1).astype(BF16),
        wout=p["w_out"].astype(BF16),
        wpool=p["w_pool"].astype(BF16), pool_scale=p["pool_scale"],
        ln1=ln[0:2], ln23=ln[2:6],
        cmp_k=_compress_weights(p["w_phi_k1"], p["w_phi_k2"], p["cmp_pos"]),
        cmp_v=_compress_weights(p["w_phi_v1"], p["w_phi_v2"], p["cmp_pos"]),
    )


PROMPT_TOKENS = 512


def _prompt_layer(x, mod, w, rel_bias, alpha):
    nb, t, d = x.shape
    assert t % TQ == 0 and t % PROMPT_TOKENS == 0
    (x1, u_pool, q, k_c, v_c, k_s, v_s, k_w, v_w, ks_bf, vs_t, kw_bf, vw_t, gates_t) = _ffn_in(
        x, mod, w["wi1"], w["wo1"], w["wp"], w["ln1"], alpha=alpha, nbk=1, tt=PROMPT_TOKENS, for_prompt=True)
    pool_out = _pool_mix(u_pool, None, w["wpool"], w["pool_scale"], pos0=0, nbk=1, tt=PROMPT_TOKENS,
                         name="pool_prompt")
    kc_cmp = _compress(k_c, *w["cmp_k"], transposed=False, name="compress_k_prompt")
    vc_cmp_t = _compress(v_c, *w["cmp_v"], transposed=True, name="compress_v_prompt")
    n_c = (t - L_CMP) // STRIDE + 1
    n_blk = t // L_SEL
    near, tcmp, e0 = _prompt_tables(rel_bias, t)
    nsa = _nsa_prompt(q, gates_t, kc_cmp, vc_cmp_t, ks_bf, vs_t, kw_bf, vw_t, tcmp, near,
                      _cover_t(t // ROW_TOKENS, n_c, n_blk), e0=e0, top_n=min(TOP_N, n_blk))
    y = _ffn_out(pool_out, nsa, x1, mod, w["wout"], w["wi2"], w["wo2"], w["ln23"], alpha=alpha,
                 nbk=1, tt=PROMPT_TOKENS, name="outproj_ffn2_prompt")
    keep = min(WINDOW, t)
    heads = lambda a: a.reshape(nb, -1, N_KV, HEAD_DIM)
    states = (heads(k_c), heads(v_c), heads(k_s), heads(v_s), heads(k_w[:, t - keep:]), heads(v_w[:, t - keep:]),
              u_pool[:, t - POOL_STATE:])
    return y, states


SAMPLE_BATCHES = 32


def _sample_layer(x, mod, w, rel_bias, alpha, page_table, pools, kwin, vwin, pool_state):
    nb, t, d = x.shape
    hd = HEAD_DIM
    n_heads = rel_bias.shape[0]
    rep = n_heads // N_KV
    kvw = N_KV * hd
    pos0 = page_table.shape[1] * PAGE_SIZE
    n_c = (pos0 + t - L_CMP) // STRIDE + 1
    assert (n_c - 1) * STRIDE + L_CMP <= pos0 and pos0 % PAGE_SIZE == 0 and nb % SAMPLE_BATCHES == 0
    n_rows = pos0 // ROW_TOKENS
    n_blk = -(-(pos0 + t) // L_SEL)
    assert n_blk - 1 == pos0 // L_SEL
    blk_pad = -(-n_blk // LANE) * LANE
    wbuf = kwin.shape[1]

    x1, u_pool, q, k_c, v_c, k_s, v_s, k_w, v_w, gates = _ffn_in(
        x, mod, w["wi1"], w["wo1"], w["wp"], w["ln1"], alpha=alpha, nbk=SAMPLE_BATCHES, tt=t, for_prompt=False)
    pw = u_pool.shape[-1]
    prev = jnp.concatenate([jnp.zeros((nb, PREV_ROWS - POOL_STATE, pw), F32), pool_state], axis=1)
    pool_out = _pool_mix(u_pool, prev, w["wpool"], w["pool_scale"], pos0=pos0, nbk=SAMPLE_BATCHES, tt=t,
                         name="pool_sample")

    qh = q.reshape(nb, t, N_KV, rep, hd).transpose(0, 2, 3, 1, 4)
    own = jnp.eye(N_KV, dtype=q.dtype)[None, :, None, None, :, None]
    qbd = (qh[:, :, :, :, None, :] * own).reshape(nb, N_KV * rep * t, kvw)
    gate_rows = gates[:, :, :N_KV * GATE_ROWS].reshape(nb, t, N_KV, GATE_ROWS)[..., :3 * rep]
    gate_rows = gate_rows.reshape(nb, t, N_KV, 3, rep).transpose(0, 2, 4, 1, 3).reshape(nb, N_KV * rep * t, 3)
    gate_rows = jnp.pad(gate_rows, ((0, 0), (0, 0), (0, LANE - 3)))

    tab_c, near, newtab, wtab = _sample_tables(rel_bias, pos0, t, n_rows, n_c, wbuf)
    cov = np.zeros((n_rows, blk_pad), np.float32)
    ci = np.arange(n_rows)[:, None]
    bj = np.arange(blk_pad)[None, :]
    cov[:] = ((ci * STRIDE <= (bj + 1) * L_SEL - 1) & (ci * STRIDE + L_CMP - 1 >= bj * L_SEL)
              & (ci < n_c) & (bj < n_blk))
    channel_major = lambda a: a.transpose(0, 2, 3, 1).reshape(a.shape[0], kvw, a.shape[1])
    oc, imp = _sample_cmp(page_table, channel_major(pools[0]), channel_major(pools[1]), qbd, tab_c, w["cmp_k"],
                          w["cmp_v"], jnp.asarray(cov, BF16), rep=rep, t_new=t)

    sel_rows = -(-n_blk // 8) * 8
    imp_t = imp.transpose(2, 0, 1).reshape(blk_pad, nb * N_KV * t)[:sel_rows]
    selb_t = _select_blocks(imp_t, top_n=min(TOP_N, n_blk), n_blk=n_blk, pos0=pos0, t_new=t)
    selb_t = jnp.pad(selb_t, ((0, blk_pad - sel_rows), (0, 0)), constant_values=NEG)
    selb = selb_t.reshape(blk_pad, nb, N_KV * t).transpose(1, 2, 0)

    key_blk = lax.broadcasted_iota(jnp.int32, (pos0 // L_SEL, pos0), 1) // L_SEL
    expand = (key_blk == lax.broadcasted_iota(jnp.int32, (pos0 // L_SEL, pos0), 0)).astype(BF16)
    kvf = lambda a: a.reshape(a.shape[0], a.shape[1], kvw)
    o = _sample_slc(page_table, channel_major(pools[2]), channel_major(pools[3]), qbd, selb, gate_rows, oc,
                    k_s, v_s, k_w, v_w, channel_major(kwin), channel_major(vwin), near, newtab, wtab, expand,
                    rep=rep, t_new=t)
    nsa = o[:, :, :hd].reshape(nb, N_KV, rep, t, hd).transpose(0, 3, 1, 2, 4).reshape(nb, t, n_heads * hd).astype(BF16)

    y = _ffn_out(pool_out, nsa, x1, mod, w["wout"], w["wi2"], w["wo2"], w["ln23"], alpha=alpha,
                 nbk=SAMPLE_BATCHES, tt=t, name="outproj_ffn2_sample")
    keep = min(WINDOW, pos0 + t)
    heads = lambda a: a.reshape(nb, -1, N_KV, HEAD_DIM)
    kw_ext = jnp.concatenate([kvf(kwin), k_w], axis=1)
    vw_ext = jnp.concatenate([kvf(vwin), v_w], axis=1)
    pool_ext = jnp.concatenate([pool_state, u_pool], axis=1)
    states = (heads(k_c), heads(v_c), heads(k_s), heads(v_s), heads(kw_ext[:, kw_ext.shape[1] - keep:]),
              heads(vw_ext[:, vw_ext.shape[1] - keep:]), pool_ext[:, pool_ext.shape[1] - POOL_STATE:])
    return y, states


def kernel(x_prompt, x_sample, c_prompt, c_sample, cache_k_cmp, cache_v_cmp, cache_k_slc, cache_v_slc, page_table,
           state_k_win, state_v_win, state_pool, w_ada, b_ada, ln_gain, ln_bias, w_ffn1_in, w_ffn1_out, w_ffn2_in,
           w_ffn2_out, w_in, w_phi_k1, w_phi_k2, w_phi_v1, w_phi_v2, cmp_pos, w_pool, pool_scale, w_out, rel_bias):
    depth = w_ada.shape[0]
    alpha = (2.0 * depth) ** 0.25
    n_heads = rel_bias.shape[0]
    nb_p, nb_s = x_prompt.shape[0], x_sample.shape[0]
    d = x_prompt.shape[-1]
    xp, xs = x_prompt, x_sample
    c_all = jnp.concatenate([c_prompt, c_sample], axis=0)
    p_states, s_states = [], []
    for l in range(depth):
        p = dict(w_ada=w_ada[l], b_ada=b_ada[l], ln_gain=ln_gain[l], ln_bias=ln_bias[l], w_ffn1_in=w_ffn1_in[l],
                 w_ffn1_out=w_ffn1_out[l], w_ffn2_in=w_ffn2_in[l], w_ffn2_out=w_ffn2_out[l], w_in=w_in[l],
                 w_phi_k1=w_phi_k1[l], w_phi_k2=w_phi_k2[l], w_phi_v1=w_phi_v1[l], w_phi_v2=w_phi_v2[l],
                 cmp_pos=cmp_pos[l], w_pool=w_pool[l], pool_scale=pool_scale[l], w_out=w_out[l])
        w = _layer_weights(p, n_heads)
        mod = _ada(c_all, p["w_ada"], p["b_ada"]).reshape(nb_p + nb_s, 9, d)
        xp, sp = _prompt_layer(xp, mod[:nb_p], w, rel_bias, alpha)
        xs, ss = _sample_layer(xs, mod[nb_p:], w, rel_bias, alpha, page_table,
                               (cache_k_cmp[l], cache_v_cmp[l], cache_k_slc[l], cache_v_slc[l]),
                               state_k_win[l], state_v_win[l], state_pool[l])
        p_states.append(sp)
        s_states.append(ss)
    stack = lambda states: tuple(jnp.stack(a) for a in zip(*states))
    return (xp, xs) + stack(p_states) + stack(s_states)
```

```python
import functools
import math

import numpy as np
import jax
import jax.numpy as jnp
from jax import lax
from jax.experimental import pallas as pl
from jax.experimental.pallas import tpu as pltpu

F32 = jnp.float32
BF16 = jnp.bfloat16

HEAD_DIM = 64
N_KV = 2
L_CMP = 32
STRIDE = 16
L_SEL = 64
TOP_N = 16
WINDOW = 512
N_BUCKETS = 32
MAX_DIST = 128
POOL_WINDOWS = (2, 4, 8, 16)
POOL_STATE = max(POOL_WINDOWS) - 1
LN_EPS = 1e-5
NEG = -1e30
PAGE_SIZE = 128

Q_SCALE = HEAD_DIM ** -0.5 * math.log2(math.e)
TQ = 128
FAR_CHUNKS = 4
FAR_BLOCKS = FAR_CHUNKS * TQ // L_SEL
VT_ROWS = HEAD_DIM + 16

LANE = 128
MXU_COLS = 256
FF_CHUNK = MXU_COLS
VMEM_LIMIT = 56 * 1024 * 1024


def _cparams(n_axes):
    return pltpu.CompilerParams(dimension_semantics=("arbitrary",) * n_axes,
                                vmem_limit_bytes=VMEM_LIMIT)


def _const_spec(shape):
    nd = len(shape)
    return pl.BlockSpec(shape, lambda *_: (0,) * nd, pipeline_mode=pl.Buffered(1))


def _layer_norm(y, gain, bias):
    mu = jnp.mean(y, axis=-1, keepdims=True)
    yc = y - mu
    var = jnp.mean(yc * yc, axis=-1, keepdims=True)
    return yc * lax.rsqrt(var + LN_EPS) * gain + bias


def _swiglu(u_bf, wi_ref, wo_ref):
    d_ff = wo_ref.shape[0]
    acc = None
    for c in range(d_ff // FF_CHUNK):
        lo = c * FF_CHUNK
        a = jnp.dot(u_bf, wi_ref[:, lo:lo + FF_CHUNK], preferred_element_type=F32)
        b = jnp.dot(u_bf, wi_ref[:, d_ff + lo:d_ff + lo + FF_CHUNK], preferred_element_type=F32)
        hid = (a * jax.nn.sigmoid(a) * b).astype(BF16)
        part = jnp.dot(hid, wo_ref[lo:lo + FF_CHUNK, :], preferred_element_type=F32)
        acc = part if acc is None else acc + part
    return acc


def _ada_body(c_ref, w_ref, b_ref, o_ref):
    c = c_ref[...]
    act = (c * jax.nn.sigmoid(c)).astype(BF16)
    o_ref[...] = jnp.dot(act, w_ref[...].astype(BF16), preferred_element_type=F32) + b_ref[...]


def _ada(c_all, w_ada, b_ada):
    nb, d = c_all.shape
    n_out = w_ada.shape[1]
    cols = 9 * LANE
    return pl.pallas_call(
        _ada_body,
        grid=(n_out // cols,),
        in_specs=[pl.BlockSpec((nb, d), lambda j: (0, 0)),
                  pl.BlockSpec((d, cols), lambda j: (0, j)),
                  pl.BlockSpec((1, cols), lambda j: (0, j))],
        out_specs=pl.BlockSpec((nb, cols), lambda j: (0, j)),
        out_shape=jax.ShapeDtypeStruct((nb, n_out), F32),
        compiler_params=_cparams(1),
        name="ada",
    )(c_all, w_ada, b_ada.reshape(1, n_out))


def _ffn_in_body(x_ref, mod_ref, wi_ref, wo_ref, wp_ref, ln_ref, *out_refs, alpha, pool_w, nsa_w, kv_w, for_prompt):
    nbk, tt, d = x_ref.shape
    n = nbk * tt
    x = x_ref[...]
    mod = mod_ref[...]
    u = (x * (1.0 + mod[:, 1:2, :]) + mod[:, 0:1, :]).reshape(n, d).astype(BF16)
    h = _swiglu(u, wi_ref, wo_ref).reshape(nbk, tt, d)
    x1 = _layer_norm(alpha * x + 0.5 * mod[:, 2:3, :] * h, ln_ref[0:1, :], ln_ref[1:2, :])
    u1 = (x1 * (1.0 + mod[:, 4:5, :]) + mod[:, 3:4, :]).reshape(n, d).astype(BF16)
    proj = jnp.dot(u1, wp_ref[...], preferred_element_type=F32)

    x1_ref, upool_ref, q_ref = out_refs[:3]
    kv_refs = out_refs[3:9]
    x1_ref[...] = x1
    upool_ref[...] = proj[:, :pool_w].reshape(nbk, tt, pool_w)
    q_ref[...] = (proj[:, pool_w:pool_w + nsa_w] * Q_SCALE).astype(BF16).reshape(nbk, tt, nsa_w)
    off = pool_w + nsa_w
    kv = []
    for i in range(6):
        blk = proj[:, off + i * kv_w:off + (i + 1) * kv_w]
        kv.append(blk)
        kv_refs[i][...] = blk.reshape(nbk, tt, kv_w)
    gates = jax.nn.sigmoid(proj[:, off + 6 * kv_w:off + 6 * kv_w + LANE])
    if for_prompt:
        ksb_ref, vst_ref, kwb_ref, vwt_ref, gt_ref = out_refs[9:]
        row = lax.broadcasted_iota(jnp.int32, (n, kv_w), 0)
        lane = lax.broadcasted_iota(jnp.int32, (n, kv_w), 1)
        onehot = jnp.where((row // L_SEL) % FAR_BLOCKS == lane, 1.0, 0.0)
        ones_row = jnp.where(lax.broadcasted_iota(jnp.int32, (VT_ROWS - HEAD_DIM, n), 0) == 0, 1.0, 0.0)

        def values_t(x):
            xt = x.T
            return jnp.concatenate([piece for g in range(N_KV)
                                    for piece in (xt[g * HEAD_DIM:(g + 1) * HEAD_DIM], ones_row)], axis=0).astype(BF16)

        ksb_ref[0] = jnp.concatenate([kv[2], onehot], axis=1).astype(BF16)
        vst_ref[0] = values_t(kv[3])
        kwb_ref[0] = kv[4].astype(BF16)
        vwt_ref[0] = values_t(kv[5])
        gt_ref[0] = gates.T[:gt_ref.shape[1], :]
    else:
        out_refs[9][...] = gates.reshape(nbk, tt, LANE)


def _ffn_in(x, mod, wi, wo, wp, ln, *, alpha, nbk, tt, for_prompt):
    nb, t, d = x.shape
    pool_w = d // 2
    nsa_w = d - pool_w
    kv_w = N_KV * HEAD_DIM
    grid = (nb // nbk, t // tt)
    tok = lambda w: pl.BlockSpec((nbk, tt, w), lambda i, j: (i, j, 0))
    out_specs = [tok(d), tok(pool_w), tok(nsa_w)] + [tok(kv_w)] * 6
    out_shape = ([jax.ShapeDtypeStruct((nb, t, d), F32), jax.ShapeDtypeStruct((nb, t, pool_w), F32),
                  jax.ShapeDtypeStruct((nb, t, nsa_w), BF16)]
                 + [jax.ShapeDtypeStruct((nb, t, kv_w), F32)] * 6)
    if for_prompt:
        assert nbk == 1 and tt % (FAR_BLOCKS * L_SEL) == 0
        tr = lambda rows: pl.BlockSpec((1, rows, tt), lambda i, j: (i, 0, j))
        vt_rows = N_KV * VT_ROWS
        out_specs += [tok(2 * kv_w), tr(vt_rows), tok(kv_w), tr(vt_rows), tr(N_KV * GATE_ROWS)]
        out_shape += [jax.ShapeDtypeStruct((nb, t, 2 * kv_w), BF16), jax.ShapeDtypeStruct((nb, vt_rows, t), BF16),
                      jax.ShapeDtypeStruct((nb, t, kv_w), BF16), jax.ShapeDtypeStruct((nb, vt_rows, t), BF16),
                      jax.ShapeDtypeStruct((nb, N_KV * GATE_ROWS, t), F32)]
    else:
        out_specs += [tok(LANE)]
        out_shape += [jax.ShapeDtypeStruct((nb, t, LANE), F32)]
    body = functools.partial(_ffn_in_body, alpha=alpha, pool_w=pool_w, nsa_w=nsa_w, kv_w=kv_w,
                             for_prompt=for_prompt)
    return pl.pallas_call(
        body,
        grid=grid,
        in_specs=[tok(d),
                  pl.BlockSpec((nbk, mod.shape[1], d), lambda i, j: (i, 0, 0)),
                  _const_spec(wi.shape), _const_spec(wo.shape), _const_spec(wp.shape), _const_spec(ln.shape)],
        out_specs=out_specs,
        out_shape=out_shape,
        compiler_params=_cparams(2),
        name="ffn1_inproj_prompt" if for_prompt else "ffn1_inproj_sample",
    )(x, mod, wi, wo, wp, ln)


GATE_ROWS = 16


def _gate_column_source(n_heads):
    rep = n_heads // N_KV
    src = np.full((LANE,), -1, np.int32)
    for g in range(N_KV):
        for br in range(3):
            for r in range(rep):
                src[g * GATE_ROWS + br * rep + r] = br * n_heads + g * rep + r
    return src


def _ffn_out_body(pool_ref, nsa_ref, x1_ref, mod_ref, wout_ref, wi_ref, wo_ref, ln_ref, o_ref, *, alpha):
    nbk, tt, d = x1_ref.shape
    n = nbk * tt
    pw = pool_ref.shape[-1]
    mod = mod_ref[...]
    x1 = x1_ref[...]
    mix = (jnp.dot(pool_ref[...].reshape(n, pw), wout_ref[:pw, :], preferred_element_type=F32)
           + jnp.dot(nsa_ref[...].reshape(n, d - pw), wout_ref[pw:, :], preferred_element_type=F32))
    x2 = _layer_norm(alpha * x1 + mod[:, 5:6, :] * mix.reshape(nbk, tt, d), ln_ref[0:1, :], ln_ref[1:2, :])
    u = (x2 * (1.0 + mod[:, 7:8, :]) + mod[:, 6:7, :]).reshape(n, d).astype(BF16)
    h = _swiglu(u, wi_ref, wo_ref).reshape(nbk, tt, d)
    o_ref[...] = _layer_norm(alpha * x2 + 0.5 * mod[:, 8:9, :] * h, ln_ref[2:3, :], ln_ref[3:4, :])


def _ffn_out(pool_out, nsa_out, x1, mod, wout, wi, wo, ln, *, alpha, nbk, tt, name):
    nb, t, d = x1.shape
    tok = lambda w: pl.BlockSpec((nbk, tt, w), lambda i, j: (i, j, 0))
    return pl.pallas_call(
        functools.partial(_ffn_out_body, alpha=alpha),
        grid=(nb // nbk, t // tt),
        in_specs=[tok(pool_out.shape[-1]), tok(nsa_out.shape[-1]), tok(d),
                  pl.BlockSpec((nbk, mod.shape[1], d), lambda i, j: (i, 0, 0)),
                  _const_spec(wout.shape), _const_spec(wi.shape), _const_spec(wo.shape), _const_spec(ln.shape)],
        out_specs=tok(d),
        out_shape=jax.ShapeDtypeStruct((nb, t, d), F32),
        compiler_params=_cparams(2),
        name=name,
    )(pool_out, nsa_out, x1, mod, wout, wi, wo, ln)


PREV_ROWS = 16


def _pool_body(u_ref, prev_ref, w_ref, scale_ref, o_ref, ext_ref, *, pos0, zero_first):
    nbk, tt, width = u_ref.shape
    gw = width // len(POOL_WINDOWS)
    j = pl.program_id(1)
    u = u_ref[...]
    prev = prev_ref[...]
    if zero_first:
        prev = jnp.where(j == 0, 0.0, prev)
    ext_ref[:, 0:PREV_ROWS, :] = prev
    ext_ref[:, PREV_ROWS:, :] = u
    pos = pos0 + j * tt + lax.broadcasted_iota(jnp.int32, (1, tt, 1), 1)
    for g, w in enumerate(POOL_WINDOWS):
        lo = g * gw
        tot = None
        for k in range(w):
            part = ext_ref[:, PREV_ROWS - k:PREV_ROWS - k + tt, lo:lo + gw]
            tot = part if tot is None else tot + part
        cnt = jnp.minimum(pos + 1, w).astype(F32)
        pooled = (tot / cnt - u[:, :, lo:lo + gw]).astype(BF16).reshape(nbk * tt, gw)
        mixed = jnp.dot(pooled, w_ref[g], preferred_element_type=F32) * scale_ref[:, lo:lo + gw]
        o_ref[:, :, lo:lo + gw] = mixed.astype(BF16).reshape(nbk, tt, gw)


def _pool_mix(u_pool, prev, w_pool_bf, pool_scale, *, pos0, nbk, tt, name):
    nb, t, width = u_pool.shape
    tok = pl.BlockSpec((nbk, tt, width), lambda i, j: (i, j, 0))
    if prev is None:
        assert nbk == 1 and tt % PREV_ROWS == 0 and pos0 == 0
        step = tt // PREV_ROWS
        prev_arr = u_pool
        prev_spec = pl.BlockSpec((1, PREV_ROWS, width), lambda i, j: (i, jnp.maximum(j * step - 1, 0), 0))
    else:
        assert t == tt
        prev_arr = prev
        prev_spec = pl.BlockSpec((nbk, PREV_ROWS, width), lambda i, j: (i, 0, 0))
    return pl.pallas_call(
        functools.partial(_pool_body, pos0=pos0, zero_first=prev is None),
        grid=(nb // nbk, t // tt),
        in_specs=[tok, prev_spec, _const_spec(w_pool_bf.shape), _const_spec((1, width))],
        out_specs=tok,
        out_shape=jax.ShapeDtypeStruct((nb, t, width), BF16),
        scratch_shapes=[pltpu.VMEM((nbk, PREV_ROWS + tt, width), F32)],
        compiler_params=_cparams(2),
        name=name,
    )(u_pool, prev_arr, w_pool_bf, pool_scale.reshape(1, width))


ROW_TOKENS = STRIDE


def _compress_body(k_ref, wexp_ref, w2_ref, pos_ref, o_ref, sh_ref, *, transposed):
    kvw = k_ref.shape[2]
    nr = k_ref.shape[1] // ROW_TOKENS
    half = wexp_ref.shape[1] // 2
    p = None
    for i in range(ROW_TOKENS // 2):
        lhs = jnp.concatenate([k_ref[0, pl.ds(2 * i + e, nr, stride=ROW_TOKENS), :] for e in range(2)], axis=1)
        part = jnp.dot(lhs.astype(BF16), wexp_ref[2 * i * kvw:2 * (i + 1) * kvw, :], preferred_element_type=F32)
        p = part if p is None else p + part
    pp = jnp.dot(pos_ref[...], wexp_ref[...], preferred_element_type=F32)
    posb = pp[0:1, :half] + pp[1:2, half:]
    sh_ref[0:nr, :] = p[:, half:]
    sh_ref[nr:nr + 8, :] = jnp.zeros((8, half), F32)
    h = p[:, :half] + sh_ref[1:nr + 1, :] + posb
    c = jnp.dot(jax.nn.gelu(h).astype(BF16), w2_ref[...], preferred_element_type=F32)
    if transposed:
        ct = c.T
        ones_row = jnp.where(lax.broadcasted_iota(jnp.int32, (VT_ROWS - HEAD_DIM, nr), 0) == 0, 1.0, 0.0)
        c = jnp.concatenate([piece for g in range(N_KV)
                             for piece in (ct[g * HEAD_DIM:(g + 1) * HEAD_DIM], ones_row)], axis=0)
    o_ref[0] = c.astype(BF16)


def _compress(tokens, wexp, w2bd, posrows, *, transposed, name):
    nb, t, k = tokens.shape
    nr = t // ROW_TOKENS
    half = wexp.shape[1] // 2
    oshape = (nb, N_KV * VT_ROWS, nr) if transposed else (nb, nr, half)
    return pl.pallas_call(
        functools.partial(_compress_body, transposed=transposed),
        grid=(nb,),
        in_specs=[pl.BlockSpec((1, t, k), lambda b: (b, 0, 0)),
                  _const_spec(wexp.shape), _const_spec(w2bd.shape), _const_spec(posrows.shape)],
        out_specs=pl.BlockSpec((1,) + oshape[1:], lambda b: (b, 0, 0)),
        out_shape=jax.ShapeDtypeStruct(oshape, BF16),
        scratch_shapes=[pltpu.VMEM((nr + 8, half), F32)],
        compiler_params=_cparams(1),
        name=name,
    )(tokens, wexp, w2bd, posrows)


def _compress_weights(w1, w2, cmp_pos):
    eye = jnp.eye(N_KV, dtype=F32)
    w_blk = w1.reshape(L_CMP, HEAD_DIM, HEAD_DIM)
    halves = []
    for part in range(L_CMP // ROW_TOKENS):
        wpart = w_blk[part * ROW_TOKENS:(part + 1) * ROW_TOKENS]
        halves.append(jnp.einsum("gh,lio->lgiho", eye, wpart).reshape(ROW_TOKENS * N_KV * HEAD_DIM, N_KV * HEAD_DIM))
    wexp = jnp.concatenate(halves, axis=1).astype(BF16)
    w2bd = jnp.einsum("gh,io->giho", eye, w2).reshape(N_KV * HEAD_DIM, N_KV * HEAD_DIM).astype(BF16)
    pos = jnp.broadcast_to(cmp_pos.reshape(L_CMP // ROW_TOKENS, ROW_TOKENS, 1, HEAD_DIM),
                           (L_CMP // ROW_TOKENS, ROW_TOKENS, N_KV, HEAD_DIM)).reshape(L_CMP // ROW_TOKENS, -1)
    posrows = jnp.zeros((8, pos.shape[1]), F32).at[:pos.shape[0]].set(pos).astype(BF16)
    return wexp, w2bd, posrows


def _bucket_starts():
    d = np.arange(0, 4 * MAX_DIST)
    exact = N_BUCKETS // 2
    nf = np.maximum(d, 1).astype(np.float32)
    large = exact + (np.log(nf / np.float32(exact)) / np.float32(math.log(MAX_DIST / exact))
                     * np.float32(N_BUCKETS - exact)).astype(np.int32)
    bucket = np.where(d < exact, d, np.minimum(large, N_BUCKETS - 1))
    assert bucket.max() == N_BUCKETS - 1 and np.all(np.diff(bucket) >= 0)
    return [int(np.argmax(bucket >= b)) for b in range(1, N_BUCKETS)]


def _bias_minus_far(rel_bias, dist, head_axis):
    tab = (rel_bias - rel_bias[:, N_BUCKETS - 1:]) * math.log2(math.e)
    hshape = (1,) * head_axis + (-1,) + (1,) * (dist.ndim - head_axis)
    dist = jnp.expand_dims(dist, head_axis)
    out = jnp.broadcast_to(tab[:, 0].reshape(hshape),
                           dist.shape[:head_axis] + (tab.shape[0],) + dist.shape[head_axis + 1:])
    for b, start in enumerate(_bucket_starts(), start=1):
        out = jnp.where(dist >= start, tab[:, b].reshape(hshape), out)
    return out


def _prompt_tables(rel_bias, t):
    h = rel_bias.shape[0]
    qo = jnp.arange(TQ, dtype=jnp.int32)
    n_near = WINDOW // L_SEL + 2 * (TQ // L_SEL)
    dd = (WINDOW // L_SEL) - jnp.arange(n_near, dtype=jnp.int32)
    ko = jnp.arange(L_SEL, dtype=jnp.int32)
    d = (L_SEL * dd[:, None, None] + qo[None, None, :] - ko[None, :, None]).reshape(n_near * L_SEL, TQ)
    ok = (d >= 0) & (d < WINDOW)
    near = jnp.where(ok[:, None, :], _bias_minus_far(rel_bias, d, 1), NEG)
    near = near.reshape(n_near * L_SEL, h * TQ)
    n_tiles = t // TQ
    per_tile = TQ // STRIDE
    e0 = per_tile * (n_tiles - 1)
    rows = e0 + t // STRIDE
    e = e0 - jnp.arange(rows, dtype=jnp.int32)
    dc = STRIDE * e[:, None] - (L_CMP - 1) + qo[None, :]
    cmp = jnp.where((dc >= 0)[:, None, :], _bias_minus_far(rel_bias, dc, 1), NEG)
    cmp = cmp.reshape(rows, h * TQ)
    return near.astype(F32), cmp.astype(F32), e0


def _cover_t(n_rows, n_c, n_blk):
    i = np.arange(n_rows)[None, :]
    j = np.arange(n_blk)[:, None]
    start = i * STRIDE
    end = start + L_CMP - 1
    cov = (start <= (j + 1) * L_SEL - 1) & (end >= j * L_SEL) & (i < n_c)
    return jnp.asarray(cov.astype(np.float32), BF16)


def _descending_rank(v, n_valid, n_live=None):
    rows, cols = v.shape
    sub = lax.broadcasted_iota(jnp.int32, (8, cols), 0)

    def count(rank, lo):
        mid = v[lo:lo + 8]
        for jp in range(lo, min(lo + 8, n_valid)):
            other = v[jp:jp + 1, :]
            parts = [jnp.where(other > mid, 1.0, jnp.where((other == mid) & (sub > jp - lo), 1.0, 0.0))]
            if lo > 0:
                parts.insert(0, jnp.where(other > v[:lo], 1.0, 0.0))
            if lo + 8 < rows:
                parts.append(jnp.where(other >= v[lo + 8:], 1.0, 0.0))
            rank = rank + jnp.concatenate(parts, axis=0)
        return rank

    rank = jnp.zeros((rows, cols), F32)
    for lo in range(0, n_valid, 8):
        if n_live is None:
            rank = count(rank, lo)
        else:
            rank = lax.cond(lo < n_live, functools.partial(count, lo=lo), lambda r: r, rank)
    return rank


def _nsa_prompt_body(q_ref, gt_ref, kc_ref, vct_ref, ks_ref, vst_ref, kw_ref, vwt_ref, tcmp_ref, near_ref, cov_ref,
                     o_ref, selb_ref, qaug_ref, qaug1_ref, s0_ref, s1_ref, sw_ref, sd_ref, *, e0, top_n):
    it = pl.program_id(1)
    hd = HEAD_DIM
    rep = q_ref.shape[2] // (N_KV * hd)
    gc = rep * TQ
    cols = N_KV * gc
    n_blk = cov_ref.shape[0]
    per = TQ // L_SEL

    def per_group(x):
        return jnp.concatenate([x[:, g * TQ:(g + 1) * TQ] for g in range(N_KV) for _ in range(rep)], axis=1)

    def pv(vt_ref, keys, p, rows=VT_ROWS):
        return jnp.concatenate(
            [jnp.dot(vt_ref[0, g * rows:(g + 1) * rows, keys], p[:, g * gc:(g + 1) * gc],
                     preferred_element_type=F32) for g in range(N_KV)], axis=1)

    qt = q_ref[0].astype(F32).T
    zero = jnp.zeros((hd, TQ), F32)
    tiles = []
    for g in range(N_KV):
        for r in range(rep):
            blk = qt[(g * rep + r) * hd:(g * rep + r + 1) * hd, :]
            tiles.append(jnp.concatenate([blk if gg == g else zero for gg in range(N_KV)], axis=0))
    qbd = jnp.concatenate(tiles, axis=1).astype(BF16)

    gate = [jnp.concatenate([gt_ref[0, g * GATE_ROWS + br * rep + r:g * GATE_ROWS + br * rep + r + 1, :]
                             for g in range(N_KV) for r in range(rep)], axis=1) for br in range(3)]

    n_ck = kc_ref.shape[1]
    sc = jnp.dot(kc_ref[0], qbd, preferred_element_type=F32)
    t0 = pl.multiple_of(e0 - (TQ // STRIDE) * it, 8)
    sc = sc + tcmp_ref[pl.ds(t0, n_ck), :]

    n_wc = WINDOW // TQ + 1
    offs = []
    for k in range(n_wc):
        kidx = it - (n_wc - 1) + k
        offs.append(pl.multiple_of(jnp.maximum(kidx, 0) * TQ, TQ))
        tab_off = pl.multiple_of(jnp.where(kidx >= 0, k * TQ, near_ref.shape[0] - TQ), TQ)
        sw_ref[k * TQ:(k + 1) * TQ, :] = (jnp.dot(kw_ref[0, pl.ds(offs[k], TQ), :], qbd, preferred_element_type=F32)
                                          + near_ref[pl.ds(tab_off, TQ), :])

    mc = jnp.max(sc, axis=0, keepdims=True)
    pc = jnp.exp2(sc - mc).astype(BF16)
    oc = pv(vct_ref, slice(None), pc)
    inv_lc = jnp.where(mc > 0.1 * NEG, 1.0 / oc[hd:hd + 1], 0.0)
    oc = oc[:hd] * inv_lc
    imp = jnp.dot(cov_ref[...], pc, preferred_element_type=F32) * inv_lc

    sw = sw_ref[...]
    pw = jnp.exp2(sw - jnp.max(sw, axis=0, keepdims=True)).astype(BF16)
    ow = None
    for k in range(n_wc):
        part = pv(vwt_ref, pl.ds(offs[k], TQ), pw[k * TQ:(k + 1) * TQ, :])
        ow = part if ow is None else ow + part

    diag0 = jnp.maximum(it - 1, 0)
    dk = 2 * TQ
    kvw = N_KV * hd
    offd = pl.multiple_of(diag0 * TQ, TQ)
    near_off = pl.multiple_of((WINDOW // L_SEL - per * (it - diag0)) * L_SEL, TQ)
    sd_ref[...] = (jnp.dot(ks_ref[0, pl.ds(offd, dk), 0:kvw], qbd, preferred_element_type=F32)
                   + near_ref[pl.ds(near_off, dk), :])

    v = []
    for g in range(N_KV):
        vg = imp[:, g * gc:g * gc + TQ]
        for r in range(1, rep):
            vg = vg + imp[:, g * gc + r * TQ:g * gc + (r + 1) * TQ]
        v.append(vg)
    v = jnp.concatenate(v, axis=1)
    j = lax.broadcasted_iota(jnp.int32, v.shape, 0)
    lane = lax.broadcasted_iota(jnp.int32, v.shape, 1)
    cur = per * it + (lane % TQ) // L_SEL
    forced = (j == 0) | (j == cur) | (j == cur - 1)
    v = jnp.where(forced, 1e9, jnp.where(j <= cur, v, -1e9))
    rank = _descending_rank(v, n_blk, n_live=per * (it + 1))
    selb = jnp.where((rank < top_n) & (j <= cur), 0.0, NEG)

    selb_ref[...] = per_group(jnp.where(j < per * diag0, selb, NEG))
    gk = FAR_CHUNKS * TQ
    for ref in (qaug_ref, qaug1_ref):
        ref[0:kvw, :] = qbd
        ref[kvw + 16:, :] = jnp.zeros((ref.shape[0] - kvw - 16, cols), BF16)

    def online_step(carry, s, keys):
        m, acc = carry
        m_new = jnp.maximum(m, jnp.max(s, axis=0, keepdims=True))
        p = jnp.exp2(s - m_new).astype(BF16)
        return m_new, jnp.exp2(m - m_new) * acc + pv(vst_ref, keys, p)

    def far_keys(cb):
        return pl.ds(pl.multiple_of(cb * gk, gk), gk)

    def far_scores(cb, q_ref2, s_out):
        sel = selb_ref[pl.ds(pl.multiple_of(cb * FAR_BLOCKS, FAR_BLOCKS), FAR_BLOCKS), :]
        q_ref2[kvw:kvw + 16, :] = jnp.concatenate([sel, jnp.zeros((16 - FAR_BLOCKS, cols), F32)], axis=0).astype(BF16)
        s_out[...] = jnp.dot(ks_ref[0, far_keys(cb), :], q_ref2[...], preferred_element_type=F32)

    def far_pair(jb, carry):
        far_scores(2 * jb, qaug_ref, s0_ref)
        far_scores(2 * jb + 1, qaug1_ref, s1_ref)
        carry = online_step(carry, s0_ref[...], far_keys(2 * jb))
        return online_step(carry, s1_ref[...], far_keys(2 * jb + 1))

    def far_single(cb, carry):
        far_scores(cb, qaug_ref, s0_ref)
        return online_step(carry, s0_ref[...], far_keys(cb))

    carry = (jnp.full((1, cols), NEG, F32), jnp.zeros((VT_ROWS, cols), F32))
    n_groups = (diag0 + FAR_CHUNKS - 1) // FAR_CHUNKS
    carry = lax.fori_loop(0, n_groups // 2, far_pair, carry)
    carry = lax.fori_loop(2 * (n_groups // 2), n_groups, far_single, carry)

    sd = sd_ref[...]
    rows = []
    for k in range(dk // L_SEL):
        rk = jnp.max(jnp.where(j == per * diag0 + k, selb, NEG), axis=0, keepdims=True)
        rows.append(jnp.broadcast_to(per_group(rk), (L_SEL, cols)))
    sd = sd + jnp.concatenate(rows, axis=0)
    _, acc_s = online_step(carry, sd, pl.ds(offd, dk))

    out_t = (gate[0] * oc + (gate[1] / acc_s[hd:hd + 1]) * acc_s[:hd]
             + (gate[2] / ow[hd:hd + 1]) * ow[:hd])
    pieces = []
    for c in range(0, N_KV * rep, 2):
        pair = jnp.concatenate([out_t[:, c * TQ:(c + 1) * TQ], out_t[:, (c + 1) * TQ:(c + 2) * TQ]], axis=0)
        pieces.append(pair.T)
    o_ref[0] = jnp.concatenate(pieces, axis=1).astype(BF16)


def _nsa_prompt(q, gt, kc, vct, ks, vst, kw, vwt, tcmp, near, cov_t, *, e0, top_n):
    nb, t, nsa_w = q.shape
    assert t % (2 * FAR_CHUNKS * TQ) == 0
    cols = (nsa_w // HEAD_DIM) * TQ
    per_b = lambda shape: pl.BlockSpec((1,) + shape, lambda b, i: (b, 0, 0))
    return pl.pallas_call(
        functools.partial(_nsa_prompt_body, e0=e0, top_n=top_n),
        grid=(nb, t // TQ),
        in_specs=[pl.BlockSpec((1, TQ, nsa_w), lambda b, i: (b, i, 0)),
                  pl.BlockSpec((1, N_KV * GATE_ROWS, TQ), lambda b, i: (b, 0, i)),
                  per_b(kc.shape[1:]), per_b(vct.shape[1:]),
                  per_b(ks.shape[1:]), per_b(vst.shape[1:]), per_b(kw.shape[1:]), per_b(vwt.shape[1:]),
                  _const_spec(tcmp.shape), _const_spec(near.shape), _const_spec(cov_t.shape)],
        out_specs=pl.BlockSpec((1, TQ, nsa_w), lambda b, i: (b, i, 0)),
        out_shape=jax.ShapeDtypeStruct((nb, t, nsa_w), BF16),
        scratch_shapes=[pltpu.VMEM((cov_t.shape[0], cols), F32),
                        pltpu.VMEM((ks.shape[2], cols), BF16), pltpu.VMEM((ks.shape[2], cols), BF16),
                        pltpu.VMEM((FAR_CHUNKS * TQ, cols), F32), pltpu.VMEM((FAR_CHUNKS * TQ, cols), F32),
                        pltpu.VMEM((WINDOW + TQ, cols), F32), pltpu.VMEM((2 * TQ, cols), F32)],
        compiler_params=_cparams(2),
        name="nsa_prompt",
    )(q, gt, kc, vct, ks, vst, kw, vwt, tcmp, near, cov_t)


def _dot_nt(a, b):
    return lax.dot_general(a, b, (((1,), (1,)), ((), ())), preferred_element_type=F32)


def _page_copies(pt_ref, pools, bufs, sems, batch, slot, dst):
    n_pages = pt_ref.shape[1]
    out = []
    for i, (pool, buf) in enumerate(zip(pools, bufs)):
        for p in range(n_pages):
            page = 0 if batch is None else pt_ref[batch, p]
            out.append(pltpu.make_async_copy(pool.at[page], dst(buf, slot, p), sems.at[i, slot]))
    return out


def _gather_step(pt_ref, pools, bufs, sems, dst):
    b = pl.program_id(0)
    slot = b % 2

    @pl.when(b == 0)
    def _():
        for cp in _page_copies(pt_ref, pools, bufs, sems, 0, 0, dst):
            cp.start()

    for cp in _page_copies(pt_ref, pools, bufs, sems, None, slot, dst):
        cp.wait()
    nxt = jnp.minimum(b + 1, pl.num_programs(0) - 1)
    for cp in _page_copies(pt_ref, pools, bufs, sems, nxt, 1 - slot, dst):
        cp.start()
    return slot


def _gather_drain(pt_ref, pools, bufs, sems, dst, slot):
    @pl.when(pl.program_id(0) == pl.num_programs(0) - 1)
    def _():
        for cp in _page_copies(pt_ref, pools, bufs, sems, None, 1 - slot, dst):
            cp.wait()


def _position_bias(pos_ref, wexp_ref):
    half = wexp_ref.shape[1] // 2
    pp = jnp.dot(pos_ref[...], wexp_ref[...], preferred_element_type=F32)
    return pp[0:1, :half] + pp[1:2, half:]


def _regroup_pages(buf, slot, perm_ref, x_ref):
    n_pages = buf.shape[1]
    kvw = buf.shape[2]
    rpp = PAGE_SIZE // ROW_TOKENS
    nr = n_pages * rpp
    for i in range(n_pages // 2):
        pair = buf[slot, 2 * i:2 * i + 2].reshape(2 * kvw, PAGE_SIZE).astype(BF16)
        xt = _dot_nt(perm_ref[...], pair)
        for pp in range(2):
            for l in range(ROW_TOKENS):
                row0 = (l // 2) * nr + (2 * i + pp) * rpp
                x_ref[row0:row0 + rpp, (l % 2) * kvw:(l % 2 + 1) * kvw] = (
                    xt[l * rpp:(l + 1) * rpp, pp * kvw:(pp + 1) * kvw])


def _compress_contract(x_ref, wexp_ref):
    nr = x_ref.shape[0] // (ROW_TOKENS // 2)
    kvw = x_ref.shape[1] // 2
    p = None
    for i in range(ROW_TOKENS // 2):
        part = jnp.dot(x_ref[i * nr:(i + 1) * nr, :].astype(BF16), wexp_ref[2 * i * kvw:2 * (i + 1) * kvw, :],
                       preferred_element_type=F32)
        p = part if p is None else p + part
    return p


def _compress_finish(p, w2_ref, posb, sh_ref):
    nr = p.shape[0]
    half = p.shape[1] // 2
    sh_ref[0:nr, :] = p[:, half:]
    sh_ref[nr:nr + 8, :] = jnp.zeros((8, half), F32)
    h = p[:, :half] + sh_ref[1:nr + 1, :] + posb
    return jnp.dot(jax.nn.gelu(h).astype(BF16), w2_ref[...], preferred_element_type=F32).astype(BF16)


def _sample_cmp_body(pt_ref, kpool_ref, vpool_ref, qbd_ref, tab_ref, perm_ref, wk_ref, w2k_ref, pk_ref, wv_ref, w2v_ref,
                     pv_ref, cov_ref, oc_ref, imp_ref, kbuf, vbuf, sems, xk_ref, xv_ref, shk_ref, shv_ref, posb_ref,
                     *, rep, t_new):
    page_dst = lambda buf, sl, p: buf.at[sl, p]
    slot = _gather_step(pt_ref, (kpool_ref, vpool_ref), (kbuf, vbuf), sems, page_dst)

    @pl.when(pl.program_id(0) == 0)
    def _():
        posb_ref[0:1, :] = _position_bias(pk_ref, wk_ref)
        posb_ref[1:2, :] = _position_bias(pv_ref, wv_ref)

    _regroup_pages(kbuf, slot, perm_ref, xk_ref)
    _regroup_pages(vbuf, slot, perm_ref, xv_ref)
    pk = _compress_contract(xk_ref, wk_ref)
    pv = _compress_contract(xv_ref, wv_ref)
    kc = _compress_finish(pk, w2k_ref, posb_ref[0:1, :], shk_ref)
    vc = _compress_finish(pv, w2v_ref, posb_ref[1:2, :], shv_ref)
    s = _dot_nt(qbd_ref[0], kc) + tab_ref[...]
    m = jnp.max(s, axis=1, keepdims=True)
    p = jnp.where(s > 0.1 * NEG, jnp.exp2(s - m), 0.0)
    l = jnp.sum(p, axis=1, keepdims=True)
    pn = (p * jnp.where(l > 0.0, 1.0 / l, 0.0)).astype(BF16)
    oc_ref[0] = jnp.dot(pn, vc, preferred_element_type=F32)
    imp = jnp.dot(pn, cov_ref[...], preferred_element_type=F32)
    per_g = rep * t_new
    parts = []
    for g in range(N_KV):
        acc = imp[g * per_g:g * per_g + t_new]
        for r in range(1, rep):
            acc = acc + imp[g * per_g + r * t_new:g * per_g + (r + 1) * t_new]
        parts.append(acc)
    imp_ref[0] = jnp.concatenate(parts, axis=0)
    _gather_drain(pt_ref, (kpool_ref, vpool_ref), (kbuf, vbuf), sems, page_dst, slot)


def _sample_cmp(page_table, kpool_t, vpool_t, qbd, tab, cmp_k, cmp_v, cov, *, rep, t_new):
    nb, n_pages = page_table.shape
    rpp = PAGE_SIZE // ROW_TOKENS
    n_rows = n_pages * rpp
    n_q = qbd.shape[1]
    kvw = N_KV * HEAD_DIM
    tok = np.arange(PAGE_SIZE)
    perm = jnp.asarray((tok[None, :] == (tok[:, None] % rpp) * ROW_TOKENS + tok[:, None] // rpp).astype(np.float32), BF16)
    const = lambda a: pl.BlockSpec(a.shape, lambda b, pt: (0,) * a.ndim, pipeline_mode=pl.Buffered(1))
    grid_spec = pltpu.PrefetchScalarGridSpec(
        num_scalar_prefetch=1,
        grid=(nb,),
        in_specs=[pl.BlockSpec(memory_space=pl.ANY), pl.BlockSpec(memory_space=pl.ANY),
                  pl.BlockSpec((1, n_q, kvw), lambda b, pt: (b, 0, 0)), const(tab), const(perm),
                  const(cmp_k[0]), const(cmp_k[1]), const(cmp_k[2]), const(cmp_v[0]), const(cmp_v[1]), const(cmp_v[2]),
                  const(cov)],
        out_specs=[pl.BlockSpec((1, n_q, kvw), lambda b, pt: (b, 0, 0)),
                   pl.BlockSpec((1, N_KV * t_new, cov.shape[1]), lambda b, pt: (b, 0, 0))],
        scratch_shapes=[pltpu.VMEM((2, n_pages, kvw, PAGE_SIZE), F32), pltpu.VMEM((2, n_pages, kvw, PAGE_SIZE), F32),
                        pltpu.SemaphoreType.DMA((2, 2)),
                        pltpu.VMEM((n_pages * PAGE_SIZE // 2, 2 * kvw), F32),
                        pltpu.VMEM((n_pages * PAGE_SIZE // 2, 2 * kvw), F32),
                        pltpu.VMEM((n_rows + 8, kvw), F32), pltpu.VMEM((n_rows + 8, kvw), F32),
                        pltpu.VMEM((8, kvw), F32)],
    )
    return pl.pallas_call(
        functools.partial(_sample_cmp_body, rep=rep, t_new=t_new),
        grid_spec=grid_spec,
        out_shape=[jax.ShapeDtypeStruct((nb, n_q, kvw), F32),
                   jax.ShapeDtypeStruct((nb, N_KV * t_new, cov.shape[1]), F32)],
        compiler_params=_cparams(1),
        name="sample_cmp",
    )(page_table, kpool_t, vpool_t, qbd, tab, perm, *cmp_k, *cmp_v, cov)


def _select_body(imp_ref, o_ref, *, top_n, n_blk, pos0, t_new):
    rows, cols = imp_ref.shape
    j = lax.broadcasted_iota(jnp.int32, (rows, cols), 0)
    col = pl.program_id(0) * cols + lax.broadcasted_iota(jnp.int32, (rows, cols), 1)
    cur = (pos0 + col % t_new) // L_SEL
    forced = (j == 0) | (j == cur) | (j == cur - 1)
    v = jnp.where(forced, 1e9, jnp.where(j <= cur, imp_ref[...], -1e9))
    v = jnp.where(j < n_blk, v, -3e38)
    rank = _descending_rank(v, n_blk)
    o_ref[...] = jnp.where((rank < top_n) & (j <= cur) & (j < n_blk), 0.0, NEG)


def _select_blocks(imp_t, *, top_n, n_blk, pos0, t_new):
    rows, cols = imp_t.shape
    tile = 4 * LANE
    return pl.pallas_call(
        functools.partial(_select_body, top_n=top_n, n_blk=n_blk, pos0=pos0, t_new=t_new),
        grid=(cols // tile,),
        in_specs=[pl.BlockSpec((rows, tile), lambda i: (0, i))],
        out_specs=pl.BlockSpec((rows, tile), lambda i: (0, i)),
        out_shape=jax.ShapeDtypeStruct((rows, cols), F32),
        compiler_params=_cparams(1),
        name="sample_select",
    )(imp_t)


def _softmax_pv(parts):
    m = None
    for s, _, _ in parts:
        mk = jnp.max(s, axis=1, keepdims=True)
        m = mk if m is None else jnp.maximum(m, mk)
    l = None
    o = None
    for s, v, channel_major in parts:
        p = jnp.exp2(s - m)
        lk = jnp.sum(p, axis=1, keepdims=True)
        pb = p.astype(BF16)
        ok = _dot_nt(pb, v) if channel_major else jnp.dot(pb, v, preferred_element_type=F32)
        l = lk if l is None else l + lk
        o = ok if o is None else o + ok
    return o / l


def _sample_slc_body(pt_ref, kpool_ref, vpool_ref, qbd_ref, selb_ref, gate_ref, oc_ref, ksn_ref, vsn_ref, kwn_ref,
                     vwn_ref, kwin_ref, vwin_ref, near_ref, newtab_ref, wtab_ref, expand_ref,
                     o_ref, kbuf, vbuf, sems, pad_ref, *, rep, t_new):
    b = pl.program_id(0)
    page_dst = lambda buf, sl, p: buf.at[sl, :, pl.ds(p * PAGE_SIZE, PAGE_SIZE)]
    slot = _gather_step(pt_ref, (kpool_ref, vpool_ref), (kbuf, vbuf), sems, page_dst)
    hd = HEAD_DIM
    kvw = N_KV * hd

    @pl.when(b == 0)
    def _():
        pad_ref[...] = jnp.zeros(pad_ref.shape, F32)

    news = []
    for i, ref in enumerate((ksn_ref, vsn_ref, kwn_ref, vwn_ref)):
        pad_ref[i, 0:t_new, :] = ref[0]
        news.append(pad_ref[i].astype(BF16))
    ks_new, vs_new, kw_new, vw_new = news
    qbd = qbd_ref[0]

    per_g = rep * t_new
    sel = selb_ref[0]
    sel_rows = jnp.concatenate([sel[g * t_new:(g + 1) * t_new] for g in range(N_KV) for _ in range(rep)], axis=0)
    n_past_blk = expand_ref.shape[0]
    s_past = jnp.dot(jnp.concatenate([qbd, sel_rows[:, :n_past_blk].astype(BF16)], axis=1),
                     jnp.concatenate([kbuf[slot].astype(BF16), expand_ref[...]], axis=0),
                     preferred_element_type=F32)
    n_past = s_past.shape[1]
    n_near = near_ref.shape[1]
    s_past = jnp.concatenate([s_past[:, :n_past - n_near], s_past[:, n_past - n_near:] + near_ref[...]], axis=1)
    s_new = _dot_nt(qbd, ks_new) + newtab_ref[0]
    s_win = jnp.dot(qbd, kwin_ref[0].astype(BF16), preferred_element_type=F32) + wtab_ref[...]
    s_wnew = _dot_nt(qbd, kw_new) + newtab_ref[1]
    o_s = _softmax_pv([(s_past, vbuf[slot].astype(BF16), True), (s_new, vs_new, False)])
    o_w = _softmax_pv([(s_win, vwin_ref[0].astype(BF16), True), (s_wnew, vw_new, False)])

    gate = gate_ref[0]
    comb = gate[:, 0:1] * oc_ref[0] + gate[:, 1:2] * o_s + gate[:, 2:3] * o_w
    row = lax.broadcasted_iota(jnp.int32, comb.shape, 0)
    o_ref[0] = jnp.where(row < per_g, comb, pltpu.roll(comb, hd, axis=1))
    _gather_drain(pt_ref, (kpool_ref, vpool_ref), (kbuf, vbuf), sems, page_dst, slot)


def _sample_slc(page_table, kpool, vpool, qbd, selb, gate_rows, oc, ks_new, vs_new, kw_new, vw_new, kwin, vwin,
                near, newtab, wtab, expand, *, rep, t_new):
    nb, n_pages = page_table.shape
    n_keys = n_pages * PAGE_SIZE
    n_q = qbd.shape[1]
    kvw = N_KV * HEAD_DIM
    const = lambda a: pl.BlockSpec(a.shape, lambda b, pt: (0,) * a.ndim, pipeline_mode=pl.Buffered(1))
    per_b = lambda a: pl.BlockSpec((1,) + a.shape[1:], lambda b, pt: (b,) + (0,) * (a.ndim - 1))
    grid_spec = pltpu.PrefetchScalarGridSpec(
        num_scalar_prefetch=1,
        grid=(nb,),
        in_specs=[pl.BlockSpec(memory_space=pl.ANY), pl.BlockSpec(memory_space=pl.ANY),
                  per_b(qbd), per_b(selb), per_b(gate_rows), per_b(oc),
                  per_b(ks_new), per_b(vs_new), per_b(kw_new), per_b(vw_new), per_b(kwin), per_b(vwin),
                  const(near), const(newtab), const(wtab), const(expand)],
        out_specs=pl.BlockSpec((1, n_q, kvw), lambda b, pt: (b, 0, 0)),
        scratch_shapes=[pltpu.VMEM((2, kvw, n_keys), F32), pltpu.VMEM((2, kvw, n_keys), F32),
                        pltpu.SemaphoreType.DMA((2, 2)), pltpu.VMEM((4, LANE, kvw), F32)],
    )
    return pl.pallas_call(
        functools.partial(_sample_slc_body, rep=rep, t_new=t_new),
        grid_spec=grid_spec,
        out_shape=jax.ShapeDtypeStruct((nb, n_q, kvw), F32),
        compiler_params=_cparams(1),
        name="sample_slc_win",
    )(page_table, kpool, vpool, qbd, selb, gate_rows, oc, ks_new, vs_new, kw_new, vw_new, kwin, vwin,
      near, newtab, wtab, expand)


def _sample_tables(rel_bias, pos0, t_new, n_rows, n_c, wbuf):
    h = rel_bias.shape[0]
    qpos = pos0 + jnp.arange(t_new, dtype=jnp.int32)

    def table(kpos, ok_extra=None, upper=None):
        d = qpos[:, None] - kpos[None, :]
        ok = d >= 0
        if upper is not None:
            ok = ok & (d < upper)
        if ok_extra is not None:
            ok = ok & ok_extra[None, :]
        tab = jnp.where(ok[None], _bias_minus_far(rel_bias, d, 0), NEG)
        return tab.reshape(h * t_new, kpos.shape[0]).astype(F32)

    c = jnp.arange(n_rows, dtype=jnp.int32)
    cmp = table(c * STRIDE + (L_CMP - 1), ok_extra=c < n_c)
    near = table(pos0 - PAGE_SIZE + jnp.arange(PAGE_SIZE, dtype=jnp.int32))
    lane = jnp.arange(LANE, dtype=jnp.int32)
    new_slc = table(pos0 + lane, ok_extra=lane < t_new)
    new_win = table(pos0 + lane, ok_extra=lane < t_new, upper=WINDOW)
    kpos_w = pos0 - wbuf + jnp.arange(wbuf, dtype=jnp.int32)
    win = table(kpos_w, ok_extra=kpos_w >= 0, upper=WINDOW)
    return cmp, near, jnp.stack([new_slc, new_win]), win


def _layer_weights(p, n_heads):
    d = p["w_in"].shape[0]
    kv_end = d + 6 * N_KV * HEAD_DIM
    src = _gate_column_source(n_heads)
    gate_cols = jnp.where(jnp.asarray(src >= 0)[None, :], p["w_in"][:, kv_end + np.maximum(src, 0)], 0.0)
    ln = jnp.stack([p["ln_gain"][0], p["ln_bias"][0], p["ln_gain"][1], p["ln_bias"][1],
                    p["ln_gain"][2], p["ln_bias"][2]])
    return dict(
        wi1=p["w_ffn1_in"].astype(BF16), wo1=p["w_ffn1_out"].astype(BF16),
        wi2=p["w_ffn2_in"].astype(BF16), wo2=p["w_ffn2_out"].astype(BF16),
        wp=jnp.concatenate([p["w_in"][:, :kv_end], gate_cols], axis=1).astype(BF16),
        wout=p["w_out"].astype(BF16),
        wpool=p["w_pool"].astype(BF16), pool_scale=p["pool_scale"],
        ln1=ln[0:2], ln23=ln[2:6],
        cmp_k=_compress_weights(p["w_phi_k1"], p["w_phi_k2"], p["cmp_pos"]),
        cmp_v=_compress_weights(p["w_phi_v1"], p["w_phi_v2"], p["cmp_pos"]),
    )


PROMPT_TOKENS = 512


def _prompt_layer(x, mod, w, rel_bias, alpha):
    nb, t, d = x.shape
    assert t % TQ == 0 and t % PROMPT_TOKENS == 0
    (x1, u_pool, q, k_c, v_c, k_s, v_s, k_w, v_w, ks_bf, vs_t, kw_bf, vw_t, gates_t) = _ffn_in(
        x, mod, w["wi1"], w["wo1"], w["wp"], w["ln1"], alpha=alpha, nbk=1, tt=PROMPT_TOKENS, for_prompt=True)
    pool_out = _pool_mix(u_pool, None, w["wpool"], w["pool_scale"], pos0=0, nbk=1, tt=PROMPT_TOKENS,
                         name="pool_prompt")
    kc_cmp = _compress(k_c, *w["cmp_k"], transposed=False, name="compress_k_prompt")
    vc_cmp_t = _compress(v_c, *w["cmp_v"], transposed=True, name="compress_v_prompt")
    n_c = (t - L_CMP) // STRIDE + 1
    n_blk = t // L_SEL
    near, tcmp, e0 = _prompt_tables(rel_bias, t)
    nsa = _nsa_prompt(q, gates_t, kc_cmp, vc_cmp_t, ks_bf, vs_t, kw_bf, vw_t, tcmp, near,
                      _cover_t(t // ROW_TOKENS, n_c, n_blk), e0=e0, top_n=min(TOP_N, n_blk))
    y = _ffn_out(pool_out, nsa, x1, mod, w["wout"], w["wi2"], w["wo2"], w["ln23"], alpha=alpha,
                 nbk=1, tt=PROMPT_TOKENS, name="outproj_ffn2_prompt")
    keep = min(WINDOW, t)
    heads = lambda a: a.reshape(nb, -1, N_KV, HEAD_DIM)
    states = (heads(k_c), heads(v_c), heads(k_s), heads(v_s), heads(k_w[:, t - keep:]), heads(v_w[:, t - keep:]),
              u_pool[:, t - POOL_STATE:])
    return y, states


SAMPLE_BATCHES = 32


def _sample_layer(x, mod, w, rel_bias, alpha, page_table, pools, kwin, vwin, pool_state):
    nb, t, d = x.shape
    hd = HEAD_DIM
    n_heads = rel_bias.shape[0]
    rep = n_heads // N_KV
    kvw = N_KV * hd
    pos0 = page_table.shape[1] * PAGE_SIZE
    n_c = (pos0 + t - L_CMP) // STRIDE + 1
    assert (n_c - 1) * STRIDE + L_CMP <= pos0 and pos0 % PAGE_SIZE == 0 and nb % SAMPLE_BATCHES == 0
    n_rows = pos0 // ROW_TOKENS
    n_blk = -(-(pos0 + t) // L_SEL)
    assert n_blk - 1 == pos0 // L_SEL
    blk_pad = -(-n_blk // LANE) * LANE
    wbuf = kwin.shape[1]

    x1, u_pool, q, k_c, v_c, k_s, v_s, k_w, v_w, gates = _ffn_in(
        x, mod, w["wi1"], w["wo1"], w["wp"], w["ln1"], alpha=alpha, nbk=SAMPLE_BATCHES, tt=t, for_prompt=False)
    pw = u_pool.shape[-1]
    prev = jnp.concatenate([jnp.zeros((nb, PREV_ROWS - POOL_STATE, pw), F32), pool_state], axis=1)
    pool_out = _pool_mix(u_pool, prev, w["wpool"], w["pool_scale"], pos0=pos0, nbk=SAMPLE_BATCHES, tt=t,
                         name="pool_sample")

    qh = q.reshape(nb, t, N_KV, rep, hd).transpose(0, 2, 3, 1, 4)
    own = jnp.eye(N_KV, dtype=q.dtype)[None, :, None, None, :, None]
    qbd = (qh[:, :, :, :, None, :] * own).reshape(nb, N_KV * rep * t, kvw)
    gate_rows = gates[:, :, :N_KV * GATE_ROWS].reshape(nb, t, N_KV, GATE_ROWS)[..., :3 * rep]
    gate_rows = gate_rows.reshape(nb, t, N_KV, 3, rep).transpose(0, 2, 4, 1, 3).reshape(nb, N_KV * rep * t, 3)
    gate_rows = jnp.pad(gate_rows, ((0, 0), (0, 0), (0, LANE - 3)))

    tab_c, near, newtab, wtab = _sample_tables(rel_bias, pos0, t, n_rows, n_c, wbuf)
    cov = np.zeros((n_rows, blk_pad), np.float32)
    ci = np.arange(n_rows)[:, None]
    bj = np.arange(blk_pad)[None, :]
    cov[:] = ((ci * STRIDE <= (bj + 1) * L_SEL - 1) & (ci * STRIDE + L_CMP - 1 >= bj * L_SEL)
              & (ci < n_c) & (bj < n_blk))
    channel_major = lambda a: a.transpose(0, 2, 3, 1).reshape(a.shape[0], kvw, a.shape[1])
    oc, imp = _sample_cmp(page_table, channel_major(pools[0]), channel_major(pools[1]), qbd, tab_c, w["cmp_k"],
                          w["cmp_v"], jnp.asarray(cov, BF16), rep=rep, t_new=t)

    sel_rows = -(-n_blk // 8) * 8
    imp_t = imp.transpose(2, 0, 1).reshape(blk_pad, nb * N_KV * t)[:sel_rows]
    selb_t = _select_blocks(imp_t, top_n=min(TOP_N, n_blk), n_blk=n_blk, pos0=pos0, t_new=t)
    selb_t = jnp.pad(selb_t, ((0, blk_pad - sel_rows), (0, 0)), constant_values=NEG)
    selb = selb_t.reshape(blk_pad, nb, N_KV * t).transpose(1, 2, 0)

    key_blk = lax.broadcasted_iota(jnp.int32, (pos0 // L_SEL, pos0), 1) // L_SEL
    expand = (key_blk == lax.broadcasted_iota(jnp.int32, (pos0 // L_SEL, pos0), 0)).astype(BF16)
    kvf = lambda a: a.reshape(a.shape[0], a.shape[1], kvw)
    o = _sample_slc(page_table, channel_major(pools[2]), channel_major(pools[3]), qbd, selb, gate_rows, oc,
                    k_s, v_s, k_w, v_w, channel_major(kwin), channel_major(vwin), near, newtab, wtab, expand,
                    rep=rep, t_new=t)
    nsa = o[:, :, :hd].reshape(nb, N_KV, rep, t, hd).transpose(0, 3, 1, 2, 4).reshape(nb, t, n_heads * hd).astype(BF16)

    y = _ffn_out(pool_out, nsa, x1, mod, w["wout"], w["wi2"], w["wo2"], w["ln23"], alpha=alpha,
                 nbk=SAMPLE_BATCHES, tt=t, name="outproj_ffn2_sample")
    keep = min(WINDOW, pos0 + t)
    heads = lambda a: a.reshape(nb, -1, N_KV, HEAD_DIM)
    kw_ext = jnp.concatenate([kvf(kwin), k_w], axis=1)
    vw_ext = jnp.concatenate([kvf(vwin), v_w], axis=1)
    pool_ext = jnp.concatenate([pool_state, u_pool], axis=1)
    states = (heads(k_c), heads(v_c), heads(k_s), heads(v_s), heads(kw_ext[:, kw_ext.shape[1] - keep:]),
              heads(vw_ext[:, vw_ext.shape[1] - keep:]), pool_ext[:, pool_ext.shape[1] - POOL_STATE:])
    return y, states


def kernel(x_prompt, x_sample, c_prompt, c_sample, cache_k_cmp, cache_v_cmp, cache_k_slc, cache_v_slc, page_table,
           state_k_win, state_v_win, state_pool, w_ada, b_ada, ln_gain, ln_bias, w_ffn1_in, w_ffn1_out, w_ffn2_in,
           w_ffn2_out, w_in, w_phi_k1, w_phi_k2, w_phi_v1, w_phi_v2, cmp_pos, w_pool, pool_scale, w_out, rel_bias):
    depth = w_ada.shape[0]
    alpha = (2.0 * depth) ** 0.25
    n_heads = rel_bias.shape[0]
    nb_p, nb_s = x_prompt.shape[0], x_sample.shape[0]
    d = x_prompt.shape[-1]
    xp, xs = x_prompt, x_sample
    c_all = jnp.concatenate([c_prompt, c_sample], axis=0)
    p_states, s_states = [], []
    for l in range(depth):
        p = dict(w_ada=w_ada[l], b_ada=b_ada[l], ln_gain=ln_gain[l], ln_bias=ln_bias[l], w_ffn1_in=w_ffn1_in[l],
                 w_ffn1_out=w_ffn1_out[l], w_ffn2_in=w_ffn2_in[l], w_ffn2_out=w_ffn2_out[l], w_in=w_in[l],
                 w_phi_k1=w_phi_k1[l], w_phi_k2=w_phi_k2[l], w_phi_v1=w_phi_v1[l], w_phi_v2=w_phi_v2[l],
                 cmp_pos=cmp_pos[l], w_pool=w_pool[l], pool_scale=pool_scale[l], w_out=w_out[l])
        w = _layer_weights(p, n_heads)
        mod = _ada(c_all, p["w_ada"], p["b_ada"]).reshape(nb_p + nb_s, 9, d)
        xp, sp = _prompt_layer(xp, mod[:nb_p], w, rel_bias, alpha)
        xs, ss = _sample_layer(xs, mod[nb_p:], w, rel_bias, alpha, page_table,
                               (cache_k_cmp[l], cache_v_cmp[l], cache_k_slc[l], cache_v_slc[l]),
                               state_k_win[l], state_v_win[l], state_pool[l])
        p_states.append(sp)
        s_states.append(ss)
    stack = lambda states: tuple(jnp.stack(a) for a in zip(*states))
    return (xp, xs) + stack(p_states) + stack(s_states)
```

```python
import functools
import math

import numpy as np
import jax
import jax.numpy as jnp
from jax import lax
from jax.experimental import pallas as pl
from jax.experimental.pallas import tpu as pltpu

F32 = jnp.float32
BF16 = jnp.bfloat16

HEAD_DIM = 64
N_KV = 2
L_CMP = 32
STRIDE = 16
L_SEL = 64
TOP_N = 16
WINDOW = 512
N_BUCKETS = 32
MAX_DIST = 128
POOL_WINDOWS = (2, 4, 8, 16)
POOL_STATE = max(POOL_WINDOWS) - 1
LN_EPS = 1e-5
NEG = -1e30
PAGE_SIZE = 128

Q_SCALE = HEAD_DIM ** -0.5 * math.log2(math.e)
TQ = 128
FAR_CHUNKS = 4
FAR_BLOCKS = FAR_CHUNKS * TQ // L_SEL
VT_ROWS = HEAD_DIM + 16

GATHER_DEPTH = 3
LANE = 128
MXU_COLS = 256
FF_CHUNK = MXU_COLS
VMEM_LIMIT = 56 * 1024 * 1024


def _cparams(n_axes):
    return pltpu.CompilerParams(dimension_semantics=("arbitrary",) * n_axes,
                                vmem_limit_bytes=VMEM_LIMIT)


def _const_spec(shape):
    nd = len(shape)
    return pl.BlockSpec(shape, lambda *_: (0,) * nd, pipeline_mode=pl.Buffered(1))


def _layer_norm(y, gain, bias):
    mu = jnp.mean(y, axis=-1, keepdims=True)
    yc = y - mu
    var = jnp.mean(yc * yc, axis=-1, keepdims=True)
    return yc * lax.rsqrt(var + LN_EPS) * gain + bias


def _swiglu(u_bf, wi_ref, wo_ref):
    d_ff = wo_ref.shape[0]
    acc = None
    for c in range(d_ff // FF_CHUNK):
        lo = c * FF_CHUNK
        a = jnp.dot(u_bf, wi_ref[:, lo:lo + FF_CHUNK], preferred_element_type=F32)
        b = jnp.dot(u_bf, wi_ref[:, d_ff + lo:d_ff + lo + FF_CHUNK], preferred_element_type=F32)
        hid = (a * jax.nn.sigmoid(a) * b).astype(BF16)
        part = jnp.dot(hid, wo_ref[lo:lo + FF_CHUNK, :], preferred_element_type=F32)
        acc = part if acc is None else acc + part
    return acc


def _ada_body(c_ref, w_ref, b_ref, o_ref):
    c = c_ref[...]
    act = (c * jax.nn.sigmoid(c)).astype(BF16)
    o_ref[...] = jnp.dot(act, w_ref[...].astype(BF16), preferred_element_type=F32) + b_ref[...]


def _ada(c_all, w_ada, b_ada):
    nb, d = c_all.shape
    n_out = w_ada.shape[1]
    cols = 9 * LANE
    return pl.pallas_call(
        _ada_body,
        grid=(n_out // cols,),
        in_specs=[pl.BlockSpec((nb, d), lambda j: (0, 0)),
                  pl.BlockSpec((d, cols), lambda j: (0, j)),
                  pl.BlockSpec((1, cols), lambda j: (0, j))],
        out_specs=pl.BlockSpec((nb, cols), lambda j: (0, j)),
        out_shape=jax.ShapeDtypeStruct((nb, n_out), F32),
        compiler_params=_cparams(1),
        name="ada",
    )(c_all, w_ada, b_ada.reshape(1, n_out))


def _ffn_in_body(x_ref, mod_ref, wi_ref, wo_ref, wp_ref, ln_ref, *out_refs, alpha, pool_w, nsa_w, kv_w, for_prompt):
    nbk, tt, d = x_ref.shape
    n = nbk * tt
    x = x_ref[...]
    mod = mod_ref[...]
    u = (x * (1.0 + mod[:, 1:2, :]) + mod[:, 0:1, :]).reshape(n, d).astype(BF16)
    h = _swiglu(u, wi_ref, wo_ref).reshape(nbk, tt, d)
    x1 = _layer_norm(alpha * x + 0.5 * mod[:, 2:3, :] * h, ln_ref[0:1, :], ln_ref[1:2, :])
    u1 = (x1 * (1.0 + mod[:, 4:5, :]) + mod[:, 3:4, :]).reshape(n, d).astype(BF16)
    proj = jnp.dot(u1, wp_ref[...], preferred_element_type=F32)

    x1_ref, upool_ref, q_ref = out_refs[:3]
    kv_refs = out_refs[3:9]
    x1_ref[...] = x1
    upool_ref[...] = proj[:, :pool_w].reshape(nbk, tt, pool_w)
    q_ref[...] = (proj[:, pool_w:pool_w + nsa_w] * Q_SCALE).astype(BF16).reshape(nbk, tt, nsa_w)
    off = pool_w + nsa_w
    kv = []
    for i in range(6):
        blk = proj[:, off + i * kv_w:off + (i + 1) * kv_w]
        kv.append(blk)
        kv_refs[i][...] = blk.reshape(nbk, tt, kv_w)
    gates = jax.nn.sigmoid(proj[:, off + 6 * kv_w:off + 6 * kv_w + LANE])
    if for_prompt:
        ksb_ref, vst_ref, kwb_ref, vwt_ref, gt_ref = out_refs[9:]
        row = lax.broadcasted_iota(jnp.int32, (n, kv_w), 0)
        lane = lax.broadcasted_iota(jnp.int32, (n, kv_w), 1)
        onehot = jnp.where((row // L_SEL) % FAR_BLOCKS == lane, 1.0, 0.0)
        ones_row = jnp.where(lax.broadcasted_iota(jnp.int32, (VT_ROWS - HEAD_DIM, n), 0) == 0, 1.0, 0.0)

        def values_t(x):
            xt = x.T
            return jnp.concatenate([piece for g in range(N_KV)
                                    for piece in (xt[g * HEAD_DIM:(g + 1) * HEAD_DIM], ones_row)], axis=0).astype(BF16)

        ksb_ref[0] = jnp.concatenate([kv[2], onehot], axis=1).astype(BF16)
        vst_ref[0] = values_t(kv[3])
        kwb_ref[0] = kv[4].astype(BF16)
        vwt_ref[0] = values_t(kv[5])
        gt_ref[0] = gates.T[:gt_ref.shape[1], :]
    else:
        out_refs[9][...] = gates.reshape(nbk, tt, LANE)


def _ffn_in(x, mod, wi, wo, wp, ln, *, alpha, nbk, tt, for_prompt):
    nb, t, d = x.shape
    pool_w = d // 2
    nsa_w = d - pool_w
    kv_w = N_KV * HEAD_DIM
    grid = (nb // nbk, t // tt)
    tok = lambda w: pl.BlockSpec((nbk, tt, w), lambda i, j: (i, j, 0))
    out_specs = [tok(d), tok(pool_w), tok(nsa_w)] + [tok(kv_w)] * 6
    out_shape = ([jax.ShapeDtypeStruct((nb, t, d), F32), jax.ShapeDtypeStruct((nb, t, pool_w), F32),
                  jax.ShapeDtypeStruct((nb, t, nsa_w), BF16)]
                 + [jax.ShapeDtypeStruct((nb, t, kv_w), F32)] * 6)
    if for_prompt:
        assert nbk == 1 and tt % (FAR_BLOCKS * L_SEL) == 0
        tr = lambda rows: pl.BlockSpec((1, rows, tt), lambda i, j: (i, 0, j))
        vt_rows = N_KV * VT_ROWS
        out_specs += [tok(2 * kv_w), tr(vt_rows), tok(kv_w), tr(vt_rows), tr(N_KV * GATE_ROWS)]
        out_shape += [jax.ShapeDtypeStruct((nb, t, 2 * kv_w), BF16), jax.ShapeDtypeStruct((nb, vt_rows, t), BF16),
                      jax.ShapeDtypeStruct((nb, t, kv_w), BF16), jax.ShapeDtypeStruct((nb, vt_rows, t), BF16),
                      jax.ShapeDtypeStruct((nb, N_KV * GATE_ROWS, t), F32)]
    else:
        out_specs += [tok(LANE)]
        out_shape += [jax.ShapeDtypeStruct((nb, t, LANE), F32)]
    body = functools.partial(_ffn_in_body, alpha=alpha, pool_w=pool_w, nsa_w=nsa_w, kv_w=kv_w,
                             for_prompt=for_prompt)
    return pl.pallas_call(
        body,
        grid=grid,
        in_specs=[tok(d),
                  pl.BlockSpec((nbk, mod.shape[1], d), lambda i, j: (i, 0, 0)),
                  _const_spec(wi.shape), _const_spec(wo.shape), _const_spec(wp.shape), _const_spec(ln.shape)],
        out_specs=out_specs,
        out_shape=out_shape,
        compiler_params=_cparams(2),
        name="ffn1_inproj_prompt" if for_prompt else "ffn1_inproj_sample",
    )(x, mod, wi, wo, wp, ln)


GATE_ROWS = 16


def _gate_column_source(n_heads):
    rep = n_heads // N_KV
    src = np.full((LANE,), -1, np.int32)
    for g in range(N_KV):
        for br in range(3):
            for r in range(rep):
                src[g * GATE_ROWS + br * rep + r] = br * n_heads + g * rep + r
    return src


def _ffn_out_body(pool_ref, nsa_ref, x1_ref, mod_ref, wout_ref, wi_ref, wo_ref, ln_ref, o_ref, *, alpha):
    nbk, tt, d = x1_ref.shape
    n = nbk * tt
    pw = pool_ref.shape[-1]
    mod = mod_ref[...]
    x1 = x1_ref[...]
    mix = (jnp.dot(pool_ref[...].reshape(n, pw), wout_ref[:pw, :], preferred_element_type=F32)
           + jnp.dot(nsa_ref[...].reshape(n, d - pw), wout_ref[pw:, :], preferred_element_type=F32))
    x2 = _layer_norm(alpha * x1 + mod[:, 5:6, :] * mix.reshape(nbk, tt, d), ln_ref[0:1, :], ln_ref[1:2, :])
    u = (x2 * (1.0 + mod[:, 7:8, :]) + mod[:, 6:7, :]).reshape(n, d).astype(BF16)
    h = _swiglu(u, wi_ref, wo_ref).reshape(nbk, tt, d)
    o_ref[...] = _layer_norm(alpha * x2 + 0.5 * mod[:, 8:9, :] * h, ln_ref[2:3, :], ln_ref[3:4, :])


def _ffn_out(pool_out, nsa_out, x1, mod, wout, wi, wo, ln, *, alpha, nbk, tt, name):
    nb, t, d = x1.shape
    tok = lambda w: pl.BlockSpec((nbk, tt, w), lambda i, j: (i, j, 0))
    return pl.pallas_call(
        functools.partial(_ffn_out_body, alpha=alpha),
        grid=(nb // nbk, t // tt),
        in_specs=[tok(pool_out.shape[-1]), tok(nsa_out.shape[-1]), tok(d),
                  pl.BlockSpec((nbk, mod.shape[1], d), lambda i, j: (i, 0, 0)),
                  _const_spec(wout.shape), _const_spec(wi.shape), _const_spec(wo.shape), _const_spec(ln.shape)],
        out_specs=tok(d),
        out_shape=jax.ShapeDtypeStruct((nb, t, d), F32),
        compiler_params=_cparams(2),
        name=name,
    )(pool_out, nsa_out, x1, mod, wout, wi, wo, ln)


PREV_ROWS = 16


def _pool_body(u_ref, prev_ref, w_ref, scale_ref, o_ref, ext_ref, *, pos0, zero_first):
    nbk, tt, width = u_ref.shape
    gw = width // len(POOL_WINDOWS)
    j = pl.program_id(1)
    u = u_ref[...]
    prev = prev_ref[...]
    if zero_first:
        prev = jnp.where(j == 0, 0.0, prev)
    ext_ref[:, 0:PREV_ROWS, :] = prev
    ext_ref[:, PREV_ROWS:, :] = u
    pos = pos0 + j * tt + lax.broadcasted_iota(jnp.int32, (1, tt, 1), 1)
    for g, w in enumerate(POOL_WINDOWS):
        lo = g * gw
        tot = None
        for k in range(w):
            part = ext_ref[:, PREV_ROWS - k:PREV_ROWS - k + tt, lo:lo + gw]
            tot = part if tot is None else tot + part
        cnt = jnp.minimum(pos + 1, w).astype(F32)
        pooled = (tot / cnt - u[:, :, lo:lo + gw]).astype(BF16).reshape(nbk * tt, gw)
        mixed = jnp.dot(pooled, w_ref[g], preferred_element_type=F32) * scale_ref[:, lo:lo + gw]
        o_ref[:, :, lo:lo + gw] = mixed.astype(BF16).reshape(nbk, tt, gw)


def _pool_mix(u_pool, prev, w_pool_bf, pool_scale, *, pos0, nbk, tt, name):
    nb, t, width = u_pool.shape
    tok = pl.BlockSpec((nbk, tt, width), lambda i, j: (i, j, 0))
    if prev is None:
        assert nbk == 1 and tt % PREV_ROWS == 0 and pos0 == 0
        step = tt // PREV_ROWS
        prev_arr = u_pool
        prev_spec = pl.BlockSpec((1, PREV_ROWS, width), lambda i, j: (i, jnp.maximum(j * step - 1, 0), 0))
    else:
        assert t == tt
        prev_arr = prev
        prev_spec = pl.BlockSpec((nbk, PREV_ROWS, width), lambda i, j: (i, 0, 0))
    return pl.pallas_call(
        functools.partial(_pool_body, pos0=pos0, zero_first=prev is None),
        grid=(nb // nbk, t // tt),
        in_specs=[tok, prev_spec, _const_spec(w_pool_bf.shape), _const_spec((1, width))],
        out_specs=tok,
        out_shape=jax.ShapeDtypeStruct((nb, t, width), BF16),
        scratch_shapes=[pltpu.VMEM((nbk, PREV_ROWS + tt, width), F32)],
        compiler_params=_cparams(2),
        name=name,
    )(u_pool, prev_arr, w_pool_bf, pool_scale.reshape(1, width))


ROW_TOKENS = STRIDE


def _compress_body(k_ref, wexp_ref, w2_ref, pos_ref, o_ref, sh_ref, *, transposed):
    kvw = k_ref.shape[2]
    nr = k_ref.shape[1] // ROW_TOKENS
    half = wexp_ref.shape[1] // 2
    p = None
    for i in range(ROW_TOKENS // 2):
        lhs = jnp.concatenate([k_ref[0, pl.ds(2 * i + e, nr, stride=ROW_TOKENS), :] for e in range(2)], axis=1)
        part = jnp.dot(lhs.astype(BF16), wexp_ref[2 * i * kvw:2 * (i + 1) * kvw, :], preferred_element_type=F32)
        p = part if p is None else p + part
    pp = jnp.dot(pos_ref[...], wexp_ref[...], preferred_element_type=F32)
    posb = pp[0:1, :half] + pp[1:2, half:]
    sh_ref[0:nr, :] = p[:, half:]
    sh_ref[nr:nr + 8, :] = jnp.zeros((8, half), F32)
    h = p[:, :half] + sh_ref[1:nr + 1, :] + posb
    c = jnp.dot(jax.nn.gelu(h).astype(BF16), w2_ref[...], preferred_element_type=F32)
    if transposed:
        ct = c.T
        ones_row = jnp.where(lax.broadcasted_iota(jnp.int32, (VT_ROWS - HEAD_DIM, nr), 0) == 0, 1.0, 0.0)
        c = jnp.concatenate([piece for g in range(N_KV)
                             for piece in (ct[g * HEAD_DIM:(g + 1) * HEAD_DIM], ones_row)], axis=0)
    o_ref[0] = c.astype(BF16)


def _compress(tokens, wexp, w2bd, posrows, *, transposed, name):
    nb, t, k = tokens.shape
    nr = t // ROW_TOKENS
    half = wexp.shape[1] // 2
    oshape = (nb, N_KV * VT_ROWS, nr) if transposed else (nb, nr, half)
    return pl.pallas_call(
        functools.partial(_compress_body, transposed=transposed),
        grid=(nb,),
        in_specs=[pl.BlockSpec((1, t, k), lambda b: (b, 0, 0)),
                  _const_spec(wexp.shape), _const_spec(w2bd.shape), _const_spec(posrows.shape)],
        out_specs=pl.BlockSpec((1,) + oshape[1:], lambda b: (b, 0, 0)),
        out_shape=jax.ShapeDtypeStruct(oshape, BF16),
        scratch_shapes=[pltpu.VMEM((nr + 8, half), F32)],
        compiler_params=_cparams(1),
        name=name,
    )(tokens, wexp, w2bd, posrows)


def _compress_weights(w1, w2, cmp_pos):
    eye = jnp.eye(N_KV, dtype=F32)
    w_blk = w1.reshape(L_CMP, HEAD_DIM, HEAD_DIM)
    halves = []
    for part in range(L_CMP // ROW_TOKENS):
        wpart = w_blk[part * ROW_TOKENS:(part + 1) * ROW_TOKENS]
        halves.append(jnp.einsum("gh,lio->lgiho", eye, wpart).reshape(ROW_TOKENS * N_KV * HEAD_DIM, N_KV * HEAD_DIM))
    wexp = jnp.concatenate(halves, axis=1).astype(BF16)
    w2bd = jnp.einsum("gh,io->giho", eye, w2).reshape(N_KV * HEAD_DIM, N_KV * HEAD_DIM).astype(BF16)
    pos = jnp.broadcast_to(cmp_pos.reshape(L_CMP // ROW_TOKENS, ROW_TOKENS, 1, HEAD_DIM),
                           (L_CMP // ROW_TOKENS, ROW_TOKENS, N_KV, HEAD_DIM)).reshape(L_CMP // ROW_TOKENS, -1)
    posrows = jnp.zeros((8, pos.shape[1]), F32).at[:pos.shape[0]].set(pos).astype(BF16)
    return wexp, w2bd, posrows


def _bucket_starts():
    d = np.arange(0, 4 * MAX_DIST)
    exact = N_BUCKETS // 2
    nf = np.maximum(d, 1).astype(np.float32)
    large = exact + (np.log(nf / np.float32(exact)) / np.float32(math.log(MAX_DIST / exact))
                     * np.float32(N_BUCKETS - exact)).astype(np.int32)
    bucket = np.where(d < exact, d, np.minimum(large, N_BUCKETS - 1))
    assert bucket.max() == N_BUCKETS - 1 and np.all(np.diff(bucket) >= 0)
    return [int(np.argmax(bucket >= b)) for b in range(1, N_BUCKETS)]


def _bias_minus_far(rel_bias, dist, head_axis):
    tab = (rel_bias - rel_bias[:, N_BUCKETS - 1:]) * math.log2(math.e)
    hshape = (1,) * head_axis + (-1,) + (1,) * (dist.ndim - head_axis)
    dist = jnp.expand_dims(dist, head_axis)
    out = jnp.broadcast_to(tab[:, 0].reshape(hshape),
                           dist.shape[:head_axis] + (tab.shape[0],) + dist.shape[head_axis + 1:])
    for b, start in enumerate(_bucket_starts(), start=1):
        out = jnp.where(dist >= start, tab[:, b].reshape(hshape), out)
    return out


def _prompt_tables(rel_bias, t):
    h = rel_bias.shape[0]
    qo = jnp.arange(TQ, dtype=jnp.int32)
    n_near = WINDOW // L_SEL + 2 * (TQ // L_SEL)
    dd = (WINDOW // L_SEL) - jnp.arange(n_near, dtype=jnp.int32)
    ko = jnp.arange(L_SEL, dtype=jnp.int32)
    d = (L_SEL * dd[:, None, None] + qo[None, None, :] - ko[None, :, None]).reshape(n_near * L_SEL, TQ)
    ok = (d >= 0) & (d < WINDOW)
    near = jnp.where(ok[:, None, :], _bias_minus_far(rel_bias, d, 1), NEG)
    near = near.reshape(n_near * L_SEL, h * TQ)
    n_tiles = t // TQ
    per_tile = TQ // STRIDE
    e0 = per_tile * (n_tiles - 1)
    rows = e0 + t // STRIDE
    e = e0 - jnp.arange(rows, dtype=jnp.int32)
    dc = STRIDE * e[:, None] - (L_CMP - 1) + qo[None, :]
    cmp = jnp.where((dc >= 0)[:, None, :], _bias_minus_far(rel_bias, dc, 1), NEG)
    cmp = cmp.reshape(rows, h * TQ)
    return near.astype(F32), cmp.astype(F32), e0


def _cover_t(n_rows, n_c, n_blk):
    i = np.arange(n_rows)[None, :]
    j = np.arange(n_blk)[:, None]
    start = i * STRIDE
    end = start + L_CMP - 1
    cov = (start <= (j + 1) * L_SEL - 1) & (end >= j * L_SEL) & (i < n_c)
    return jnp.asarray(cov.astype(np.float32), BF16)


def _descending_rank(v, n_valid, n_live=None):
    rows, cols = v.shape
    sub = lax.broadcasted_iota(jnp.int32, (8, cols), 0)

    def count(rank, lo):
        mid = v[lo:lo + 8]
        for jp in range(lo, min(lo + 8, n_valid)):
            other = v[jp:jp + 1, :]
            parts = [jnp.where(other > mid, 1.0, jnp.where((other == mid) & (sub > jp - lo), 1.0, 0.0))]
            if lo > 0:
                parts.insert(0, jnp.where(other > v[:lo], 1.0, 0.0))
            if lo + 8 < rows:
                parts.append(jnp.where(other >= v[lo + 8:], 1.0, 0.0))
            rank = rank + jnp.concatenate(parts, axis=0)
        return rank

    rank = jnp.zeros((rows, cols), F32)
    for lo in range(0, n_valid, 8):
        if n_live is None:
            rank = count(rank, lo)
        else:
            rank = lax.cond(lo < n_live, functools.partial(count, lo=lo), lambda r: r, rank)
    return rank


def _nsa_prompt_body(q_ref, gt_ref, kc_ref, vct_ref, ks_ref, vst_ref, kw_ref, vwt_ref, tcmp_ref, near_ref, cov_ref,
                     o_ref, selb_ref, qaug_ref, qaug1_ref, s0_ref, s1_ref, sw_ref, sd_ref, *, e0, top_n):
    it = pl.program_id(1)
    hd = HEAD_DIM
    rep = q_ref.shape[2] // (N_KV * hd)
    gc = rep * TQ
    cols = N_KV * gc
    n_blk = cov_ref.shape[0]
    per = TQ // L_SEL

    def per_group(x):
        return jnp.concatenate([x[:, g * TQ:(g + 1) * TQ] for g in range(N_KV) for _ in range(rep)], axis=1)

    def pv(vt_ref, keys, p, rows=VT_ROWS):
        return jnp.concatenate(
            [jnp.dot(vt_ref[0, g * rows:(g + 1) * rows, keys], p[:, g * gc:(g + 1) * gc],
                     preferred_element_type=F32) for g in range(N_KV)], axis=1)

    qt = q_ref[0].astype(F32).T
    zero = jnp.zeros((hd, TQ), F32)
    tiles = []
    for g in range(N_KV):
        for r in range(rep):
            blk = qt[(g * rep + r) * hd:(g * rep + r + 1) * hd, :]
            tiles.append(jnp.concatenate([blk if gg == g else zero for gg in range(N_KV)], axis=0))
    qbd = jnp.concatenate(tiles, axis=1).astype(BF16)

    gate = [jnp.concatenate([gt_ref[0, g * GATE_ROWS + br * rep + r:g * GATE_ROWS + br * rep + r + 1, :]
                             for g in range(N_KV) for r in range(rep)], axis=1) for br in range(3)]

    n_ck = kc_ref.shape[1]
    sc = jnp.dot(kc_ref[0], qbd, preferred_element_type=F32)
    t0 = pl.multiple_of(e0 - (TQ // STRIDE) * it, 8)
    sc = sc + tcmp_ref[pl.ds(t0, n_ck), :]

    n_wc = WINDOW // TQ + 1
    offs = []
    for k in range(n_wc):
        kidx = it - (n_wc - 1) + k
        offs.append(pl.multiple_of(jnp.maximum(kidx, 0) * TQ, TQ))
        tab_off = pl.multiple_of(jnp.where(kidx >= 0, k * TQ, near_ref.shape[0] - TQ), TQ)
        sw_ref[k * TQ:(k + 1) * TQ, :] = (jnp.dot(kw_ref[0, pl.ds(offs[k], TQ), :], qbd, preferred_element_type=F32)
                                          + near_ref[pl.ds(tab_off, TQ), :])

    mc = jnp.max(sc, axis=0, keepdims=True)
    pc = jnp.exp2(sc - mc).astype(BF16)
    oc = pv(vct_ref, slice(None), pc)
    inv_lc = jnp.where(mc > 0.1 * NEG, 1.0 / oc[hd:hd + 1], 0.0)
    oc = oc[:hd] * inv_lc
    imp = jnp.dot(cov_ref[...], pc, preferred_element_type=F32) * inv_lc

    sw = sw_ref[...]
    pw = jnp.exp2(sw - jnp.max(sw, axis=0, keepdims=True)).astype(BF16)
    ow = None
    for k in range(n_wc):
        part = pv(vwt_ref, pl.ds(offs[k], TQ), pw[k * TQ:(k + 1) * TQ, :])
        ow = part if ow is None else ow + part

    diag0 = jnp.maximum(it - 1, 0)
    dk = 2 * TQ
    kvw = N_KV * hd
    offd = pl.multiple_of(diag0 * TQ, TQ)
    near_off = pl.multiple_of((WINDOW // L_SEL - per * (it - diag0)) * L_SEL, TQ)
    sd_ref[...] = (jnp.dot(ks_ref[0, pl.ds(offd, dk), 0:kvw], qbd, preferred_element_type=F32)
                   + near_ref[pl.ds(near_off, dk), :])

    v = []
    for g in range(N_KV):
        vg = imp[:, g * gc:g * gc + TQ]
        for r in range(1, rep):
            vg = vg + imp[:, g * gc + r * TQ:g * gc + (r + 1) * TQ]
        v.append(vg)
    v = jnp.concatenate(v, axis=1)
    j = lax.broadcasted_iota(jnp.int32, v.shape, 0)
    lane = lax.broadcasted_iota(jnp.int32, v.shape, 1)
    cur = per * it + (lane % TQ) // L_SEL
    forced = (j == 0) | (j == cur) | (j == cur - 1)
    v = jnp.where(forced, 1e9, jnp.where(j <= cur, v, -1e9))
    rank = _descending_rank(v, n_blk, n_live=per * (it + 1))
    selb = jnp.where((rank < top_n) & (j <= cur), 0.0, NEG)

    selb_ref[...] = per_group(jnp.where(j < per * diag0, selb, NEG))
    gk = FAR_CHUNKS * TQ
    for ref in (qaug_ref, qaug1_ref):
        ref[0:kvw, :] = qbd
        ref[kvw + 16:, :] = jnp.zeros((ref.shape[0] - kvw - 16, cols), BF16)

    def online_step(carry, s, keys):
        m, acc = carry
        m_new = jnp.maximum(m, jnp.max(s, axis=0, keepdims=True))
        p = jnp.exp2(s - m_new).astype(BF16)
        return m_new, jnp.exp2(m - m_new) * acc + pv(vst_ref, keys, p)

    def far_keys(cb):
        return pl.ds(pl.multiple_of(cb * gk, gk), gk)

    def far_scores(cb, q_ref2, s_out):
        sel = selb_ref[pl.ds(pl.multiple_of(cb * FAR_BLOCKS, FAR_BLOCKS), FAR_BLOCKS), :]
        q_ref2[kvw:kvw + 16, :] = jnp.concatenate([sel, jnp.zeros((16 - FAR_BLOCKS, cols), F32)], axis=0).astype(BF16)
        s_out[...] = jnp.dot(ks_ref[0, far_keys(cb), :], q_ref2[...], preferred_element_type=F32)

    def far_pair(jb, carry):
        far_scores(2 * jb, qaug_ref, s0_ref)
        far_scores(2 * jb + 1, qaug1_ref, s1_ref)
        carry = online_step(carry, s0_ref[...], far_keys(2 * jb))
        return online_step(carry, s1_ref[...], far_keys(2 * jb + 1))

    def far_single(cb, carry):
        far_scores(cb, qaug_ref, s0_ref)
        return online_step(carry, s0_ref[...], far_keys(cb))

    carry = (jnp.full((1, cols), NEG, F32), jnp.zeros((VT_ROWS, cols), F32))
    n_groups = (diag0 + FAR_CHUNKS - 1) // FAR_CHUNKS
    carry = lax.fori_loop(0, n_groups // 2, far_pair, carry)
    carry = lax.fori_loop(2 * (n_groups // 2), n_groups, far_single, carry)

    sd = sd_ref[...]
    rows = []
    for k in range(dk // L_SEL):
        rk = jnp.max(jnp.where(j == per * diag0 + k, selb, NEG), axis=0, keepdims=True)
        rows.append(jnp.broadcast_to(per_group(rk), (L_SEL, cols)))
    sd = sd + jnp.concatenate(rows, axis=0)
    _, acc_s = online_step(carry, sd, pl.ds(offd, dk))

    out_t = (gate[0] * oc + (gate[1] / acc_s[hd:hd + 1]) * acc_s[:hd]
             + (gate[2] / ow[hd:hd + 1]) * ow[:hd])
    pieces = []
    for c in range(0, N_KV * rep, 2):
        pair = jnp.concatenate([out_t[:, c * TQ:(c + 1) * TQ], out_t[:, (c + 1) * TQ:(c + 2) * TQ]], axis=0)
        pieces.append(pair.T)
    o_ref[0] = jnp.concatenate(pieces, axis=1).astype(BF16)


def _nsa_prompt(q, gt, kc, vct, ks, vst, kw, vwt, tcmp, near, cov_t, *, e0, top_n):
    nb, t, nsa_w = q.shape
    assert t % (2 * FAR_CHUNKS * TQ) == 0
    cols = (nsa_w // HEAD_DIM) * TQ
    per_b = lambda shape: pl.BlockSpec((1,) + shape, lambda b, i: (b, 0, 0))
    return pl.pallas_call(
        functools.partial(_nsa_prompt_body, e0=e0, top_n=top_n),
        grid=(nb, t // TQ),
        in_specs=[pl.BlockSpec((1, TQ, nsa_w), lambda b, i: (b, i, 0)),
                  pl.BlockSpec((1, N_KV * GATE_ROWS, TQ), lambda b, i: (b, 0, i)),
                  per_b(kc.shape[1:]), per_b(vct.shape[1:]),
                  per_b(ks.shape[1:]), per_b(vst.shape[1:]), per_b(kw.shape[1:]), per_b(vwt.shape[1:]),
                  _const_spec(tcmp.shape), _const_spec(near.shape), _const_spec(cov_t.shape)],
        out_specs=pl.BlockSpec((1, TQ, nsa_w), lambda b, i: (b, i, 0)),
        out_shape=jax.ShapeDtypeStruct((nb, t, nsa_w), BF16),
        scratch_shapes=[pltpu.VMEM((cov_t.shape[0], cols), F32),
                        pltpu.VMEM((ks.shape[2], cols), BF16), pltpu.VMEM((ks.shape[2], cols), BF16),
                        pltpu.VMEM((FAR_CHUNKS * TQ, cols), F32), pltpu.VMEM((FAR_CHUNKS * TQ, cols), F32),
                        pltpu.VMEM((WINDOW + TQ, cols), F32), pltpu.VMEM((2 * TQ, cols), F32)],
        compiler_params=_cparams(2),
        name="nsa_prompt",
    )(q, gt, kc, vct, ks, vst, kw, vwt, tcmp, near, cov_t)


def _dot_nt(a, b):
    return lax.dot_general(a, b, (((1,), (1,)), ((), ())), preferred_element_type=F32)


def _page_copies(pt_ref, pools, bufs, sems, batch, slot, dst):
    n_pages = pt_ref.shape[1]
    out = []
    for i, (pool, buf) in enumerate(zip(pools, bufs)):
        for p in range(n_pages):
            page = 0 if batch is None else pt_ref[batch, p]
            out.append(pltpu.make_async_copy(pool.at[page], dst(buf, slot, p), sems.at[i, slot]))
    return out


def _gather_step(pt_ref, pools, bufs, sems, dst):
    b = pl.program_id(0)
    last = pl.num_programs(0) - 1
    depth = bufs[0].shape[0]
    slot = lax.rem(b, depth)

    @pl.when(b == 0)
    def _():
        for ahead in range(depth - 1):
            for cp in _page_copies(pt_ref, pools, bufs, sems, jnp.minimum(ahead, last), ahead, dst):
                cp.start()

    for cp in _page_copies(pt_ref, pools, bufs, sems, None, slot, dst):
        cp.wait()
    nxt = jnp.minimum(b + depth - 1, last)
    for cp in _page_copies(pt_ref, pools, bufs, sems, nxt, lax.rem(b + depth - 1, depth), dst):
        cp.start()
    return slot


def _gather_drain(pt_ref, pools, bufs, sems, dst, slot):
    depth = bufs[0].shape[0]

    @pl.when(pl.program_id(0) == pl.num_programs(0) - 1)
    def _():
        for ahead in range(1, depth):
            for cp in _page_copies(pt_ref, pools, bufs, sems, None, lax.rem(slot + ahead, depth), dst):
                cp.wait()


def _position_bias(pos_ref, wexp_ref):
    half = wexp_ref.shape[1] // 2
    pp = jnp.dot(pos_ref[...], wexp_ref[...], preferred_element_type=F32)
    return pp[0:1, :half] + pp[1:2, half:]


def _regroup_pages(buf, slot, perm_ref, x_ref):
    n_pages = buf.shape[1]
    kvw = buf.shape[2]
    rpp = PAGE_SIZE // ROW_TOKENS
    nr = n_pages * rpp
    for i in range(n_pages // 2):
        pair = buf[slot, 2 * i:2 * i + 2].reshape(2 * kvw, PAGE_SIZE).astype(BF16)
        xt = _dot_nt(perm_ref[...], pair)
        for pp in range(2):
            for l in range(ROW_TOKENS):
                row0 = (l // 2) * nr + (2 * i + pp) * rpp
                x_ref[row0:row0 + rpp, (l % 2) * kvw:(l % 2 + 1) * kvw] = (
                    xt[l * rpp:(l + 1) * rpp, pp * kvw:(pp + 1) * kvw])


def _compress_contract(x_ref, wexp_ref):
    nr = x_ref.shape[0] // (ROW_TOKENS // 2)
    kvw = x_ref.shape[1] // 2
    p = None
    for i in range(ROW_TOKENS // 2):
        part = jnp.dot(x_ref[i * nr:(i + 1) * nr, :].astype(BF16), wexp_ref[2 * i * kvw:2 * (i + 1) * kvw, :],
                       preferred_element_type=F32)
        p = part if p is None else p + part
    return p


def _compress_finish(p, w2_ref, posb, sh_ref):
    nr = p.shape[0]
    half = p.shape[1] // 2
    sh_ref[0:nr, :] = p[:, half:]
    sh_ref[nr:nr + 8, :] = jnp.zeros((8, half), F32)
    h = p[:, :half] + sh_ref[1:nr + 1, :] + posb
    return jnp.dot(jax.nn.gelu(h).astype(BF16), w2_ref[...], preferred_element_type=F32).astype(BF16)


def _sample_cmp_body(pt_ref, kpool_ref, vpool_ref, qbd_ref, tab_ref, perm_ref, wk_ref, w2k_ref, pk_ref, wv_ref, w2v_ref,
                     pv_ref, cov_ref, oc_ref, imp_ref, kbuf, vbuf, sems, xk_ref, xv_ref, shk_ref, shv_ref, posb_ref,
                     *, rep, t_new):
    page_dst = lambda buf, sl, p: buf.at[sl, p]
    slot = _gather_step(pt_ref, (kpool_ref, vpool_ref), (kbuf, vbuf), sems, page_dst)

    @pl.when(pl.program_id(0) == 0)
    def _():
        posb_ref[0:1, :] = _position_bias(pk_ref, wk_ref)
        posb_ref[1:2, :] = _position_bias(pv_ref, wv_ref)

    _regroup_pages(kbuf, slot, perm_ref, xk_ref)
    _regroup_pages(vbuf, slot, perm_ref, xv_ref)
    pk = _compress_contract(xk_ref, wk_ref)
    pv = _compress_contract(xv_ref, wv_ref)
    kc = _compress_finish(pk, w2k_ref, posb_ref[0:1, :], shk_ref)
    vc = _compress_finish(pv, w2v_ref, posb_ref[1:2, :], shv_ref)
    s = _dot_nt(qbd_ref[0], kc) + tab_ref[...]
    m = jnp.max(s, axis=1, keepdims=True)
    p = jnp.where(s > 0.1 * NEG, jnp.exp2(s - m), 0.0)
    l = jnp.sum(p, axis=1, keepdims=True)
    pn = (p * jnp.where(l > 0.0, 1.0 / l, 0.0)).astype(BF16)
    oc_ref[0] = jnp.dot(pn, vc, preferred_element_type=F32)
    imp = jnp.dot(pn, cov_ref[...], preferred_element_type=F32)
    per_g = rep * t_new
    parts = []
    for g in range(N_KV):
        acc = imp[g * per_g:g * per_g + t_new]
        for r in range(1, rep):
            acc = acc + imp[g * per_g + r * t_new:g * per_g + (r + 1) * t_new]
        parts.append(acc)
    imp_ref[0] = jnp.concatenate(parts, axis=0)
    _gather_drain(pt_ref, (kpool_ref, vpool_ref), (kbuf, vbuf), sems, page_dst, slot)


def _sample_cmp(page_table, kpool_t, vpool_t, qbd, tab, cmp_k, cmp_v, cov, *, rep, t_new):
    nb, n_pages = page_table.shape
    rpp = PAGE_SIZE // ROW_TOKENS
    n_rows = n_pages * rpp
    n_q = qbd.shape[1]
    kvw = N_KV * HEAD_DIM
    tok = np.arange(PAGE_SIZE)
    perm = jnp.asarray((tok[None, :] == (tok[:, None] % rpp) * ROW_TOKENS + tok[:, None] // rpp).astype(np.float32), BF16)
    const = lambda a: pl.BlockSpec(a.shape, lambda b, pt: (0,) * a.ndim, pipeline_mode=pl.Buffered(1))
    grid_spec = pltpu.PrefetchScalarGridSpec(
        num_scalar_prefetch=1,
        grid=(nb,),
        in_specs=[pl.BlockSpec(memory_space=pl.ANY), pl.BlockSpec(memory_space=pl.ANY),
                  pl.BlockSpec((1, n_q, kvw), lambda b, pt: (b, 0, 0)), const(tab), const(perm),
                  const(cmp_k[0]), const(cmp_k[1]), const(cmp_k[2]), const(cmp_v[0]), const(cmp_v[1]), const(cmp_v[2]),
                  const(cov)],
        out_specs=[pl.BlockSpec((1, n_q, kvw), lambda b, pt: (b, 0, 0)),
                   pl.BlockSpec((1, N_KV * t_new, cov.shape[1]), lambda b, pt: (b, 0, 0))],
        scratch_shapes=[pltpu.VMEM((2, n_pages, kvw, PAGE_SIZE), F32), pltpu.VMEM((2, n_pages, kvw, PAGE_SIZE), F32),
                        pltpu.SemaphoreType.DMA((2, 2)),
                        pltpu.VMEM((n_pages * PAGE_SIZE // 2, 2 * kvw), F32),
                        pltpu.VMEM((n_pages * PAGE_SIZE // 2, 2 * kvw), F32),
                        pltpu.VMEM((n_rows + 8, kvw), F32), pltpu.VMEM((n_rows + 8, kvw), F32),
                        pltpu.VMEM((8, kvw), F32)],
    )
    return pl.pallas_call(
        functools.partial(_sample_cmp_body, rep=rep, t_new=t_new),
        grid_spec=grid_spec,
        out_shape=[jax.ShapeDtypeStruct((nb, n_q, kvw), F32),
                   jax.ShapeDtypeStruct((nb, N_KV * t_new, cov.shape[1]), F32)],
        compiler_params=_cparams(1),
        name="sample_cmp",
    )(page_table, kpool_t, vpool_t, qbd, tab, perm, *cmp_k, *cmp_v, cov)


def _select_body(imp_ref, o_ref, *, top_n, n_blk, pos0, t_new):
    rows, cols = imp_ref.shape
    j = lax.broadcasted_iota(jnp.int32, (rows, cols), 0)
    col = pl.program_id(0) * cols + lax.broadcasted_iota(jnp.int32, (rows, cols), 1)
    cur = (pos0 + col % t_new) // L_SEL
    forced = (j == 0) | (j == cur) | (j == cur - 1)
    v = jnp.where(forced, 1e9, jnp.where(j <= cur, imp_ref[...], -1e9))
    v = jnp.where(j < n_blk, v, -3e38)
    rank = _descending_rank(v, n_blk)
    o_ref[...] = jnp.where((rank < top_n) & (j <= cur) & (j < n_blk), 0.0, NEG)


def _select_blocks(imp_t, *, top_n, n_blk, pos0, t_new):
    rows, cols = imp_t.shape
    tile = 4 * LANE
    return pl.pallas_call(
        functools.partial(_select_body, top_n=top_n, n_blk=n_blk, pos0=pos0, t_new=t_new),
        grid=(cols // tile,),
        in_specs=[pl.BlockSpec((rows, tile), lambda i: (0, i))],
        out_specs=pl.BlockSpec((rows, tile), lambda i: (0, i)),
        out_shape=jax.ShapeDtypeStruct((rows, cols), F32),
        compiler_params=_cparams(1),
        name="sample_select",
    )(imp_t)


def _softmax_pv(parts):
    m = None
    for s, _, _ in parts:
        mk = jnp.max(s, axis=1, keepdims=True)
        m = mk if m is None else jnp.maximum(m, mk)
    l = None
    o = None
    for s, v, channel_major in parts:
        p = jnp.exp2(s - m)
        lk = jnp.sum(p, axis=1, keepdims=True)
        pb = p.astype(BF16)
        ok = _dot_nt(pb, v) if channel_major else jnp.dot(pb, v, preferred_element_type=F32)
        l = lk if l is None else l + lk
        o = ok if o is None else o + ok
    return o / l


def _sample_slc_body(pt_ref, kpool_ref, vpool_ref, qbd_ref, selb_ref, gate_ref, oc_ref, ksn_ref, vsn_ref, kwn_ref,
                     vwn_ref, kwin_ref, vwin_ref, near_ref, newtab_ref, wtab_ref, expand_ref,
                     o_ref, kbuf, vbuf, sems, pad_ref, *, rep, t_new):
    b = pl.program_id(0)
    page_dst = lambda buf, sl, p: buf.at[sl, :, pl.ds(p * PAGE_SIZE, PAGE_SIZE)]
    slot = _gather_step(pt_ref, (kpool_ref, vpool_ref), (kbuf, vbuf), sems, page_dst)
    hd = HEAD_DIM
    kvw = N_KV * hd

    @pl.when(b == 0)
    def _():
        pad_ref[...] = jnp.zeros(pad_ref.shape, F32)

    news = []
    for i, ref in enumerate((ksn_ref, vsn_ref, kwn_ref, vwn_ref)):
        pad_ref[i, 0:t_new, :] = ref[0]
        news.append(pad_ref[i].astype(BF16))
    ks_new, vs_new, kw_new, vw_new = news
    qbd = qbd_ref[0]

    per_g = rep * t_new
    sel = selb_ref[0]
    sel_rows = jnp.concatenate([sel[g * t_new:(g + 1) * t_new] for g in range(N_KV) for _ in range(rep)], axis=0)
    n_past_blk = expand_ref.shape[0]
    s_past = jnp.dot(jnp.concatenate([qbd, sel_rows[:, :n_past_blk].astype(BF16)], axis=1),
                     jnp.concatenate([kbuf[slot].astype(BF16), expand_ref[...]], axis=0),
                     preferred_element_type=F32)
    n_past = s_past.shape[1]
    n_near = near_ref.shape[1]
    s_past = jnp.concatenate([s_past[:, :n_past - n_near], s_past[:, n_past - n_near:] + near_ref[...]], axis=1)
    s_new = _dot_nt(qbd, ks_new) + newtab_ref[0]
    s_win = jnp.dot(qbd, kwin_ref[0].astype(BF16), preferred_element_type=F32) + wtab_ref[...]
    s_wnew = _dot_nt(qbd, kw_new) + newtab_ref[1]
    o_s = _softmax_pv([(s_past, vbuf[slot].astype(BF16), True), (s_new, vs_new, False)])
    o_w = _softmax_pv([(s_win, vwin_ref[0].astype(BF16), True), (s_wnew, vw_new, False)])

    gate = gate_ref[0]
    comb = gate[:, 0:1] * oc_ref[0] + gate[:, 1:2] * o_s + gate[:, 2:3] * o_w
    row = lax.broadcasted_iota(jnp.int32, comb.shape, 0)
    o_ref[0] = jnp.where(row < per_g, comb, pltpu.roll(comb, hd, axis=1))
    _gather_drain(pt_ref, (kpool_ref, vpool_ref), (kbuf, vbuf), sems, page_dst, slot)


def _sample_slc(page_table, kpool, vpool, qbd, selb, gate_rows, oc, ks_new, vs_new, kw_new, vw_new, kwin, vwin,
                near, newtab, wtab, expand, *, rep, t_new):
    nb, n_pages = page_table.shape
    n_keys = n_pages * PAGE_SIZE
    n_q = qbd.shape[1]
    kvw = N_KV * HEAD_DIM
    const = lambda a: pl.BlockSpec(a.shape, lambda b, pt: (0,) * a.ndim, pipeline_mode=pl.Buffered(1))
    per_b = lambda a: pl.BlockSpec((1,) + a.shape[1:], lambda b, pt: (b,) + (0,) * (a.ndim - 1))
    grid_spec = pltpu.PrefetchScalarGridSpec(
        num_scalar_prefetch=1,
        grid=(nb,),
        in_specs=[pl.BlockSpec(memory_space=pl.ANY), pl.BlockSpec(memory_space=pl.ANY),
                  per_b(qbd), per_b(selb), per_b(gate_rows), per_b(oc),
                  per_b(ks_new), per_b(vs_new), per_b(kw_new), per_b(vw_new), per_b(kwin), per_b(vwin),
                  const(near), const(newtab), const(wtab), const(expand)],
        out_specs=pl.BlockSpec((1, n_q, kvw), lambda b, pt: (b, 0, 0)),
        scratch_shapes=[pltpu.VMEM((GATHER_DEPTH, kvw, n_keys), F32), pltpu.VMEM((GATHER_DEPTH, kvw, n_keys), F32),
                        pltpu.SemaphoreType.DMA((2, GATHER_DEPTH)), pltpu.VMEM((4, LANE, kvw), F32)],
    )
    return pl.pallas_call(
        functools.partial(_sample_slc_body, rep=rep, t_new=t_new),
        grid_spec=grid_spec,
        out_shape=jax.ShapeDtypeStruct((nb, n_q, kvw), F32),
        compiler_params=_cparams(1),
        name="sample_slc_win",
    )(page_table, kpool, vpool, qbd, selb, gate_rows, oc, ks_new, vs_new, kw_new, vw_new, kwin, vwin,
      near, newtab, wtab, expand)


def _sample_tables(rel_bias, pos0, t_new, n_rows, n_c, wbuf):
    h = rel_bias.shape[0]
    qpos = pos0 + jnp.arange(t_new, dtype=jnp.int32)

    def table(kpos, ok_extra=None, upper=None):
        d = qpos[:, None] - kpos[None, :]
        ok = d >= 0
        if upper is not None:
            ok = ok & (d < upper)
        if ok_extra is not None:
            ok = ok & ok_extra[None, :]
        tab = jnp.where(ok[None], _bias_minus_far(rel_bias, d, 0), NEG)
        return tab.reshape(h * t_new, kpos.shape[0]).astype(F32)

    c = jnp.arange(n_rows, dtype=jnp.int32)
    cmp = table(c * STRIDE + (L_CMP - 1), ok_extra=c < n_c)
    near = table(pos0 - PAGE_SIZE + jnp.arange(PAGE_SIZE, dtype=jnp.int32))
    lane = jnp.arange(LANE, dtype=jnp.int32)
    new_slc = table(pos0 + lane, ok_extra=lane < t_new)
    new_win = table(pos0 + lane, ok_extra=lane < t_new, upper=WINDOW)
    kpos_w = pos0 - wbuf + jnp.arange(wbuf, dtype=jnp.int32)
    win = table(kpos_w, ok_extra=kpos_w >= 0, upper=WINDOW)
    return cmp, near, jnp.stack([new_slc, new_win]), win


def _layer_weights(p, n_heads):
    d = p["w_in"].shape[0]
    kv_end = d + 6 * N_KV * HEAD_DIM
    src = _gate_column_source(n_heads)
    gate_cols = jnp.where(jnp.asarray(src >= 0)[None, :], p["w_in"][:, kv_end + np.maximum(src, 0)], 0.0)
    ln = jnp.stack([p["ln_gain"][0], p["ln_bias"][0], p["ln_gain"][1], p["ln_bias"][1],
                    p["ln_gain"][2], p["ln_bias"][2]])
    return dict(
        wi1=p["w_ffn1_in"].astype(BF16), wo1=p["w_ffn1_out"].astype(BF16),
        wi2=p["w_ffn2_in"].astype(BF16), wo2=p["w_ffn2_out"].astype(BF16),
        wp=jnp.concatenate([p["w_in"][:, :kv_end], gate_cols], axis=1).astype(BF16),
        wout=p["w_out"].astype(BF16),
        wpool=p["w_pool"].astype(BF16), pool_scale=p["pool_scale"],
        ln1=ln[0:2], ln23=ln[2:6],
        cmp_k=_compress_weights(p["w_phi_k1"], p["w_phi_k2"], p["cmp_pos"]),
        cmp_v=_compress_weights(p["w_phi_v1"], p["w_phi_v2"], p["cmp_pos"]),
    )


PROMPT_TOKENS = 512


def _prompt_layer(x, mod, w, rel_bias, alpha):
    nb, t, d = x.shape
    assert t % TQ == 0 and t % PROMPT_TOKENS == 0
    (x1, u_pool, q, k_c, v_c, k_s, v_s, k_w, v_w, ks_bf, vs_t, kw_bf, vw_t, gates_t) = _ffn_in(
        x, mod, w["wi1"], w["wo1"], w["wp"], w["ln1"], alpha=alpha, nbk=1, tt=PROMPT_TOKENS, for_prompt=True)
    pool_out = _pool_mix(u_pool, None, w["wpool"], w["pool_scale"], pos0=0, nbk=1, tt=PROMPT_TOKENS,
                         name="pool_prompt")
    kc_cmp = _compress(k_c, *w["cmp_k"], transposed=False, name="compress_k_prompt")
    vc_cmp_t = _compress(v_c, *w["cmp_v"], transposed=True, name="compress_v_prompt")
    n_c = (t - L_CMP) // STRIDE + 1
    n_blk = t // L_SEL
    near, tcmp, e0 = _prompt_tables(rel_bias, t)
    nsa = _nsa_prompt(q, gates_t, kc_cmp, vc_cmp_t, ks_bf, vs_t, kw_bf, vw_t, tcmp, near,
                      _cover_t(t // ROW_TOKENS, n_c, n_blk), e0=e0, top_n=min(TOP_N, n_blk))
    y = _ffn_out(pool_out, nsa, x1, mod, w["wout"], w["wi2"], w["wo2"], w["ln23"], alpha=alpha,
                 nbk=1, tt=PROMPT_TOKENS, name="outproj_ffn2_prompt")
    keep = min(WINDOW, t)
    heads = lambda a: a.reshape(nb, -1, N_KV, HEAD_DIM)
    states = (heads(k_c), heads(v_c), heads(k_s), heads(v_s), heads(k_w[:, t - keep:]), heads(v_w[:, t - keep:]),
              u_pool[:, t - POOL_STATE:])
    return y, states


SAMPLE_BATCHES = 32


def _sample_layer(x, mod, w, rel_bias, alpha, page_table, pools, kwin, vwin, pool_state):
    nb, t, d = x.shape
    hd = HEAD_DIM
    n_heads = rel_bias.shape[0]
    rep = n_heads // N_KV
    kvw = N_KV * hd
    pos0 = page_table.shape[1] * PAGE_SIZE
    n_c = (pos0 + t - L_CMP) // STRIDE + 1
    assert (n_c - 1) * STRIDE + L_CMP <= pos0 and pos0 % PAGE_SIZE == 0 and nb % SAMPLE_BATCHES == 0
    n_rows = pos0 // ROW_TOKENS
    n_blk = -(-(pos0 + t) // L_SEL)
    assert n_blk - 1 == pos0 // L_SEL
    blk_pad = -(-n_blk // LANE) * LANE
    wbuf = kwin.shape[1]

    x1, u_pool, q, k_c, v_c, k_s, v_s, k_w, v_w, gates = _ffn_in(
        x, mod, w["wi1"], w["wo1"], w["wp"], w["ln1"], alpha=alpha, nbk=SAMPLE_BATCHES, tt=t, for_prompt=False)
    pw = u_pool.shape[-1]
    prev = jnp.concatenate([jnp.zeros((nb, PREV_ROWS - POOL_STATE, pw), F32), pool_state], axis=1)
    pool_out = _pool_mix(u_pool, prev, w["wpool"], w["pool_scale"], pos0=pos0, nbk=SAMPLE_BATCHES, tt=t,
                         name="pool_sample")

    qh = q.reshape(nb, t, N_KV, rep, hd).transpose(0, 2, 3, 1, 4)
    own = jnp.eye(N_KV, dtype=q.dtype)[None, :, None, None, :, None]
    qbd = (qh[:, :, :, :, None, :] * own).reshape(nb, N_KV * rep * t, kvw)
    gate_rows = gates[:, :, :N_KV * GATE_ROWS].reshape(nb, t, N_KV, GATE_ROWS)[..., :3 * rep]
    gate_rows = gate_rows.reshape(nb, t, N_KV, 3, rep).transpose(0, 2, 4, 1, 3).reshape(nb, N_KV * rep * t, 3)
    gate_rows = jnp.pad(gate_rows, ((0, 0), (0, 0), (0, LANE - 3)))

    tab_c, near, newtab, wtab = _sample_tables(rel_bias, pos0, t, n_rows, n_c, wbuf)
    cov = np.zeros((n_rows, blk_pad), np.float32)
    ci = np.arange(n_rows)[:, None]
    bj = np.arange(blk_pad)[None, :]
    cov[:] = ((ci * STRIDE <= (bj + 1) * L_SEL - 1) & (ci * STRIDE + L_CMP - 1 >= bj * L_SEL)
              & (ci < n_c) & (bj < n_blk))
    channel_major = lambda a: a.transpose(0, 2, 3, 1).reshape(a.shape[0], kvw, a.shape[1])
    oc, imp = _sample_cmp(page_table, channel_major(pools[0]), channel_major(pools[1]), qbd, tab_c, w["cmp_k"],
                          w["cmp_v"], jnp.asarray(cov, BF16), rep=rep, t_new=t)

    sel_rows = -(-n_blk // 8) * 8
    imp_t = imp.transpose(2, 0, 1).reshape(blk_pad, nb * N_KV * t)[:sel_rows]
    selb_t = _select_blocks(imp_t, top_n=min(TOP_N, n_blk), n_blk=n_blk, pos0=pos0, t_new=t)
    selb_t = jnp.pad(selb_t, ((0, blk_pad - sel_rows), (0, 0)), constant_values=NEG)
    selb = selb_t.reshape(blk_pad, nb, N_KV * t).transpose(1, 2, 0)

    key_blk = lax.broadcasted_iota(jnp.int32, (pos0 // L_SEL, pos0), 1) // L_SEL
    expand = (key_blk == lax.broadcasted_iota(jnp.int32, (pos0 // L_SEL, pos0), 0)).astype(BF16)
    kvf = lambda a: a.reshape(a.shape[0], a.shape[1], kvw)
    o = _sample_slc(page_table, channel_major(pools[2]), channel_major(pools[3]), qbd, selb, gate_rows, oc,
                    k_s, v_s, k_w, v_w, channel_major(kwin), channel_major(vwin), near, newtab, wtab, expand,
                    rep=rep, t_new=t)
    nsa = o[:, :, :hd].reshape(nb, N_KV, rep, t, hd).transpose(0, 3, 1, 2, 4).reshape(nb, t, n_heads * hd).astype(BF16)

    y = _ffn_out(pool_out, nsa, x1, mod, w["wout"], w["wi2"], w["wo2"], w["ln23"], alpha=alpha,
                 nbk=SAMPLE_BATCHES, tt=t, name="outproj_ffn2_sample")
    keep = min(WINDOW, pos0 + t)
    heads = lambda a: a.reshape(nb, -1, N_KV, HEAD_DIM)
    kw_ext = jnp.concatenate([kvf(kwin), k_w], axis=1)
    vw_ext = jnp.concatenate([kvf(vwin), v_w], axis=1)
    pool_ext = jnp.concatenate([pool_state, u_pool], axis=1)
    states = (heads(k_c), heads(v_c), heads(k_s), heads(v_s), heads(kw_ext[:, kw_ext.shape[1] - keep:]),
              heads(vw_ext[:, vw_ext.shape[1] - keep:]), pool_ext[:, pool_ext.shape[1] - POOL_STATE:])
    return y, states


def kernel(x_prompt, x_sample, c_prompt, c_sample, cache_k_cmp, cache_v_cmp, cache_k_slc, cache_v_slc, page_table,
           state_k_win, state_v_win, state_pool, w_ada, b_ada, ln_gain, ln_bias, w_ffn1_in, w_ffn1_out, w_ffn2_in,
           w_ffn2_out, w_in, w_phi_k1, w_phi_k2, w_phi_v1, w_phi_v2, cmp_pos, w_pool, pool_scale, w_out, rel_bias):
    depth = w_ada.shape[0]
    alpha = (2.0 * depth) ** 0.25
    n_heads = rel_bias.shape[0]
    nb_p, nb_s = x_prompt.shape[0], x_sample.shape[0]
    d = x_prompt.shape[-1]
    xp, xs = x_prompt, x_sample
    c_all = jnp.concatenate([c_prompt, c_sample], axis=0)
    p_states, s_states = [], []
    for l in range(depth):
        p = dict(w_ada=w_ada[l], b_ada=b_ada[l], ln_gain=ln_gain[l], ln_bias=ln_bias[l], w_ffn1_in=w_ffn1_in[l],
                 w_ffn1_out=w_ffn1_out[l], w_ffn2_in=w_ffn2_in[l], w_ffn2_out=w_ffn2_out[l], w_in=w_in[l],
                 w_phi_k1=w_phi_k1[l], w_phi_k2=w_phi_k2[l], w_phi_v1=w_phi_v1[l], w_phi_v2=w_phi_v2[l],
                 cmp_pos=cmp_pos[l], w_pool=w_pool[l], pool_scale=pool_scale[l], w_out=w_out[l])
        w = _layer_weights(p, n_heads)
        mod = _ada(c_all, p["w_ada"], p["b_ada"]).reshape(nb_p + nb_s, 9, d)
        xp, sp = _prompt_layer(xp, mod[:nb_p], w, rel_bias, alpha)
        xs, ss = _sample_layer(xs, mod[nb_p:], w, rel_bias, alpha, page_table,
                               (cache_k_cmp[l], cache_v_cmp[l], cache_k_slc[l], cache_v_slc[l]),
                               state_k_win[l], state_v_win[l], state_pool[l])
        p_states.append(sp)
        s_states.append(ss)
    stack = lambda states: tuple(jnp.stack(a) for a in zip(*states))
    return (xp, xs) + stack(p_states) + stack(s_states)
```
